```python
import jax, jax.numpy as jnp
from jax import lax
import numpy as np

D_MODEL = 1024
BATCH = 16
SEQ = 2048
DEPTH = 4

N_A_LAYERS = DEPTH // 2
N_B_LAYERS = DEPTH - N_A_LAYERS
PLE_DIM = 256
NORM_EPS = 1e-6

DN_HEADS = 8
DN_DK = 128
DN_DV = 128
DN_CONV = 4
DN_CHUNK = 64
DN_QK = DN_HEADS * DN_DK
DN_V = DN_HEADS * DN_DV
DN_IN = 2 * DN_QK + 2 * DN_V + 2 * DN_HEADS

DIL_CONFIGS = ((128, 1), (512, 4), (2048, 16))
N_ATT_GROUPS = len(DIL_CONFIGS)
HEAD_DIM = 128
Q_PER_GROUP = 4
KV_PER_GROUP = 2
Q_REP = Q_PER_GROUP // KV_PER_GROUP
ATT_BLOCK = 128
ALIBI_MAX = 8.0
Q_WIDTH = N_ATT_GROUPS * Q_PER_GROUP * HEAD_DIM
KV_WIDTH = N_ATT_GROUPS * 2 * KV_PER_GROUP * HEAD_DIM
ATT_OUT = Q_PER_GROUP * HEAD_DIM

MOE_GROUPS = 4
MOE_EXPERTS_PER_GROUP = 8
MOE_EXPERTS = MOE_GROUPS * MOE_EXPERTS_PER_GROUP
MOE_TOPK = 2
MOE_HIDDEN = 512
MOE_BLOCK = 128

kernel_name = 'yoco_deltanet_dilated_hmoe'


def rms_norm(x, g):
    xf = x.astype(jnp.float32)
    y = xf * lax.rsqrt(jnp.mean(xf * xf, axis=-1, keepdims=True) + NORM_EPS)
    return (y * g.astype(jnp.float32)).astype(x.dtype)


def l2_norm(x):
    return x * lax.rsqrt(jnp.sum(x * x, axis=-1, keepdims=True) + NORM_EPS)


def alibi_slopes():
    n = N_ATT_GROUPS * Q_PER_GROUP
    s = 2.0 ** (-ALIBI_MAX * np.arange(1, n + 1) / n)
    return s.reshape(N_ATT_GROUPS, KV_PER_GROUP, Q_REP).astype(np.float32)


def causal_depthwise_conv(x, w):
    k, c = w.shape
    return lax.conv_general_dilated(x, w[:, None, :].astype(x.dtype), window_strides=(1,),
                                    padding=((k - 1, 0),), dimension_numbers=('NWC', 'WIO', 'NWC'),
                                    feature_group_count=c)


def _to_chunks(t, n_chunks):
    b, s, h = t.shape[:3]
    t = t.reshape((b, n_chunks, s // n_chunks, h) + t.shape[3:])
    return jnp.moveaxis(t, 3, 1)


def chunk_gated_delta_rule(q, k, v, beta, g_log):
    b, s, h, dk = q.shape
    dv = v.shape[-1]
    n = s // DN_CHUNK
    q, k, v = _to_chunks(q, n), _to_chunks(k, n), _to_chunks(v, n)
    beta, g = _to_chunks(beta, n), _to_chunks(g_log, n)
    g = jnp.cumsum(g, axis=-1)
    idx = jnp.arange(DN_CHUNK)
    causal = idx[:, None] >= idx[None, :]
    strict = idx[:, None] > idx[None, :]
    decay = jnp.exp(jnp.where(causal, g[..., :, None] - g[..., None, :], -jnp.inf))
    kk = jnp.einsum('bhnid,bhnjd->bhnij', k, k)
    a = jnp.where(strict, beta[..., :, None] * kk * decay, 0.0) + jnp.eye(DN_CHUNK, dtype=jnp.float32)
    rhs = jnp.concatenate([beta[..., None] * v, (beta * jnp.exp(g))[..., None] * k], axis=-1)
    sol = lax.linalg.triangular_solve(a, rhs, left_side=True, lower=True, unit_diagonal=True)
    u_t, w = sol[..., :dv], sol[..., dv:]
    attn = jnp.einsum('bhnid,bhnjd->bhnij', q, k) * decay
    q_g = q * jnp.exp(g)[..., None]
    g_last = g[..., -1]
    k_d = k * jnp.exp(g_last[..., None] - g)[..., None]

    def step(state, xs):
        u_tc, w_c, attn_c, qg_c, kd_c, gl_c = xs
        u = u_tc - jnp.einsum('bhcd,bhde->bhce', w_c, state)
        o = jnp.einsum('bhcd,bhde->bhce', qg_c, state) + jnp.einsum('bhij,bhje->bhie', attn_c, u)
        state = jnp.exp(gl_c)[..., None, None] * state + jnp.einsum('bhcd,bhce->bhde', kd_c, u)
        return state, o

    tr = lambda t: jnp.moveaxis(t, 2, 0)
    state0 = jnp.zeros((b, h, dk, dv), jnp.float32)
    _, o = lax.scan(step, state0, (tr(u_t), tr(w), tr(attn), tr(q_g), tr(k_d), tr(g_last)))
    return o.transpose(1, 0, 3, 2, 4).reshape(b, s, h, dv)


def gated_deltanet(xn, w_in, conv_w, a_log, dt_bias, o_norm, w_out):
    b, s, _ = xn.shape
    proj = xn @ w_in
    qkv = proj[..., :2 * DN_QK + DN_V]
    z = proj[..., 2 * DN_QK + DN_V:2 * DN_QK + 2 * DN_V]
    b_pre = proj[..., 2 * DN_QK + 2 * DN_V:2 * DN_QK + 2 * DN_V + DN_HEADS]
    a_pre = proj[..., 2 * DN_QK + 2 * DN_V + DN_HEADS:]
    qkv = jax.nn.silu(causal_depthwise_conv(qkv, conv_w))
    q = l2_norm(qkv[..., :DN_QK].reshape(b, s, DN_HEADS, DN_DK).astype(jnp.float32)) * (DN_DK ** -0.5)
    k = l2_norm(qkv[..., DN_QK:2 * DN_QK].reshape(b, s, DN_HEADS, DN_DK).astype(jnp.float32))
    v = qkv[..., 2 * DN_QK:].reshape(b, s, DN_HEADS, DN_DV).astype(jnp.float32)
    beta = jax.nn.sigmoid(b_pre.astype(jnp.float32))
    g_log = -jnp.exp(a_log.astype(jnp.float32)) * jax.nn.softplus(a_pre.astype(jnp.float32) + dt_bias.astype(jnp.float32))
    o = chunk_gated_delta_rule(q, k, v, beta, g_log)
    o = rms_norm(o, o_norm) * jax.nn.silu(z.reshape(b, s, DN_HEADS, DN_DV).astype(jnp.float32))
    return o.reshape(b, s, DN_V).astype(xn.dtype) @ w_out


def key_blocks(t, dil):
    b, s, hk, hd = t.shape
    l = s // dil
    nblk = -(-l // ATT_BLOCK)
    lp = nblk * ATT_BLOCK
    t = t.reshape(b, l, dil, hk, hd).transpose(0, 2, 3, 1, 4)
    t = jnp.pad(t, ((0, 0), (0, 0), (0, 0), (0, lp - l), (0, 0))).reshape(b, dil, hk, nblk, ATT_BLOCK, hd)
    prev = jnp.pad(t, ((0, 0), (0, 0), (0, 0), (1, 0), (0, 0), (0, 0)))[:, :, :, :-1]
    return jnp.concatenate([prev, t], axis=4)


def query_blocks(q, dil):
    b, s, hk, r, hd = q.shape
    l = s // dil
    nblk = -(-l // ATT_BLOCK)
    lp = nblk * ATT_BLOCK
    q = q.reshape(b, l, dil, hk, r, hd).transpose(0, 2, 3, 4, 1, 5)
    q = jnp.pad(q, ((0, 0), (0, 0), (0, 0), (0, 0), (0, lp - l), (0, 0)))
    return q.reshape(b, dil, hk, r, nblk, ATT_BLOCK, hd)


def dilated_branch(q, kb, vb, window, dil, slopes):
    b, s, hk, r, hd = q.shape
    l = s // dil
    nblk = kb.shape[3]
    lp = nblk * ATT_BLOCK
    qb = query_blocks(q, dil)
    sc = jnp.einsum('bdgrnqh,bdgnkh->bdgrnqk', qb, kb).astype(jnp.float32) * (hd ** -0.5)
    qi = jnp.arange(ATT_BLOCK)[:, None]
    ki = jnp.arange(2 * ATT_BLOCK)[None, :]
    dist = ATT_BLOCK + qi - ki
    key_pos = (jnp.arange(nblk) * ATT_BLOCK - ATT_BLOCK)[:, None, None] + ki[None]
    valid = (dist >= 0) & (dist <= window // dil) & (key_pos >= 0)
    bias = -jnp.asarray(slopes)[:, :, None, None, None] * (dil * dist).astype(jnp.float32)
    sc = jnp.where(valid, sc + bias, -jnp.inf)
    lse = jax.nn.logsumexp(sc, axis=-1)
    p = jnp.exp(sc - lse[..., None])
    o = jnp.einsum('bdgrnqk,bdgnkh->bdgrnqh', p.astype(vb.dtype), vb)
    o = o.reshape(b, dil, hk, r, lp, hd)[:, :, :, :, :l].transpose(0, 4, 1, 2, 3, 5).reshape(b, s, hk * r, hd)
    lse = lse.reshape(b, dil, hk, r, lp)[..., :l].transpose(0, 4, 1, 2, 3).reshape(b, s, hk * r)
    return o, lse


def shared_dilated_kv(h, kv_norm, w_kv, k_norm):
    b, s, _ = h.shape
    kv = (rms_norm(h, kv_norm) @ w_kv).reshape(b, s, N_ATT_GROUPS, 2, KV_PER_GROUP, HEAD_DIM)
    blocks = []
    for g, (window, dil) in enumerate(DIL_CONFIGS):
        k = rms_norm(kv[:, :, g, 0], k_norm[g])
        v = kv[:, :, g, 1]
        blocks.append((key_blocks(k, dil), key_blocks(v, dil)))
    return blocks


def dilated_attention(xn, w_q, q_norm, w_out, kv_blocks):
    b, s, _ = xn.shape
    q = (xn @ w_q).reshape(b, s, N_ATT_GROUPS, KV_PER_GROUP, Q_REP, HEAD_DIM)
    slopes = alibi_slopes()
    outs, lses = [], []
    for g, (window, dil) in enumerate(DIL_CONFIGS):
        qg = rms_norm(q[:, :, g], q_norm[g])
        o, lse = dilated_branch(qg, kv_blocks[g][0], kv_blocks[g][1], window, dil, slopes[g])
        outs.append(o)
        lses.append(lse)
    o = jnp.stack(outs).astype(jnp.float32)
    wts = jax.nn.softmax(jnp.stack(lses), axis=0)
    o = jnp.sum(wts[..., None] * o, axis=0)
    return o.reshape(b, s, ATT_OUT).astype(xn.dtype) @ w_out


def grouped_expert_mlp(xt, eid, ew, w1, w3, w2):
    n_tok, d = xt.shape
    n_exp = w1.shape[0]
    flat_e = eid.reshape(-1)
    flat_t = jnp.arange(flat_e.shape[0]) // eid.shape[1]
    flat_w = ew.reshape(-1)
    m = flat_e.shape[0]
    order = jnp.argsort(flat_e)
    se, st, sw = flat_e[order], flat_t[order], flat_w[order]
    counts = jnp.bincount(flat_e, length=n_exp)
    starts = jnp.cumsum(counts) - counts
    padded = (counts + MOE_BLOCK - 1) // MOE_BLOCK * MOE_BLOCK
    pad_ends = jnp.cumsum(padded)
    pad_starts = pad_ends - padded
    dest = pad_starts[se] + jnp.arange(m) - starts[se]
    n_blocks = -(-m // MOE_BLOCK) + n_exp
    rows = n_blocks * MOE_BLOCK
    row_tok = jnp.full((rows,), n_tok, jnp.int32).at[dest].set(st.astype(jnp.int32))
    row_w = jnp.zeros((rows,), jnp.float32).at[dest].set(sw)
    block_e = jnp.minimum(jnp.searchsorted(pad_ends, jnp.arange(n_blocks) * MOE_BLOCK, side='right'), n_exp - 1)
    x_rows = jnp.concatenate([xt, jnp.zeros((1, d), xt.dtype)])[row_tok].reshape(n_blocks, MOE_BLOCK, d)

    def block_mlp(args):
        xb, e = args
        hdn = jax.nn.silu(xb @ w1[e]) * (xb @ w3[e])
        return hdn @ w2[e]

    y = lax.map(block_mlp, (x_rows, block_e)).reshape(rows, d)
    out = jnp.zeros((n_tok + 1, d), y.dtype).at[row_tok].add(y * row_w[:, None].astype(y.dtype))
    return out[:n_tok]


def hierarchical_moe(xn, w_rg, b_rg, w_re, b_re, w1, w3, w2):
    b, s, d = xn.shape
    xt = xn.reshape(-1, d)
    n = xt.shape[0]
    xf = xt.astype(jnp.float32)
    gp = jax.nn.softmax(xf @ w_rg.astype(jnp.float32) + b_rg.astype(jnp.float32), axis=-1)
    g_idx = jnp.argmax(gp, axis=-1)
    g_w = jnp.take_along_axis(gp, g_idx[:, None], axis=-1)
    el = (xf @ w_re.astype(jnp.float32) + b_re.astype(jnp.float32)).reshape(n, MOE_GROUPS, MOE_EXPERTS_PER_GROUP)
    el = jnp.take_along_axis(el, g_idx[:, None, None], axis=1)[:, 0]
    top_p, top_i = lax.top_k(jax.nn.softmax(el, axis=-1), MOE_TOPK)
    wts = g_w * top_p / jnp.sum(top_p, axis=-1, keepdims=True)
    eid = g_idx[:, None] * MOE_EXPERTS_PER_GROUP + top_i
    return grouped_expert_mlp(xt, eid, wts, w1, w3, w2).reshape(b, s, d)


def per_layer_embedding(h, p_i, g_norm, w_gate, w_proj):
    gate = jax.nn.sigmoid((rms_norm(h, g_norm) @ w_gate).astype(jnp.float32))
    return (gate * (p_i @ w_proj).astype(jnp.float32)).astype(h.dtype)


def setup_inputs(seed: int = 0) -> dict:
    key = jax.random.key(seed)
    ks = jax.random.split(key, 32)
    f32 = jnp.float32
    nrm = lambda k, shape, sc: jax.random.normal(k, shape, f32) * sc
    gain = lambda k, shape: 1.0 + 0.05 * jax.random.normal(k, shape, f32)
    dt = jnp.exp(jax.random.uniform(ks[5], (N_A_LAYERS, DN_HEADS), f32, np.log(1e-3), np.log(1e-1)))
    return {
        'x': nrm(ks[0], (BATCH, SEQ, D_MODEL), 1.0),
        'p': nrm(ks[1], (DEPTH, BATCH, SEQ, PLE_DIM), 1.0),
        'a_norm': gain(ks[2], (N_A_LAYERS, D_MODEL)),
        'a_w_in': nrm(ks[3], (N_A_LAYERS, D_MODEL, DN_IN), D_MODEL ** -0.5),
        'a_conv': nrm(ks[4], (N_A_LAYERS, DN_CONV, 2 * DN_QK + DN_V), DN_CONV ** -0.5),
        'a_A_log': jnp.log(jax.random.uniform(ks[6], (N_A_LAYERS, DN_HEADS), f32, 1.0, 16.0)),
        'a_dt_bias': dt + jnp.log(-jnp.expm1(-dt)),
        'a_o_norm': gain(ks[7], (N_A_LAYERS, DN_DV)),
        'a_w_out': nrm(ks[8], (N_A_LAYERS, DN_V, D_MODEL), DN_V ** -0.5),
        'kv_norm': gain(ks[9], (D_MODEL,)),
        'w_kv': nrm(ks[10], (D_MODEL, KV_WIDTH), D_MODEL ** -0.5),
        'k_norm': gain(ks[11], (N_ATT_GROUPS, HEAD_DIM)),
        'b_norm': gain(ks[12], (N_B_LAYERS, D_MODEL)),
        'b_w_q': nrm(ks[13], (N_B_LAYERS, D_MODEL, Q_WIDTH), D_MODEL ** -0.5),
        'b_q_norm': gain(ks[14], (N_B_LAYERS, N_ATT_GROUPS, HEAD_DIM)),
        'b_w_out': nrm(ks[15], (N_B_LAYERS, ATT_OUT, D_MODEL), ATT_OUT ** -0.5),
        'ffn_norm': gain(ks[16], (DEPTH, D_MODEL)),
        'w_router_group': nrm(ks[17], (DEPTH, D_MODEL, MOE_GROUPS), D_MODEL ** -0.5),
        'b_router_group': nrm(ks[18], (DEPTH, MOE_GROUPS), 0.01),
        'w_router_expert': nrm(ks[19], (DEPTH, D_MODEL, MOE_EXPERTS), D_MODEL ** -0.5),
        'b_router_expert': nrm(ks[20], (DEPTH, MOE_EXPERTS), 0.01),
        'w1': nrm(ks[21], (DEPTH, MOE_EXPERTS, D_MODEL, MOE_HIDDEN), D_MODEL ** -0.5),
        'w3': nrm(ks[22], (DEPTH, MOE_EXPERTS, D_MODEL, MOE_HIDDEN), D_MODEL ** -0.5),
        'w2': nrm(ks[23], (DEPTH, MOE_EXPERTS, MOE_HIDDEN, D_MODEL), MOE_HIDDEN ** -0.5),
        'ple_norm': gain(ks[24], (DEPTH, D_MODEL)),
        'w_ple_gate': nrm(ks[25], (DEPTH, D_MODEL, D_MODEL), D_MODEL ** -0.5),
        'w_ple_proj': nrm(ks[26], (DEPTH, PLE_DIM, D_MODEL), PLE_DIM ** -0.5),
    }


def reference(x, p, a_norm, a_w_in, a_conv, a_A_log, a_dt_bias, a_o_norm, a_w_out,
              kv_norm, w_kv, k_norm, b_norm, b_w_q, b_q_norm, b_w_out,
              ffn_norm, w_router_group, b_router_group, w_router_expert, b_router_expert,
              w1, w3, w2, ple_norm, w_ple_gate, w_ple_proj):
    h = x
    kv_blocks = None
    for i in range(DEPTH):
        if i < N_A_LAYERS:
            a = i
            h = h + gated_deltanet(rms_norm(h, a_norm[a]), a_w_in[a], a_conv[a], a_A_log[a],
                                   a_dt_bias[a], a_o_norm[a], a_w_out[a])
        else:
            bl = i - N_A_LAYERS
            h = h + dilated_attention(rms_norm(h, b_norm[bl]), b_w_q[bl], b_q_norm[bl], b_w_out[bl], kv_blocks)
        h = h + hierarchical_moe(rms_norm(h, ffn_norm[i]), w_router_group[i], b_router_group[i],
                                 w_router_expert[i], b_router_expert[i], w1[i], w3[i], w2[i])
        h = h + per_layer_embedding(h, p[i], ple_norm[i], w_ple_gate[i], w_ple_proj[i])
        if i == N_A_LAYERS - 1:
            kv_blocks = shared_dilated_kv(h, kv_norm, w_kv, k_norm)
    return h
```

```python
import functools

import numpy as np
import jax
import jax.numpy as jnp
from jax import lax
from jax.experimental import pallas as pl
from jax.experimental.pallas import tpu as pltpu

F32 = jnp.float32
BF16 = jnp.bfloat16

NORM_EPS = 1e-6

DN_HEADS = 8
DN_DK = 128
DN_DV = 128
DN_CONV = 4
DN_CHUNK = 64
DN_QK = DN_HEADS * DN_DK
DN_V = DN_HEADS * DN_DV
DN_MAIN = 2 * DN_QK + 2 * DN_V

DIL_CONFIGS = ((128, 1), (512, 4), (2048, 16))
N_ATT_GROUPS = len(DIL_CONFIGS)
HEAD_DIM = 128
Q_PER_GROUP = 4
KV_PER_GROUP = 2
Q_REP = Q_PER_GROUP // KV_PER_GROUP
ATT_BLOCK = 128
ALIBI_MAX = 8.0
GROUP_WIDTH = Q_PER_GROUP * HEAD_DIM

MOE_GROUPS = 4
MOE_EPG = 8
MOE_EXPERTS = MOE_GROUPS * MOE_EPG
MOE_HIDDEN = 512
MOE_BM = 256
ROUTER_LANES = 128
EXP_LANE0 = MOE_GROUPS

LANES = 128
VMEM_LIMIT = 48 * 1024 * 1024


def _alibi_slopes():
    n = N_ATT_GROUPS * Q_PER_GROUP
    s = 2.0 ** (-ALIBI_MAX * np.arange(1, n + 1) / n)
    return s.reshape(N_ATT_GROUPS, KV_PER_GROUP, Q_REP)


def _rms(x, g):
    ms = jnp.mean(x * x, axis=-1, keepdims=True)
    return x * lax.rsqrt(ms + NORM_EPS) * g


def _dot(a, b):
    return jnp.dot(a.astype(BF16), b.astype(BF16), preferred_element_type=F32)


def _dot_nt(a, b):
    return lax.dot_general(a.astype(BF16), b.astype(BF16), (((1,), (1,)), ((), ())),
                           preferred_element_type=F32)


def _dot_tn(a, b):
    return lax.dot_general(a.astype(BF16), b.astype(BF16), (((0,), (0,)), ((), ())),
                           preferred_element_type=F32)


def _split2(x):
    hi = x.astype(BF16)
    lo = (x - hi.astype(F32)).astype(BF16)
    return hi, lo


def _split3(x):
    hi = x.astype(BF16)
    r = x - hi.astype(F32)
    mid = r.astype(BF16)
    lo = (r - mid.astype(F32)).astype(BF16)
    return hi, mid, lo


def _dot_exact01(x, sel):
    hi, mid, lo = _split3(x)
    d = lambda p: jnp.dot(p, sel, preferred_element_type=F32)
    return d(hi) + d(mid) + d(lo)


def _sigmoid(x):
    return 1.0 / (1.0 + jnp.exp(-x))


def _silu(x):
    return x * _sigmoid(x)


def _nm_kernel(x_ref, g_ref, w_ref, o_ref, xn_ref):
    @pl.when(pl.program_id(1) == 0)
    def _():
        xn_ref[...] = _rms(x_ref[...], g_ref[...]).astype(BF16)

    o_ref[...] = jnp.dot(xn_ref[...], w_ref[...], preferred_element_type=F32).astype(o_ref.dtype)


def norm_matmul(x, g, w, out_dtype, tm=512, tn=512):
    n, k = x.shape
    m = w.shape[1]
    return pl.pallas_call(
        _nm_kernel,
        grid=(n // tm, m // tn),
        in_specs=[pl.BlockSpec((tm, k), lambda i, j: (i, 0)),
                  pl.BlockSpec((1, k), lambda i, j: (0, 0)),
                  pl.BlockSpec((k, tn), lambda i, j: (0, j))],
        out_specs=pl.BlockSpec((tm, tn), lambda i, j: (i, j)),
        out_shape=jax.ShapeDtypeStruct((n, m), out_dtype),
        scratch_shapes=[pltpu.VMEM((tm, k), BF16)],
        compiler_params=pltpu.CompilerParams(dimension_semantics=("parallel", "arbitrary"),
                                             vmem_limit_bytes=VMEM_LIMIT),
        name="norm_matmul",
    )(x, g.reshape(1, k), w)


def _dn_inproj_kernel(x_ref, g_ref, w_ref, wgt_ref, o_ref, gt_ref, xn_ref):
    @pl.when(pl.program_id(1) == 0)
    def _():
        xn = _rms(x_ref[...], g_ref[...])
        xh, xl = _split2(xn)
        xn_ref[...] = xh
        wh, wl = _split2(wgt_ref[...])
        gt_ref[...] = _dot_nt(wh, xh) + _dot_nt(wh, xl) + _dot_nt(wl, xh)

    o_ref[...] = jnp.dot(xn_ref[...], w_ref[...], preferred_element_type=F32).astype(o_ref.dtype)


def dn_inproj(x, g, w_main, w_gates_t, tm=512, tn=512):
    n, k = x.shape
    m = w_main.shape[1]
    ng = w_gates_t.shape[0]
    return pl.pallas_call(
        _dn_inproj_kernel,
        grid=(n // tm, m // tn),
        in_specs=[pl.BlockSpec((tm, k), lambda i, j: (i, 0)),
                  pl.BlockSpec((1, k), lambda i, j: (0, 0)),
                  pl.BlockSpec((k, tn), lambda i, j: (0, j)),
                  pl.BlockSpec((ng, k), lambda i, j: (0, 0))],
        out_specs=[pl.BlockSpec((tm, tn), lambda i, j: (i, j)),
                   pl.BlockSpec((ng, tm), lambda i, j: (0, i))],
        out_shape=[jax.ShapeDtypeStruct((n, m), BF16), jax.ShapeDtypeStruct((ng, n), F32)],
        scratch_shapes=[pltpu.VMEM((tm, k), BF16)],
        compiler_params=pltpu.CompilerParams(dimension_semantics=("parallel", "arbitrary"),
                                             vmem_limit_bytes=VMEM_LIMIT),
        name="dn_inproj",
    )(x, g.reshape(1, k), w_main, w_gates_t)


def _mm_res_kernel(a_ref, w_ref, r_ref, o_ref):
    o_ref[...] = r_ref[...] + jnp.dot(a_ref[...], w_ref[...], preferred_element_type=F32)


def matmul_residual(a, w, res, tm=512, tn=512):
    n, k = a.shape
    m = w.shape[1]
    return pl.pallas_call(
        _mm_res_kernel,
        grid=(n // tm, m // tn),
        in_specs=[pl.BlockSpec((tm, k), lambda i, j: (i, 0)),
                  pl.BlockSpec((k, tn), lambda i, j: (0, j)),
                  pl.BlockSpec((tm, tn), lambda i, j: (i, j))],
        out_specs=pl.BlockSpec((tm, tn), lambda i, j: (i, j)),
        out_shape=jax.ShapeDtypeStruct((n, m), F32),
        compiler_params=pltpu.CompilerParams(dimension_semantics=("parallel", "parallel"),
                                             vmem_limit_bytes=VMEM_LIMIT),
        name="matmul_residual",
    )(a, w, res)


DN_PIECE = 256
DN_HALO = 16


def _deltanet_kernel(alog_ref, dtb_ref, q_ref, k_ref, v_ref, z_ref, cq_ref, ck_ref, cv_ref,
                     bpre_ref, apre_ref, onorm_ref, o_ref, qs_ref, ks_ref, vs_ref, beta_ref, gcum_ref):
    seq = q_ref.shape[1]
    c = DN_CHUNK
    n_chunks = seq // c
    head = pl.program_id(1)

    def conv_piece(x_ref, w_ref, p):
        w = w_ref[...]
        if p == 0:
            x = x_ref[0, 0:DN_PIECE, :].astype(F32)
            halo = 0
        else:
            x = x_ref[0, p * DN_PIECE - DN_HALO:(p + 1) * DN_PIECE, :].astype(F32)
            halo = DN_HALO
        row = lax.broadcasted_iota(jnp.int32, x.shape, 0)
        acc = x * w[DN_CONV - 1:DN_CONV, :]
        for j in range(1, DN_CONV):
            xs = jnp.where(row >= j, pltpu.roll(x, j, axis=0), 0.0)
            acc = acc + xs * w[DN_CONV - 1 - j:DN_CONV - j, :]
        return _silu(acc[halo:, :])

    def l2n(x):
        return x * lax.rsqrt(jnp.sum(x * x, axis=-1, keepdims=True) + NORM_EPS)

    for p in range(seq // DN_PIECE):
        rows = slice(p * DN_PIECE, (p + 1) * DN_PIECE)
        qs_ref[rows, :] = l2n(conv_piece(q_ref, cq_ref, p)) * (DN_DK ** -0.5)
        ks_ref[rows, :] = l2n(conv_piece(k_ref, ck_ref, p))
        vs_ref[rows, :] = conv_piece(v_ref, cv_ref, p)

    beta_ref[...] = _sigmoid(bpre_ref[0, 0])
    a = apre_ref[0, 0] + dtb_ref[head]
    softplus = jnp.maximum(a, 0.0) + jnp.log(1.0 + jnp.exp(-jnp.abs(a)))
    g_log = -jnp.exp(jnp.full(a.shape, alog_ref[head], F32)) * softplus
    ki = lax.broadcasted_iota(jnp.int32, (c, c), 0)
    ji = lax.broadcasted_iota(jnp.int32, (c, c), 1)
    upper = jnp.where(ki <= ji, 1.0, 0.0).astype(BF16)
    gcum_ref[...] = _dot_exact01(g_log, upper)

    ones_cd = jnp.ones((c, DN_DV), BF16)
    eye = ki == ji
    causal = ki >= ji
    strict = ki > ji
    onorm = onorm_ref[...]

    def chunk_step(ci, state):
        r0 = pl.multiple_of(ci * c, c)
        qc = qs_ref[pl.ds(r0, c), :]
        kc = ks_ref[pl.ds(r0, c), :]
        vc = vs_ref[pl.ds(r0, c), :]
        beta_row = beta_ref[pl.ds(ci, 1), :]
        g_row = gcum_ref[pl.ds(ci, 1), :]
        g_j = jnp.broadcast_to(g_row, (c, c))
        diag = jnp.concatenate([jnp.where(eye, jnp.broadcast_to(beta_row, (c, c)), 0.0),
                                jnp.where(eye, g_j, 0.0)], axis=0)
        cols = _dot_exact01(diag, ones_cd)
        beta_c = cols[:c, :]
        g_c = cols[c:, :]
        decay = jnp.exp(jnp.where(causal, g_c[:, :c] - g_j, -jnp.inf))
        kq = _dot_nt(jnp.concatenate([kc, qc], axis=0), kc)
        m = jnp.where(strict, -(beta_c[:, :c] * kq[:c, :] * decay), 0.0)
        r = m
        pw = m
        for _ in range(5):
            pw = _dot(pw, pw)
            r = r + pw + _dot(r, pw)
        e_g = jnp.exp(g_c)
        rhs = jnp.concatenate([beta_c * vc, beta_c * e_g * kc], axis=1)
        sol = rhs + _dot(r, rhs)
        u_t = sol[:, :DN_DV]
        w = sol[:, DN_DV:]
        attn = jnp.where(causal, kq[c:, :] * decay, 0.0)
        q_g = qc * e_g
        g_last = jnp.broadcast_to(g_c[c - 1:c, :], (c, DN_DV))
        k_d = kc * jnp.exp(g_last - g_c)
        ws = _dot(jnp.concatenate([w, q_g], axis=0), state)
        u = u_t - ws[:c, :]
        o = ws[c:, :] + _dot(attn, u)
        state = jnp.exp(jnp.broadcast_to(g_c[c - 1:c, :], (DN_DK, DN_DV))) * state + _dot_tn(k_d, u)
        zc = z_ref[0, pl.ds(r0, c), :].astype(F32)
        o_ref[0, pl.ds(r0, c), :] = (_rms(o, onorm) * _silu(zc)).astype(o_ref.dtype)
        return state

    lax.fori_loop(0, n_chunks, chunk_step, jnp.zeros((DN_DK, DN_DV), F32))


def deltanet(proj, gates_t, conv_w, a_log, dt_bias, o_norm):
    b, s, _ = proj.shape
    nh = DN_HEADS
    nc = s // DN_CHUNK
    col = lambda off: pl.BlockSpec((1, s, DN_DK), lambda bi, hi: (bi, 0, off + hi))
    cw = lambda off: pl.BlockSpec((DN_CONV, DN_DK), lambda bi, hi: (0, off + hi))
    gate = lambda off: pl.BlockSpec((1, 1, nc, DN_CHUNK), lambda bi, hi: (off + hi, bi, 0, 0))
    smem = pl.BlockSpec(memory_space=pltpu.SMEM)
    return pl.pallas_call(
        _deltanet_kernel,
        grid=(b, nh),
        in_specs=[smem, smem, col(0), col(nh), col(2 * nh), col(3 * nh), cw(0), cw(nh), cw(2 * nh),
                  gate(0), gate(nh), pl.BlockSpec((1, DN_DV), lambda bi, hi: (0, 0))],
        out_specs=pl.BlockSpec((1, s, DN_DV), lambda bi, hi: (bi, 0, hi)),
        out_shape=jax.ShapeDtypeStruct((b, s, DN_V), BF16),
        scratch_shapes=[pltpu.VMEM((s, DN_DK), F32), pltpu.VMEM((s, DN_DK), F32), pltpu.VMEM((s, DN_DV), F32),
                        pltpu.VMEM((nc, DN_CHUNK), F32), pltpu.VMEM((nc, DN_CHUNK), F32)],
        compiler_params=pltpu.CompilerParams(dimension_semantics=("parallel", "parallel"),
                                             vmem_limit_bytes=VMEM_LIMIT),
        name="deltanet",
    )(a_log, dt_bias, proj, proj, proj, proj, conv_w, conv_w, conv_w, gates_t, gates_t, o_norm.reshape(1, DN_DV))


ATT_PIECE = 256
ATT_M_INIT = -1e30


def _attention_kernel(q_ref, kv_ref, qn_ref, kn_ref, o_ref, qf_ref, kf_ref, vf_ref, acc_ref, m_ref, l_ref):
    seq = q_ref.shape[1]
    grp = pl.program_id(1)
    hd = HEAD_DIM
    blk = ATT_BLOCK
    slopes = _alibi_slopes()

    qg = qn_ref[0] * (hd ** -0.5)
    kg = kn_ref[0]

    def prep(pi, carry):
        r0 = pl.multiple_of(pi * ATT_PIECE, ATT_PIECE)
        rows = pl.ds(r0, ATT_PIECE)
        for j in range(Q_PER_GROUP):
            cols = slice(j * hd, (j + 1) * hd)
            qf_ref[j, rows, :] = _rms(q_ref[0, rows, cols].astype(F32), 1.0) * qg
        for j in range(KV_PER_GROUP):
            cols = slice(j * hd, (j + 1) * hd)
            kf_ref[j, rows, :] = _rms(kv_ref[0, rows, cols].astype(F32), 1.0) * kg
            vcols = slice((KV_PER_GROUP + j) * hd, (KV_PER_GROUP + j + 1) * hd)
            vf_ref[j, rows, :] = kv_ref[0, rows, vcols].astype(F32)
        return carry

    lax.fori_loop(0, seq // ATT_PIECE, prep, 0)

    @pl.when(grp == 0)
    def _():
        def init(pi, carry):
            r0 = pl.multiple_of(pi * ATT_PIECE, ATT_PIECE)
            rows = pl.ds(r0, ATT_PIECE)
            for j in range(Q_PER_GROUP):
                acc_ref[j, rows, :] = jnp.zeros((ATT_PIECE, hd), F32)
                l_ref[j, rows, :] = jnp.zeros((ATT_PIECE, hd), F32)
                m_ref[j, rows, :] = jnp.full((ATT_PIECE, hd), ATT_M_INIT, F32)
            return carry

        lax.fori_loop(0, seq // ATT_PIECE, init, 0)

    def rows_of(start, size, dil):
        return pl.ds(start, size) if dil == 1 else pl.ds(start, size, stride=dil)

    def attend(g, dil, q_start, k_start, nk):
        qrows = rows_of(q_start, blk, dil)
        krows = rows_of(k_start, nk, dil)
        qi = lax.broadcasted_iota(jnp.int32, (blk, nk), 0)
        kidx = lax.broadcasted_iota(jnp.int32, (blk, nk), 1)
        dist = (nk - blk) + qi - kidx
        valid = (dist >= 0) & (dist <= blk)
        distf = dist.astype(F32)
        for kvh in range(KV_PER_GROUP):
            kb = kf_ref[kvh, krows, :].astype(BF16)
            vb = vf_ref[kvh, krows, :].astype(BF16)
            for rep in range(Q_REP):
                j = kvh * Q_REP + rep
                sc = _dot_nt(qf_ref[j, qrows, :], kb)
                sc = jnp.where(valid, sc - float(slopes[g, kvh, rep] * dil) * distf, -jnp.inf)
                m_old = m_ref[j, qrows, :]
                m_new = jnp.maximum(m_old, jnp.max(sc, axis=-1, keepdims=True))
                alpha = jnp.exp(m_old - m_new)
                p = jnp.exp(sc - m_new[:, 0:1])
                l_ref[j, qrows, :] = alpha * l_ref[j, qrows, :] + jnp.sum(p, axis=-1, keepdims=True)
                acc_ref[j, qrows, :] = alpha * acc_ref[j, qrows, :] + jnp.dot(p.astype(BF16), vb,
                                                                              preferred_element_type=F32)
                m_ref[j, qrows, :] = m_new

    for g, (window, dil) in enumerate(DIL_CONFIGS):
        sub_len = seq // dil
        nblk = sub_len // blk

        @pl.when(grp == g)
        def _(g=g, dil=dil, nblk=nblk):
            def residue(res, carry):
                attend(g, dil, res, res, blk)
                if nblk > 1:
                    def later(n, c2):
                        attend(g, dil, res + n * blk * dil, res + (n - 1) * blk * dil, 2 * blk)
                        return c2
                    lax.fori_loop(1, nblk, later, 0)
                return carry

            lax.fori_loop(0, dil, residue, 0)

    @pl.when(grp == N_ATT_GROUPS - 1)
    def _():
        def finish(pi, carry):
            r0 = pl.multiple_of(pi * ATT_PIECE, ATT_PIECE)
            rows = pl.ds(r0, ATT_PIECE)
            for j in range(Q_PER_GROUP):
                cols = slice(j * hd, (j + 1) * hd)
                o_ref[0, rows, cols] = (acc_ref[j, rows, :] / l_ref[j, rows, :]).astype(o_ref.dtype)
            return carry

        lax.fori_loop(0, seq // ATT_PIECE, finish, 0)


def dilated_attention(q, kv, q_norm, k_norm):
    b, s, _ = q.shape
    gw = GROUP_WIDTH
    return pl.pallas_call(
        _attention_kernel,
        grid=(b, N_ATT_GROUPS),
        in_specs=[pl.BlockSpec((1, s, gw), lambda bi, gi: (bi, 0, gi)),
                  pl.BlockSpec((1, s, gw), lambda bi, gi: (bi, 0, gi)),
                  pl.BlockSpec((1, 1, HEAD_DIM), lambda bi, gi: (gi, 0, 0)),
                  pl.BlockSpec((1, 1, HEAD_DIM), lambda bi, gi: (gi, 0, 0))],
        out_specs=pl.BlockSpec((1, s, gw), lambda bi, gi: (bi, 0, 0)),
        out_shape=jax.ShapeDtypeStruct((b, s, gw), BF16),
        scratch_shapes=[pltpu.VMEM((Q_PER_GROUP, s, HEAD_DIM), F32),
                        pltpu.VMEM((KV_PER_GROUP, s, HEAD_DIM), F32),
                        pltpu.VMEM((KV_PER_GROUP, s, HEAD_DIM), F32),
                        pltpu.VMEM((Q_PER_GROUP, s, HEAD_DIM), F32),
                        pltpu.VMEM((Q_PER_GROUP, s, HEAD_DIM), F32),
                        pltpu.VMEM((Q_PER_GROUP, s, HEAD_DIM), F32)],
        compiler_params=pltpu.CompilerParams(dimension_semantics=("parallel", "arbitrary"),
                                             vmem_limit_bytes=VMEM_LIMIT),
        name="dilated_attention",
    )(q, kv, q_norm.reshape(N_ATT_GROUPS, 1, HEAD_DIM), k_norm.reshape(N_ATT_GROUPS, 1, HEAD_DIM))


ROUTER_TM = 512
META_E0, META_E1, META_W0, META_W1, META_R0, META_R1 = 0, 1, 2, 3, 4, 5


def _router_kernel(h_ref, g_ref, w_ref, b_ref, xn_ref, meta_ref, cnt_ref, carry_ref):
    tm = h_ref.shape[0]

    @pl.when(pl.program_id(0) == 0)
    def _():
        carry_ref[...] = jnp.zeros_like(carry_ref)

    xn = _rms(h_ref[...], g_ref[...])
    xn_ref[...] = xn
    xh, xl = _split2(xn)
    wh, wl = _split2(w_ref[...])
    d = lambda a, bb: jnp.dot(a, bb, preferred_element_type=F32)
    logits = d(xh, wh) + d(xh, wl) + d(xl, wh) + b_ref[...]

    lane = lax.broadcasted_iota(jnp.int32, (tm, ROUTER_LANES), 1)
    big = jnp.int32(ROUTER_LANES)
    first_where = lambda cond: jnp.min(jnp.where(cond, lane, big), axis=-1, keepdims=True)

    gl = jnp.where(lane < MOE_GROUPS, logits, -jnp.inf)
    ge = jnp.exp(gl - jnp.max(gl, axis=-1, keepdims=True))
    gp = ge / jnp.sum(ge, axis=-1, keepdims=True)
    g_w = jnp.max(gp, axis=-1, keepdims=True)
    g_idx = first_where(gp == g_w)

    lo = EXP_LANE0 + g_idx * MOE_EPG
    in_group = (lane >= lo) & (lane < lo + MOE_EPG)
    el = jnp.where(in_group, logits, -jnp.inf)
    ee = jnp.exp(el - jnp.max(el, axis=-1, keepdims=True))
    ep = ee / jnp.sum(ee, axis=-1, keepdims=True)
    p0 = jnp.max(jnp.where(in_group, ep, -1.0), axis=-1, keepdims=True)
    i0 = first_where(in_group & (ep == p0))
    rest = in_group & (lane != i0)
    p1 = jnp.max(jnp.where(rest, ep, -1.0), axis=-1, keepdims=True)
    i1 = first_where(rest & (ep == p1))
    w0 = g_w * p0 / (p0 + p1)
    w1 = g_w * p1 / (p0 + p1)

    oh0 = jnp.where(lane == i0, 1.0, 0.0)
    oh1 = jnp.where(lane == i1, 1.0, 0.0)
    both = oh0 + oh1
    ti = lax.broadcasted_iota(jnp.int32, (tm, tm), 0)
    tj = lax.broadcasted_iota(jnp.int32, (tm, tm), 1)
    before = jnp.where(tj < ti, 1.0, 0.0).astype(BF16)
    seen = carry_ref[...] + jnp.dot(before, both.astype(BF16), preferred_element_type=F32)
    r0 = jnp.sum(seen * oh0, axis=-1, keepdims=True)
    r1 = jnp.sum(seen * oh1, axis=-1, keepdims=True)
    total = carry_ref[...] + jnp.sum(both, axis=0, keepdims=True)
    carry_ref[...] = total
    cnt_ref[...] = jnp.broadcast_to(total, cnt_ref.shape)

    e0 = (i0 - EXP_LANE0).astype(F32)
    e1 = (i1 - EXP_LANE0).astype(F32)
    meta = jnp.zeros((tm, ROUTER_LANES), F32)
    for idx, val in ((META_E0, e0), (META_E1, e1), (META_W0, w0), (META_W1, w1), (META_R0, r0), (META_R1, r1)):
        meta = jnp.where(lane == idx, val, meta)
    meta_ref[...] = meta


def moe_router(h, g, w_router, b_router):
    n, k = h.shape
    tm = ROUTER_TM
    return pl.pallas_call(
        _router_kernel,
        grid=(n // tm,),
        in_specs=[pl.BlockSpec((tm, k), lambda i: (i, 0)),
                  pl.BlockSpec((1, k), lambda i: (0, 0)),
                  pl.BlockSpec((k, ROUTER_LANES), lambda i: (0, 0)),
                  pl.BlockSpec((1, ROUTER_LANES), lambda i: (0, 0))],
        out_specs=[pl.BlockSpec((tm, k), lambda i: (i, 0)),
                   pl.BlockSpec((tm, ROUTER_LANES), lambda i: (i, 0)),
                   pl.BlockSpec((8, ROUTER_LANES), lambda i: (0, 0))],
        out_shape=[jax.ShapeDtypeStruct((n, k), F32),
                   jax.ShapeDtypeStruct((n, ROUTER_LANES), F32),
                   jax.ShapeDtypeStruct((8, ROUTER_LANES), F32)],
        scratch_shapes=[pltpu.VMEM((1, ROUTER_LANES), F32)],
        compiler_params=pltpu.CompilerParams(dimension_semantics=("arbitrary",),
                                             vmem_limit_bytes=VMEM_LIMIT),
        name="moe_router",
    )(h, g.reshape(1, k), w_router, b_router)


def _row_copy(src_hbm, row, dst_vmem, slot, sem):
    return pltpu.make_async_copy(src_hbm.at[pl.ds(row, 1), :], dst_vmem.at[pl.ds(slot, 1), :], sem)


def _gather_rows(idx_ref, src_hbm, dst_vmem, sem, count):
    def start(r, carry):
        _row_copy(src_hbm, idx_ref[r], dst_vmem, r, sem).start()
        return carry

    def wait(r, carry):
        _row_copy(src_hbm, 0, dst_vmem, r, sem).wait()
        return carry

    lax.fori_loop(0, count, start, 0)
    lax.fori_loop(0, count, wait, 0)


def _expert_kernel(be_ref, nvalid_ref, tok_ref, x_hbm, w1_ref, w3_ref, w2_ref, y_ref, xbuf_ref, sem):
    i = pl.program_id(0)
    bm = y_ref.shape[0]

    @pl.when(i < nvalid_ref[0])
    def _():
        _gather_rows(tok_ref.at[0, 0], x_hbm, xbuf_ref, sem, bm)
        x = xbuf_ref[...].astype(BF16)
        h1 = jnp.dot(x, w1_ref[0], preferred_element_type=F32)
        h3 = jnp.dot(x, w3_ref[0], preferred_element_type=F32)
        hdn = (_silu(h1) * h3).astype(BF16)
        y_ref[...] = jnp.dot(hdn, w2_ref[0], preferred_element_type=F32)

    @pl.when(i >= nvalid_ref[0])
    def _():
        y_ref[...] = jnp.zeros_like(y_ref)


def moe_experts(xn, row_tok, block_e, n_valid, w1, w3, w2):
    n, d = xn.shape
    n_blocks = block_e.shape[0]
    bm = MOE_BM
    hid = w1.shape[2]
    grid_spec = pltpu.PrefetchScalarGridSpec(
        num_scalar_prefetch=2,
        grid=(n_blocks,),
        in_specs=[pl.BlockSpec((1, 1, bm), lambda i, be, nv: (i, 0, 0), memory_space=pltpu.SMEM),
                  pl.BlockSpec(memory_space=pl.ANY),
                  pl.BlockSpec((1, d, hid), lambda i, be, nv: (be[i], 0, 0)),
                  pl.BlockSpec((1, d, hid), lambda i, be, nv: (be[i], 0, 0)),
                  pl.BlockSpec((1, hid, d), lambda i, be, nv: (be[i], 0, 0))],
        out_specs=pl.BlockSpec((bm, d), lambda i, be, nv: (i, 0)),
        scratch_shapes=[pltpu.VMEM((bm, d), F32), pltpu.SemaphoreType.DMA(())],
    )
    return pl.pallas_call(
        _expert_kernel,
        grid_spec=grid_spec,
        out_shape=jax.ShapeDtypeStruct((n_blocks * bm, d), F32),
        compiler_params=pltpu.CompilerParams(dimension_semantics=("arbitrary",),
                                             vmem_limit_bytes=VMEM_LIMIT),
        name="moe_experts",
    )(block_e, n_valid, row_tok.reshape(n_blocks, 1, bm), xn, w1, w3, w2)


COMBINE_TM = 256


def _combine_kernel(dest_ref, y_hbm, h_ref, meta_ref, o_ref, ybuf_ref, sem):
    tm = h_ref.shape[0]
    _gather_rows(dest_ref.at[0, 0], y_hbm, ybuf_ref, sem, 2 * tm)
    meta = meta_ref[...]
    w0 = meta[:, META_W0:META_W0 + 1]
    w1 = meta[:, META_W1:META_W1 + 1]
    o_ref[...] = h_ref[...] + (w0 * ybuf_ref[0:tm, :] + w1 * ybuf_ref[tm:2 * tm, :])


def moe_combine(y, dest, h, meta):
    n, d = h.shape
    tm = COMBINE_TM
    return pl.pallas_call(
        _combine_kernel,
        grid=(n // tm,),
        in_specs=[pl.BlockSpec((1, 1, 2 * tm), lambda i: (i, 0, 0), memory_space=pltpu.SMEM),
                  pl.BlockSpec(memory_space=pl.ANY),
                  pl.BlockSpec((tm, d), lambda i: (i, 0)),
                  pl.BlockSpec((tm, ROUTER_LANES), lambda i: (i, 0))],
        out_specs=pl.BlockSpec((tm, d), lambda i: (i, 0)),
        out_shape=jax.ShapeDtypeStruct((n, d), F32),
        scratch_shapes=[pltpu.VMEM((2 * tm, d), F32), pltpu.SemaphoreType.DMA(())],
        compiler_params=pltpu.CompilerParams(dimension_semantics=("arbitrary",),
                                             vmem_limit_bytes=VMEM_LIMIT),
        name="moe_combine",
    )(dest, y, h, meta)


def hierarchical_moe(h, ffn_norm, w_rg, b_rg, w_re, b_re, w1, w3, w2):
    n, d = h.shape
    pad = ROUTER_LANES - MOE_GROUPS - MOE_EXPERTS
    w_router = jnp.concatenate([w_rg, w_re, jnp.zeros((d, pad), F32)], axis=1)
    b_router = jnp.concatenate([b_rg, b_re, jnp.zeros((pad,), F32)]).reshape(1, ROUTER_LANES)
    xn, meta, cnt = moe_router(h, ffn_norm, w_router, b_router)

    bm = MOE_BM
    counts = cnt[0, EXP_LANE0:EXP_LANE0 + MOE_EXPERTS].astype(jnp.int32)
    padded = (counts + bm - 1) // bm * bm
    pad_ends = jnp.cumsum(padded)
    pad_starts = pad_ends - padded
    n_blocks = (2 * n) // bm + MOE_EXPERTS
    block_e = jnp.minimum(jnp.searchsorted(pad_ends, jnp.arange(n_blocks, dtype=jnp.int32) * bm, side='right'),
                          MOE_EXPERTS - 1).astype(jnp.int32)
    n_valid = (pad_ends[-1:] // bm).astype(jnp.int32)
    eid = meta[:, META_E0:META_E1 + 1].astype(jnp.int32)
    rank = meta[:, META_R0:META_R1 + 1].astype(jnp.int32)
    dest = pad_starts[eid] + rank
    tok = jnp.broadcast_to(jnp.arange(n, dtype=jnp.int32)[:, None], (n, 2))
    row_tok = jnp.zeros((n_blocks * bm,), jnp.int32).at[dest.reshape(-1)].set(tok.reshape(-1))

    y = moe_experts(xn, row_tok, block_e, n_valid, w1, w3, w2)
    tm = COMBINE_TM
    dest_tiles = dest.reshape(n // tm, tm, 2).transpose(0, 2, 1).reshape(n // tm, 1, 2 * tm)
    return moe_combine(y, dest_tiles, h, meta)


def _ple_kernel(h_ref, hres_ref, g_ref, wg_ref, p_ref, wp_ref, o_ref, xn_ref, pb_ref):
    @pl.when(pl.program_id(1) == 0)
    def _():
        xn_ref[...] = _rms(h_ref[...], g_ref[...]).astype(BF16)
        pb_ref[...] = p_ref[...].astype(BF16)

    gate = _sigmoid(jnp.dot(xn_ref[...], wg_ref[...], preferred_element_type=F32))
    proj = jnp.dot(pb_ref[...], wp_ref[...], preferred_element_type=F32)
    o_ref[...] = hres_ref[...] + gate * proj


def per_layer_embedding(h, p_i, g, w_gate, w_proj, tm=512, tn=512):
    n, d = h.shape
    pd = p_i.shape[1]
    return pl.pallas_call(
        _ple_kernel,
        grid=(n // tm, d // tn),
        in_specs=[pl.BlockSpec((tm, d), lambda i, j: (i, 0)),
                  pl.BlockSpec((tm, tn), lambda i, j: (i, j)),
                  pl.BlockSpec((1, d), lambda i, j: (0, 0)),
                  pl.BlockSpec((d, tn), lambda i, j: (0, j)),
                  pl.BlockSpec((tm, pd), lambda i, j: (i, 0)),
                  pl.BlockSpec((pd, tn), lambda i, j: (0, j))],
        out_specs=pl.BlockSpec((tm, tn), lambda i, j: (i, j)),
        out_shape=jax.ShapeDtypeStruct((n, d), F32),
        scratch_shapes=[pltpu.VMEM((tm, d), BF16), pltpu.VMEM((tm, pd), BF16)],
        compiler_params=pltpu.CompilerParams(dimension_semantics=("parallel", "arbitrary"),
                                             vmem_limit_bytes=VMEM_LIMIT),
        name="per_layer_embedding",
    )(h, h, g.reshape(1, d), w_gate, p_i, w_proj)


def kernel(x, p, a_norm, a_w_in, a_conv, a_A_log, a_dt_bias, a_o_norm, a_w_out, kv_norm, w_kv, k_norm, b_norm, b_w_q, b_q_norm, b_w_out, ffn_norm, w_router_group, b_router_group, w_router_expert, b_router_expert, w1, w3, w2, ple_norm, w_ple_gate, w_ple_proj):
    b, s, d = x.shape
    n = b * s
    depth = p.shape[0]
    n_a = a_norm.shape[0]
    h = x.reshape(n, d)
    kv = None
    for i in range(depth):
        if i < n_a:
            w_in = a_w_in[i]
            proj, gates_t = dn_inproj(h, a_norm[i], w_in[:, :DN_MAIN].astype(BF16), w_in[:, DN_MAIN:].T)
            o = deltanet(proj.reshape(b, s, DN_MAIN), gates_t.reshape(2 * DN_HEADS, b, s // DN_CHUNK, DN_CHUNK),
                         a_conv[i], a_A_log[i], a_dt_bias[i], a_o_norm[i])
            h = matmul_residual(o.reshape(n, DN_V), a_w_out[i].astype(BF16), h)
        else:
            bl = i - n_a
            q = norm_matmul(h, b_norm[bl], b_w_q[bl].astype(BF16), BF16)
            o = dilated_attention(q.reshape(b, s, -1), kv.reshape(b, s, -1), b_q_norm[bl], k_norm)
            h = matmul_residual(o.reshape(n, GROUP_WIDTH), b_w_out[bl].astype(BF16), h)
        h = hierarchical_moe(h, ffn_norm[i], w_router_group[i], b_router_group[i], w_router_expert[i],
                             b_router_expert[i], w1[i].astype(BF16), w3[i].astype(BF16), w2[i].astype(BF16))
        h = per_layer_embedding(h, p[i].reshape(n, -1), ple_norm[i], w_ple_gate[i].astype(BF16),
                                w_ple_proj[i].astype(BF16))
        if i == n_a - 1:
            kv = norm_matmul(h, kv_norm, w_kv.astype(BF16), BF16)
    return h.reshape(b, s, d)
```

```python
import functools

import numpy as np
import jax
import jax.numpy as jnp
from jax import lax
from jax.experimental import pallas as pl
from jax.experimental.pallas import tpu as pltpu

F32 = jnp.float32
BF16 = jnp.bfloat16

NORM_EPS = 1e-6

DN_HEADS = 8
DN_DK = 128
DN_DV = 128
DN_CONV = 4
DN_CHUNK = 128
DN_SQUARINGS = DN_CHUNK.bit_length() - 2
DN_GROUP = 4
assert DN_CHUNK == DN_DK == DN_DV
DN_QK = DN_HEADS * DN_DK
DN_V = DN_HEADS * DN_DV
DN_MAIN = 2 * DN_QK + 2 * DN_V

DIL_CONFIGS = ((128, 1), (512, 4), (2048, 16))
N_ATT_GROUPS = len(DIL_CONFIGS)
HEAD_DIM = 128
Q_PER_GROUP = 4
KV_PER_GROUP = 2
Q_REP = Q_PER_GROUP // KV_PER_GROUP
ATT_BLOCK = 128
ALIBI_MAX = 8.0
GROUP_WIDTH = Q_PER_GROUP * HEAD_DIM

MOE_GROUPS = 4
MOE_EPG = 8
MOE_EXPERTS = MOE_GROUPS * MOE_EPG
MOE_HIDDEN = 512
MOE_BM = 256
ROUTER_LANES = 128
EXP_LANE0 = MOE_GROUPS

LANES = 128
VMEM_LIMIT = 48 * 1024 * 1024


def _alibi_slopes():
    n = N_ATT_GROUPS * Q_PER_GROUP
    s = 2.0 ** (-ALIBI_MAX * np.arange(1, n + 1) / n)
    return s.reshape(N_ATT_GROUPS, KV_PER_GROUP, Q_REP)


def _rms(x, g):
    ms = jnp.mean(x * x, axis=-1, keepdims=True)
    return x * lax.rsqrt(ms + NORM_EPS) * g


def _dot(a, b):
    return jnp.dot(a.astype(BF16), b.astype(BF16), preferred_element_type=F32)


def _dot_nt(a, b):
    return lax.dot_general(a.astype(BF16), b.astype(BF16), (((1,), (1,)), ((), ())),
                           preferred_element_type=F32)


def _dot_tn(a, b):
    return lax.dot_general(a.astype(BF16), b.astype(BF16), (((0,), (0,)), ((), ())),
                           preferred_element_type=F32)


def _split2(x):
    hi = x.astype(BF16)
    lo = (x - hi.astype(F32)).astype(BF16)
    return hi, lo


def _split3(x):
    hi = x.astype(BF16)
    r = x - hi.astype(F32)
    mid = r.astype(BF16)
    lo = (r - mid.astype(F32)).astype(BF16)
    return hi, mid, lo


def _dot_exact01(x, sel):
    hi, mid, lo = _split3(x)
    d = lambda p: jnp.dot(p, sel, preferred_element_type=F32)
    return d(hi) + d(mid) + d(lo)


def _sigmoid(x):
    return 1.0 / (1.0 + jnp.exp(-x))


def _silu(x):
    return x * _sigmoid(x)


def _nm_kernel(x_ref, g_ref, w_ref, o_ref, xn_ref):
    @pl.when(pl.program_id(1) == 0)
    def _():
        xn_ref[...] = _rms(x_ref[...], g_ref[...]).astype(BF16)

    o_ref[...] = jnp.dot(xn_ref[...], w_ref[...], preferred_element_type=F32).astype(o_ref.dtype)


def norm_matmul(x, g, w, out_dtype, tm=512, tn=512):
    n, k = x.shape
    m = w.shape[1]
    return pl.pallas_call(
        _nm_kernel,
        grid=(n // tm, m // tn),
        in_specs=[pl.BlockSpec((tm, k), lambda i, j: (i, 0)),
                  pl.BlockSpec((1, k), lambda i, j: (0, 0)),
                  pl.BlockSpec((k, tn), lambda i, j: (0, j))],
        out_specs=pl.BlockSpec((tm, tn), lambda i, j: (i, j)),
        out_shape=jax.ShapeDtypeStruct((n, m), out_dtype),
        scratch_shapes=[pltpu.VMEM((tm, k), BF16)],
        compiler_params=pltpu.CompilerParams(dimension_semantics=("parallel", "arbitrary"),
                                             vmem_limit_bytes=VMEM_LIMIT),
        name="norm_matmul",
    )(x, g.reshape(1, k), w)


def _dn_inproj_kernel(x_ref, g_ref, w_ref, wgt_ref, o_ref, gt_ref, xn_ref):
    @pl.when(pl.program_id(1) == 0)
    def _():
        xn = _rms(x_ref[...], g_ref[...])
        xh, xl = _split2(xn)
        xn_ref[...] = xh
        wh, wl = _split2(wgt_ref[...])
        gt_ref[...] = _dot_nt(wh, xh) + _dot_nt(wh, xl) + _dot_nt(wl, xh)

    o_ref[...] = jnp.dot(xn_ref[...], w_ref[...], preferred_element_type=F32).astype(o_ref.dtype)


def dn_inproj(x, g, w_main, w_gates_t, tm=512, tn=512):
    n, k = x.shape
    m = w_main.shape[1]
    ng = w_gates_t.shape[0]
    return pl.pallas_call(
        _dn_inproj_kernel,
        grid=(n // tm, m // tn),
        in_specs=[pl.BlockSpec((tm, k), lambda i, j: (i, 0)),
                  pl.BlockSpec((1, k), lambda i, j: (0, 0)),
                  pl.BlockSpec((k, tn), lambda i, j: (0, j)),
                  pl.BlockSpec((ng, k), lambda i, j: (0, 0))],
        out_specs=[pl.BlockSpec((tm, tn), lambda i, j: (i, j)),
                   pl.BlockSpec((ng, tm), lambda i, j: (0, i))],
        out_shape=[jax.ShapeDtypeStruct((n, m), BF16), jax.ShapeDtypeStruct((ng, n), F32)],
        scratch_shapes=[pltpu.VMEM((tm, k), BF16)],
        compiler_params=pltpu.CompilerParams(dimension_semantics=("parallel", "arbitrary"),
                                             vmem_limit_bytes=VMEM_LIMIT),
        name="dn_inproj",
    )(x, g.reshape(1, k), w_main, w_gates_t)


def _mm_res_kernel(a_ref, w_ref, r_ref, o_ref):
    o_ref[...] = r_ref[...] + jnp.dot(a_ref[...], w_ref[...], preferred_element_type=F32)


def matmul_residual(a, w, res, tm=512, tn=512):
    n, k = a.shape
    m = w.shape[1]
    return pl.pallas_call(
        _mm_res_kernel,
        grid=(n // tm, m // tn),
        in_specs=[pl.BlockSpec((tm, k), lambda i, j: (i, 0)),
                  pl.BlockSpec((k, tn), lambda i, j: (0, j)),
                  pl.BlockSpec((tm, tn), lambda i, j: (i, j))],
        out_specs=pl.BlockSpec((tm, tn), lambda i, j: (i, j)),
        out_shape=jax.ShapeDtypeStruct((n, m), F32),
        compiler_params=pltpu.CompilerParams(dimension_semantics=("parallel", "parallel"),
                                             vmem_limit_bytes=VMEM_LIMIT),
        name="matmul_residual",
    )(a, w, res)


DN_PIECE = 256
DN_HALO = 16
DN_HB = 4
assert DN_HEADS % DN_HB == 0


def _deltanet_kernel(alog_ref, dtb_ref, q_ref, k_ref, v_ref, z_ref, cq_ref, ck_ref, cv_ref,
                     bpre_ref, apre_ref, onorm_ref, o_ref,
                     qs_ref, ks_ref, vs_ref, gcum_ref, betac_ref, gc_ref,
                     pm_ref, rq_ref, qq_ref, o0_ref, elast_ref):
    seq = q_ref.shape[1]
    c = DN_CHUNK
    n_chunks = seq // c
    assert 2 * n_chunks <= c
    head0 = pl.program_id(1) * DN_HB
    ki = lax.broadcasted_iota(jnp.int32, (c, c), 0)
    ji = lax.broadcasted_iota(jnp.int32, (c, c), 1)
    upper = jnp.where(ki <= ji, 1.0, 0.0).astype(BF16)
    causal = ki >= ji
    strict = ki > ji
    onorm = onorm_ref[...]

    def conv_piece(x_ref, w_ref, hb, p):
        cols = slice(hb * DN_DK, (hb + 1) * DN_DK)
        w = w_ref[:, cols]
        if p == 0:
            x = x_ref[0, 0:DN_PIECE, cols].astype(F32)
            halo = 0
        else:
            x = x_ref[0, p * DN_PIECE - DN_HALO:(p + 1) * DN_PIECE, cols].astype(F32)
            halo = DN_HALO
        acc = x * w[DN_CONV - 1:DN_CONV, :]
        for j in range(1, DN_CONV):
            xs = pltpu.roll(x, j, axis=0)
            if p == 0:
                xs = jnp.where(lax.broadcasted_iota(jnp.int32, x.shape, 0) >= j, xs, 0.0)
            acc = acc + xs * w[DN_CONV - 1 - j:DN_CONV - j, :]
        return _silu(acc[halo:, :])

    def l2n(x):
        return x * lax.rsqrt(jnp.sum(x * x, axis=-1, keepdims=True) + NORM_EPS)

    def prologue(hb):
        for p in range(seq // DN_PIECE):
            rows = slice(p * DN_PIECE, (p + 1) * DN_PIECE)
            qs_ref[rows, :] = l2n(conv_piece(q_ref, cq_ref, hb, p)) * (DN_DK ** -0.5)
            ks_ref[rows, :] = l2n(conv_piece(k_ref, ck_ref, hb, p))
            vs_ref[rows, :] = conv_piece(v_ref, cv_ref, hb, p)
        beta = _sigmoid(bpre_ref[hb, 0])
        a = apre_ref[hb, 0] + dtb_ref[head0 + hb]
        softplus = jnp.maximum(a, 0.0) + jnp.log(1.0 + jnp.exp(-jnp.abs(a)))
        g_log = -jnp.exp(jnp.full(a.shape, alog_ref[head0 + hb], F32)) * softplus
        gcum = _dot_exact01(g_log, upper)
        gcum_ref[...] = gcum
        t = jnp.concatenate([beta, gcum, jnp.zeros((c - 2 * n_chunks, c), F32)], axis=0).T
        for ci in range(n_chunks):
            betac_ref[ci * c:(ci + 1) * c, :] = jnp.broadcast_to(t[:, ci:ci + 1], (c, DN_DV))
            gc_ref[ci * c:(ci + 1) * c, :] = jnp.broadcast_to(t[:, n_chunks + ci:n_chunks + ci + 1], (c, DN_DV))

    def prepare(hb, cis):
        each = lambda f, *ls: [f(*xs) for xs in zip(*ls)]
        rows = [pl.ds(pl.multiple_of(ci * c, c), c) for ci in cis]
        qc = [qs_ref[r, :] for r in rows]
        kc = [ks_ref[r, :] for r in rows]
        vc = [vs_ref[r, :] for r in rows]
        beta_c = [betac_ref[r, :] for r in rows]
        g_c = [gc_ref[r, :] for r in rows]
        g_j = [jnp.broadcast_to(gcum_ref[pl.ds(ci, 1), :], (c, c)) for ci in cis]
        decay = each(lambda gi, gj: jnp.exp(jnp.where(causal, gi - gj, -jnp.inf)), g_c, g_j)
        kq = each(lambda k, q: _dot_nt(jnp.concatenate([k, q], axis=0), k), kc, qc)
        m = each(lambda b, x, d: jnp.where(strict, -(b * x[:c, :] * d), 0.0), beta_c, kq, decay)
        pw = each(lambda x: _dot(x, x), m)
        r = m
        for _ in range(DN_SQUARINGS - 1):
            xs = each(lambda p_, r_: _dot(p_, jnp.concatenate([p_, r_], axis=1)), pw, r)
            r = each(lambda r_, p_, x: r_ + p_ + x[:, c:], r, pw, xs)
            pw = [x[:, :c] for x in xs]
        xs = each(_dot, pw, r)
        r = each(lambda r_, p_, x: r_ + p_ + x, r, pw, xs)
        e_g = [jnp.exp(g) for g in g_c]
        rhs = each(lambda b, v, e, k: jnp.concatenate([b * v, b * e * k], axis=1), beta_c, vc, e_g, kc)
        sol = each(lambda rh, r_: rh + _dot(r_, rh), rhs, r)
        attn = each(lambda x, d: jnp.where(causal, x[c:, :] * d, 0.0), kq, decay)
        k_d = each(lambda k, g: k * jnp.exp(jnp.broadcast_to(g[c - 1:c, :], (c, DN_DV)) - g), kc, g_c)
        kt = each(_dot_tn, k_d, sol)
        at = each(_dot, attn, sol)
        for i, (ci, r_) in enumerate(zip(cis, rows)):
            qq_ref[hb, r_, :] = kt[i][:, :DN_DV]
            pm_ref[hb, r_, :] = kt[i][:, DN_DV:].astype(BF16)
            o0_ref[hb, r_, :] = at[i][:, :DN_DV]
            rq_ref[hb, r_, :] = (qc[i] * e_g[i] - at[i][:, DN_DV:]).astype(BF16)
            elast_ref[hb, pl.ds(pl.multiple_of(ci * 8, 8), 8), :] = jnp.exp(
                jnp.broadcast_to(g_c[i][c - 1:c, :], (8, DN_DV)))

    for hb in range(DN_HB):
        prologue(hb)

        def prepare_group(gi, carry, hb=hb):
            prepare(hb, [gi * DN_GROUP + k for k in range(DN_GROUP)])
            return carry

        lax.fori_loop(0, n_chunks // DN_GROUP, prepare_group, 0)

    def chunk_step(ci, states):
        rows = pl.ds(pl.multiple_of(ci * c, c), c)
        xs = [_dot(jnp.concatenate([pm_ref[hb, rows, :], rq_ref[hb, rows, :]], axis=0), states[hb])
              for hb in range(DN_HB)]
        new_states = []
        for hb in range(DN_HB):
            cols = slice(hb * DN_DV, (hb + 1) * DN_DV)
            e_last = jnp.broadcast_to(elast_ref[hb, pl.ds(pl.multiple_of(ci * 8, 8), 1), :], (DN_DK, DN_DV))
            new_states.append(e_last * states[hb] - xs[hb][:c, :] + qq_ref[hb, rows, :])
            o = xs[hb][c:, :] + o0_ref[hb, rows, :]
            zc = z_ref[0, rows, cols].astype(F32)
            o_ref[0, rows, cols] = (_rms(o, onorm) * _silu(zc)).astype(o_ref.dtype)
        return tuple(new_states)

    lax.fori_loop(0, n_chunks, chunk_step, tuple(jnp.zeros((DN_DK, DN_DV), F32) for _ in range(DN_HB)))


def deltanet(proj, gates_t, conv_w, a_log, dt_bias, o_norm):
    b, s, _ = proj.shape
    ng = DN_HEADS // DN_HB
    nc = s // DN_CHUNK
    wide = DN_HB * DN_DK
    col = lambda off: pl.BlockSpec((1, s, wide), lambda bi, hi: (bi, 0, off + hi))
    cw = lambda off: pl.BlockSpec((DN_CONV, wide), lambda bi, hi: (0, off + hi))
    gate = lambda off: pl.BlockSpec((DN_HB, 1, nc, DN_CHUNK), lambda bi, hi: (off + hi, bi, 0, 0))
    smem = pl.BlockSpec(memory_space=pltpu.SMEM)
    per_head = lambda dt: pltpu.VMEM((DN_HB, s, DN_DV), dt)
    return pl.pallas_call(
        _deltanet_kernel,
        grid=(b, ng),
        in_specs=[smem, smem, col(0), col(ng), col(2 * ng), col(3 * ng), cw(0), cw(ng), cw(2 * ng),
                  gate(0), gate(ng), pl.BlockSpec((1, DN_DV), lambda bi, hi: (0, 0))],
        out_specs=pl.BlockSpec((1, s, wide), lambda bi, hi: (bi, 0, hi)),
        out_shape=jax.ShapeDtypeStruct((b, s, DN_V), BF16),
        scratch_shapes=[pltpu.VMEM((s, DN_DK), F32), pltpu.VMEM((s, DN_DK), F32), pltpu.VMEM((s, DN_DV), F32),
                        pltpu.VMEM((nc, DN_CHUNK), F32), pltpu.VMEM((s, DN_DV), F32), pltpu.VMEM((s, DN_DV), F32),
                        per_head(BF16), per_head(BF16), per_head(F32), per_head(F32),
                        pltpu.VMEM((DN_HB, nc * 8, DN_DV), F32)],
        compiler_params=pltpu.CompilerParams(dimension_semantics=("parallel", "parallel"),
                                             vmem_limit_bytes=VMEM_LIMIT),
        name="deltanet",
    )(a_log, dt_bias, proj, proj, proj, proj, conv_w, conv_w, conv_w, gates_t, gates_t, o_norm.reshape(1, DN_DV))


ATT_PIECE = 256
ATT_M_INIT = -1e30


def _attention_kernel(q_ref, kv_ref, qn_ref, kn_ref, o_ref, qf_ref, kf_ref, vf_ref, acc_ref, m_ref, l_ref):
    seq = q_ref.shape[1]
    grp = pl.program_id(1)
    hd = HEAD_DIM
    blk = ATT_BLOCK
    slopes = _alibi_slopes()

    qg = qn_ref[0] * (hd ** -0.5)
    kg = kn_ref[0]

    def prep(pi, carry):
        r0 = pl.multiple_of(pi * ATT_PIECE, ATT_PIECE)
        rows = pl.ds(r0, ATT_PIECE)
        for j in range(Q_PER_GROUP):
            cols = slice(j * hd, (j + 1) * hd)
            qf_ref[j, rows, :] = _rms(q_ref[0, rows, cols].astype(F32), 1.0) * qg
        for j in range(KV_PER_GROUP):
            cols = slice(j * hd, (j + 1) * hd)
            kf_ref[j, rows, :] = _rms(kv_ref[0, rows, cols].astype(F32), 1.0) * kg
            vcols = slice((KV_PER_GROUP + j) * hd, (KV_PER_GROUP + j + 1) * hd)
            vf_ref[j, rows, :] = kv_ref[0, rows, vcols].astype(F32)
        return carry

    lax.fori_loop(0, seq // ATT_PIECE, prep, 0)

    @pl.when(grp == 0)
    def _():
        def init(pi, carry):
            r0 = pl.multiple_of(pi * ATT_PIECE, ATT_PIECE)
            rows = pl.ds(r0, ATT_PIECE)
            for j in range(Q_PER_GROUP):
                acc_ref[j, rows, :] = jnp.zeros((ATT_PIECE, hd), F32)
                l_ref[j, rows, :] = jnp.zeros((ATT_PIECE, hd), F32)
                m_ref[j, rows, :] = jnp.full((ATT_PIECE, hd), ATT_M_INIT, F32)
            return carry

        lax.fori_loop(0, seq // ATT_PIECE, init, 0)

    def rows_of(start, size, dil):
        return pl.ds(start, size) if dil == 1 else pl.ds(start, size, stride=dil)

    def attend(g, dil, q_start, k_start, nk):
        qrows = rows_of(q_start, blk, dil)
        krows = rows_of(k_start, nk, dil)
        qi = lax.broadcasted_iota(jnp.int32, (blk, nk), 0)
        kidx = lax.broadcasted_iota(jnp.int32, (blk, nk), 1)
        dist = (nk - blk) + qi - kidx
        valid = (dist >= 0) & (dist <= blk)
        distf = dist.astype(F32)
        for kvh in range(KV_PER_GROUP):
            kb = kf_ref[kvh, krows, :].astype(BF16)
            vb = vf_ref[kvh, krows, :].astype(BF16)
            for rep in range(Q_REP):
                j = kvh * Q_REP + rep
                sc = _dot_nt(qf_ref[j, qrows, :], kb)
                sc = jnp.where(valid, sc - float(slopes[g, kvh, rep] * dil) * distf, -jnp.inf)
                m_old = m_ref[j, qrows, :]
                m_new = jnp.maximum(m_old, jnp.max(sc, axis=-1, keepdims=True))
                alpha = jnp.exp(m_old - m_new)
                p = jnp.exp(sc - m_new[:, 0:1])
                l_ref[j, qrows, :] = alpha * l_ref[j, qrows, :] + jnp.sum(p, axis=-1, keepdims=True)
                acc_ref[j, qrows, :] = alpha * acc_ref[j, qrows, :] + jnp.dot(p.astype(BF16), vb,
                                                                              preferred_element_type=F32)
                m_ref[j, qrows, :] = m_new

    for g, (window, dil) in enumerate(DIL_CONFIGS):
        sub_len = seq // dil
        nblk = sub_len // blk

        @pl.when(grp == g)
        def _(g=g, dil=dil, nblk=nblk):
            def residue(res, carry):
                attend(g, dil, res, res, blk)
                if nblk > 1:
                    def later(n, c2):
                        attend(g, dil, res + n * blk * dil, res + (n - 1) * blk * dil, 2 * blk)
                        return c2
                    lax.fori_loop(1, nblk, later, 0)
                return carry

            lax.fori_loop(0, dil, residue, 0)

    @pl.when(grp == N_ATT_GROUPS - 1)
    def _():
        def finish(pi, carry):
            r0 = pl.multiple_of(pi * ATT_PIECE, ATT_PIECE)
            rows = pl.ds(r0, ATT_PIECE)
            for j in range(Q_PER_GROUP):
                cols = slice(j * hd, (j + 1) * hd)
                o_ref[0, rows, cols] = (acc_ref[j, rows, :] / l_ref[j, rows, :]).astype(o_ref.dtype)
            return carry

        lax.fori_loop(0, seq // ATT_PIECE, finish, 0)


def dilated_attention(q, kv, q_norm, k_norm):
    b, s, _ = q.shape
    gw = GROUP_WIDTH
    return pl.pallas_call(
        _attention_kernel,
        grid=(b, N_ATT_GROUPS),
        in_specs=[pl.BlockSpec((1, s, gw), lambda bi, gi: (bi, 0, gi)),
                  pl.BlockSpec((1, s, gw), lambda bi, gi: (bi, 0, gi)),
                  pl.BlockSpec((1, 1, HEAD_DIM), lambda bi, gi: (gi, 0, 0)),
                  pl.BlockSpec((1, 1, HEAD_DIM), lambda bi, gi: (gi, 0, 0))],
        out_specs=pl.BlockSpec((1, s, gw), lambda bi, gi: (bi, 0, 0)),
        out_shape=jax.ShapeDtypeStruct((b, s, gw), BF16),
        scratch_shapes=[pltpu.VMEM((Q_PER_GROUP, s, HEAD_DIM), F32),
                        pltpu.VMEM((KV_PER_GROUP, s, HEAD_DIM), F32),
                        pltpu.VMEM((KV_PER_GROUP, s, HEAD_DIM), F32),
                        pltpu.VMEM((Q_PER_GROUP, s, HEAD_DIM), F32),
                        pltpu.VMEM((Q_PER_GROUP, s, HEAD_DIM), F32),
                        pltpu.VMEM((Q_PER_GROUP, s, HEAD_DIM), F32)],
        compiler_params=pltpu.CompilerParams(dimension_semantics=("parallel", "arbitrary"),
                                             vmem_limit_bytes=VMEM_LIMIT),
        name="dilated_attention",
    )(q, kv, q_norm.reshape(N_ATT_GROUPS, 1, HEAD_DIM), k_norm.reshape(N_ATT_GROUPS, 1, HEAD_DIM))


ROUTER_TM = 512
META_E0, META_E1, META_W0, META_W1, META_R0, META_R1 = 0, 1, 2, 3, 4, 5


def _router_kernel(h_ref, g_ref, w_ref, b_ref, xn_ref, meta_ref, cnt_ref, carry_ref):
    tm = h_ref.shape[0]

    @pl.when(pl.program_id(0) == 0)
    def _():
        carry_ref[...] = jnp.zeros_like(carry_ref)

    xn = _rms(h_ref[...], g_ref[...])
    xn_ref[...] = xn
    xh, xl = _split2(xn)
    wh, wl = _split2(w_ref[...])
    d = lambda a, bb: jnp.dot(a, bb, preferred_element_type=F32)
    logits = d(xh, wh) + d(xh, wl) + d(xl, wh) + b_ref[...]

    lane = lax.broadcasted_iota(jnp.int32, (tm, ROUTER_LANES), 1)
    big = jnp.int32(ROUTER_LANES)
    first_where = lambda cond: jnp.min(jnp.where(cond, lane, big), axis=-1, keepdims=True)

    gl = jnp.where(lane < MOE_GROUPS, logits, -jnp.inf)
    ge = jnp.exp(gl - jnp.max(gl, axis=-1, keepdims=True))
    gp = ge / jnp.sum(ge, axis=-1, keepdims=True)
    g_w = jnp.max(gp, axis=-1, keepdims=True)
    g_idx = first_where(gp == g_w)

    lo = EXP_LANE0 + g_idx * MOE_EPG
    in_group = (lane >= lo) & (lane < lo + MOE_EPG)
    el = jnp.where(in_group, logits, -jnp.inf)
    ee = jnp.exp(el - jnp.max(el, axis=-1, keepdims=True))
    ep = ee / jnp.sum(ee, axis=-1, keepdims=True)
    p0 = jnp.max(jnp.where(in_group, ep, -1.0), axis=-1, keepdims=True)
    i0 = first_where(in_group & (ep == p0))
    rest = in_group & (lane != i0)
    p1 = jnp.max(jnp.where(rest, ep, -1.0), axis=-1, keepdims=True)
    i1 = first_where(rest & (ep == p1))
    w0 = g_w * p0 / (p0 + p1)
    w1 = g_w * p1 / (p0 + p1)

    oh0 = jnp.where(lane == i0, 1.0, 0.0)
    oh1 = jnp.where(lane == i1, 1.0, 0.0)
    both = oh0 + oh1
    ti = lax.broadcasted_iota(jnp.int32, (tm, tm), 0)
    tj = lax.broadcasted_iota(jnp.int32, (tm, tm), 1)
    before = jnp.where(tj < ti, 1.0, 0.0).astype(BF16)
    seen = carry_ref[...] + jnp.dot(before, both.astype(BF16), preferred_element_type=F32)
    r0 = jnp.sum(seen * oh0, axis=-1, keepdims=True)
    r1 = jnp.sum(seen * oh1, axis=-1, keepdims=True)
    total = carry_ref[...] + jnp.sum(both, axis=0, keepdims=True)
    carry_ref[...] = total
    cnt_ref[...] = jnp.broadcast_to(total, cnt_ref.shape)

    e0 = (i0 - EXP_LANE0).astype(F32)
    e1 = (i1 - EXP_LANE0).astype(F32)
    meta = jnp.zeros((tm, ROUTER_LANES), F32)
    for idx, val in ((META_E0, e0), (META_E1, e1), (META_W0, w0), (META_W1, w1), (META_R0, r0), (META_R1, r1)):
        meta = jnp.where(lane == idx, val, meta)
    meta_ref[...] = meta


def moe_router(h, g, w_router, b_router):
    n, k = h.shape
    tm = ROUTER_TM
    return pl.pallas_call(
        _router_kernel,
        grid=(n // tm,),
        in_specs=[pl.BlockSpec((tm, k), lambda i: (i, 0)),
                  pl.BlockSpec((1, k), lambda i: (0, 0)),
                  pl.BlockSpec((k, ROUTER_LANES), lambda i: (0, 0)),
                  pl.BlockSpec((1, ROUTER_LANES), lambda i: (0, 0))],
        out_specs=[pl.BlockSpec((tm, k), lambda i: (i, 0)),
                   pl.BlockSpec((tm, ROUTER_LANES), lambda i: (i, 0)),
                   pl.BlockSpec((8, ROUTER_LANES), lambda i: (0, 0))],
        out_shape=[jax.ShapeDtypeStruct((n, k), F32),
                   jax.ShapeDtypeStruct((n, ROUTER_LANES), F32),
                   jax.ShapeDtypeStruct((8, ROUTER_LANES), F32)],
        scratch_shapes=[pltpu.VMEM((1, ROUTER_LANES), F32)],
        compiler_params=pltpu.CompilerParams(dimension_semantics=("arbitrary",),
                                             vmem_limit_bytes=VMEM_LIMIT),
        name="moe_router",
    )(h, g.reshape(1, k), w_router, b_router)


def _row_copy(src_hbm, row, dst_vmem, slot, sem):
    return pltpu.make_async_copy(src_hbm.at[pl.ds(row, 1), :], dst_vmem.at[pl.ds(slot, 1), :], sem)


def _gather_rows(idx_ref, src_hbm, dst_vmem, sem, count):
    def start(r, carry):
        _row_copy(src_hbm, idx_ref[r], dst_vmem, r, sem).start()
        return carry

    def wait(r, carry):
        _row_copy(src_hbm, 0, dst_vmem, r, sem).wait()
        return carry

    lax.fori_loop(0, count, start, 0)
    lax.fori_loop(0, count, wait, 0)


def _expert_kernel(be_ref, nvalid_ref, tok_ref, x_hbm, w1_ref, w3_ref, w2_ref, y_ref, xbuf_ref, sem):
    i = pl.program_id(0)
    bm = y_ref.shape[0]

    @pl.when(i < nvalid_ref[0])
    def _():
        _gather_rows(tok_ref.at[0, 0], x_hbm, xbuf_ref, sem, bm)
        x = xbuf_ref[...].astype(BF16)
        h1 = jnp.dot(x, w1_ref[0], preferred_element_type=F32)
        h3 = jnp.dot(x, w3_ref[0], preferred_element_type=F32)
        hdn = (_silu(h1) * h3).astype(BF16)
        y_ref[...] = jnp.dot(hdn, w2_ref[0], preferred_element_type=F32)

    @pl.when(i >= nvalid_ref[0])
    def _():
        y_ref[...] = jnp.zeros_like(y_ref)


def moe_experts(xn, row_tok, block_e, n_valid, w1, w3, w2):
    n, d = xn.shape
    n_blocks = block_e.shape[0]
    bm = MOE_BM
    hid = w1.shape[2]
    grid_spec = pltpu.PrefetchScalarGridSpec(
        num_scalar_prefetch=2,
        grid=(n_blocks,),
        in_specs=[pl.BlockSpec((1, 1, bm), lambda i, be, nv: (i, 0, 0), memory_space=pltpu.SMEM),
                  pl.BlockSpec(memory_space=pl.ANY),
                  pl.BlockSpec((1, d, hid), lambda i, be, nv: (be[i], 0, 0)),
                  pl.BlockSpec((1, d, hid), lambda i, be, nv: (be[i], 0, 0)),
                  pl.BlockSpec((1, hid, d), lambda i, be, nv: (be[i], 0, 0))],
        out_specs=pl.BlockSpec((bm, d), lambda i, be, nv: (i, 0)),
        scratch_shapes=[pltpu.VMEM((bm, d), F32), pltpu.SemaphoreType.DMA(())],
    )
    return pl.pallas_call(
        _expert_kernel,
        grid_spec=grid_spec,
        out_shape=jax.ShapeDtypeStruct((n_blocks * bm, d), F32),
        compiler_params=pltpu.CompilerParams(dimension_semantics=("arbitrary",),
                                             vmem_limit_bytes=VMEM_LIMIT),
        name="moe_experts",
    )(block_e, n_valid, row_tok.reshape(n_blocks, 1, bm), xn, w1, w3, w2)


COMBINE_TM = 256


def _combine_kernel(dest_ref, y_hbm, h_ref, meta_ref, o_ref, ybuf_ref, sem):
    tm = h_ref.shape[0]
    _gather_rows(dest_ref.at[0, 0], y_hbm, ybuf_ref, sem, 2 * tm)
    meta = meta_ref[...]
    w0 = meta[:, META_W0:META_W0 + 1]
    w1 = meta[:, META_W1:META_W1 + 1]
    o_ref[...] = h_ref[...] + (w0 * ybuf_ref[0:tm, :] + w1 * ybuf_ref[tm:2 * tm, :])


def moe_combine(y, dest, h, meta):
    n, d = h.shape
    tm = COMBINE_TM
    return pl.pallas_call(
        _combine_kernel,
        grid=(n // tm,),
        in_specs=[pl.BlockSpec((1, 1, 2 * tm), lambda i: (i, 0, 0), memory_space=pltpu.SMEM),
                  pl.BlockSpec(memory_space=pl.ANY),
                  pl.BlockSpec((tm, d), lambda i: (i, 0)),
                  pl.BlockSpec((tm, ROUTER_LANES), lambda i: (i, 0))],
        out_specs=pl.BlockSpec((tm, d), lambda i: (i, 0)),
        out_shape=jax.ShapeDtypeStruct((n, d), F32),
        scratch_shapes=[pltpu.VMEM((2 * tm, d), F32), pltpu.SemaphoreType.DMA(())],
        compiler_params=pltpu.CompilerParams(dimension_semantics=("arbitrary",),
                                             vmem_limit_bytes=VMEM_LIMIT),
        name="moe_combine",
    )(dest, y, h, meta)


def hierarchical_moe(h, ffn_norm, w_rg, b_rg, w_re, b_re, w1, w3, w2):
    n, d = h.shape
    pad = ROUTER_LANES - MOE_GROUPS - MOE_EXPERTS
    w_router = jnp.concatenate([w_rg, w_re, jnp.zeros((d, pad), F32)], axis=1)
    b_router = jnp.concatenate([b_rg, b_re, jnp.zeros((pad,), F32)]).reshape(1, ROUTER_LANES)
    xn, meta, cnt = moe_router(h, ffn_norm, w_router, b_router)

    bm = MOE_BM
    counts = cnt[0, EXP_LANE0:EXP_LANE0 + MOE_EXPERTS].astype(jnp.int32)
    padded = (counts + bm - 1) // bm * bm
    pad_ends = jnp.cumsum(padded)
    pad_starts = pad_ends - padded
    n_blocks = (2 * n) // bm + MOE_EXPERTS
    block_e = jnp.minimum(jnp.searchsorted(pad_ends, jnp.arange(n_blocks, dtype=jnp.int32) * bm, side='right'),
                          MOE_EXPERTS - 1).astype(jnp.int32)
    n_valid = (pad_ends[-1:] // bm).astype(jnp.int32)
    eid = meta[:, META_E0:META_E1 + 1].astype(jnp.int32)
    rank = meta[:, META_R0:META_R1 + 1].astype(jnp.int32)
    dest = pad_starts[eid] + rank
    tok = jnp.broadcast_to(jnp.arange(n, dtype=jnp.int32)[:, None], (n, 2))
    row_tok = jnp.zeros((n_blocks * bm,), jnp.int32).at[dest.reshape(-1)].set(tok.reshape(-1))

    y = moe_experts(xn, row_tok, block_e, n_valid, w1, w3, w2)
    tm = COMBINE_TM
    dest_tiles = dest.reshape(n // tm, tm, 2).transpose(0, 2, 1).reshape(n // tm, 1, 2 * tm)
    return moe_combine(y, dest_tiles, h, meta)


def _ple_kernel(h_ref, hres_ref, g_ref, wg_ref, p_ref, wp_ref, o_ref, xn_ref, pb_ref):
    @pl.when(pl.program_id(1) == 0)
    def _():
        xn_ref[...] = _rms(h_ref[...], g_ref[...]).astype(BF16)
        pb_ref[...] = p_ref[...].astype(BF16)

    gate = _sigmoid(jnp.dot(xn_ref[...], wg_ref[...], preferred_element_type=F32))
    proj = jnp.dot(pb_ref[...], wp_ref[...], preferred_element_type=F32)
    o_ref[...] = hres_ref[...] + gate * proj


def per_layer_embedding(h, p_i, g, w_gate, w_proj, tm=512, tn=512):
    n, d = h.shape
    pd = p_i.shape[1]
    return pl.pallas_call(
        _ple_kernel,
        grid=(n // tm, d // tn),
        in_specs=[pl.BlockSpec((tm, d), lambda i, j: (i, 0)),
                  pl.BlockSpec((tm, tn), lambda i, j: (i, j)),
                  pl.BlockSpec((1, d), lambda i, j: (0, 0)),
                  pl.BlockSpec((d, tn), lambda i, j: (0, j)),
                  pl.BlockSpec((tm, pd), lambda i, j: (i, 0)),
                  pl.BlockSpec((pd, tn), lambda i, j: (0, j))],
        out_specs=pl.BlockSpec((tm, tn), lambda i, j: (i, j)),
        out_shape=jax.ShapeDtypeStruct((n, d), F32),
        scratch_shapes=[pltpu.VMEM((tm, d), BF16), pltpu.VMEM((tm, pd), BF16)],
        compiler_params=pltpu.CompilerParams(dimension_semantics=("parallel", "arbitrary"),
                                             vmem_limit_bytes=VMEM_LIMIT),
        name="per_layer_embedding",
    )(h, h, g.reshape(1, d), w_gate, p_i, w_proj)


def kernel(x, p, a_norm, a_w_in, a_conv, a_A_log, a_dt_bias, a_o_norm, a_w_out, kv_norm, w_kv, k_norm, b_norm, b_w_q, b_q_norm, b_w_out, ffn_norm, w_router_group, b_router_group, w_router_expert, b_router_expert, w1, w3, w2, ple_norm, w_ple_gate, w_ple_proj):
    b, s, d = x.shape
    n = b * s
    depth = p.shape[0]
    n_a = a_norm.shape[0]
    h = x.reshape(n, d)
    kv = None
    for i in range(depth):
        if i < n_a:
            w_in = a_w_in[i]
            proj, gates_t = dn_inproj(h, a_norm[i], w_in[:, :DN_MAIN].astype(BF16), w_in[:, DN_MAIN:].T)
            o = deltanet(proj.reshape(b, s, DN_MAIN), gates_t.reshape(2 * DN_HEADS, b, s // DN_CHUNK, DN_CHUNK),
                         a_conv[i], a_A_log[i], a_dt_bias[i], a_o_norm[i])
            h = matmul_residual(o.reshape(n, DN_V), a_w_out[i].astype(BF16), h)
        else:
            bl = i - n_a
            q = norm_matmul(h, b_norm[bl], b_w_q[bl].astype(BF16), BF16)
            o = dilated_attention(q.reshape(b, s, -1), kv.reshape(b, s, -1), b_q_norm[bl], k_norm)
            h = matmul_residual(o.reshape(n, GROUP_WIDTH), b_w_out[bl].astype(BF16), h)
        h = hierarchical_moe(h, ffn_norm[i], w_router_group[i], b_router_group[i], w_router_expert[i],
                             b_router_expert[i], w1[i].astype(BF16), w3[i].astype(BF16), w2[i].astype(BF16))
        h = per_layer_embedding(h, p[i].reshape(n, -1), ple_norm[i], w_ple_gate[i].astype(BF16),
                                w_ple_proj[i].astype(BF16))
        if i == n_a - 1:
            kv = norm_matmul(h, kv_norm, w_kv.astype(BF16), BF16)
    return h.reshape(b, s, d)
```

```python
import functools

import numpy as np
import jax
import jax.numpy as jnp
from jax import lax
from jax.experimental import pallas as pl
from jax.experimental.pallas import tpu as pltpu

F32 = jnp.float32
BF16 = jnp.bfloat16

NORM_EPS = 1e-6

DN_HEADS = 8
DN_DK = 128
DN_DV = 128
DN_CONV = 4
DN_CHUNK = 128
DN_SQUARINGS = DN_CHUNK.bit_length() - 2
DN_GROUP = 4
assert DN_CHUNK == DN_DK == DN_DV
DN_QK = DN_HEADS * DN_DK
DN_V = DN_HEADS * DN_DV
DN_MAIN = 2 * DN_QK + 2 * DN_V

DIL_CONFIGS = ((128, 1), (512, 4), (2048, 16))
N_ATT_GROUPS = len(DIL_CONFIGS)
HEAD_DIM = 128
Q_PER_GROUP = 4
KV_PER_GROUP = 2
Q_REP = Q_PER_GROUP // KV_PER_GROUP
ATT_BLOCK = 128
ALIBI_MAX = 8.0
GROUP_WIDTH = Q_PER_GROUP * HEAD_DIM

MOE_GROUPS = 4
MOE_EPG = 8
MOE_EXPERTS = MOE_GROUPS * MOE_EPG
MOE_HIDDEN = 512
MOE_BM = 256
ROUTER_LANES = 128
EXP_LANE0 = MOE_GROUPS

LANES = 128
VMEM_LIMIT = 48 * 1024 * 1024


def _alibi_slopes():
    n = N_ATT_GROUPS * Q_PER_GROUP
    s = 2.0 ** (-ALIBI_MAX * np.arange(1, n + 1) / n)
    return s.reshape(N_ATT_GROUPS, KV_PER_GROUP, Q_REP)


def _rms(x, g):
    ms = jnp.mean(x * x, axis=-1, keepdims=True)
    return x * lax.rsqrt(ms + NORM_EPS) * g


def _dot(a, b):
    return jnp.dot(a.astype(BF16), b.astype(BF16), preferred_element_type=F32)


def _dot_nt(a, b):
    return lax.dot_general(a.astype(BF16), b.astype(BF16), (((1,), (1,)), ((), ())),
                           preferred_element_type=F32)


def _dot_tn(a, b):
    return lax.dot_general(a.astype(BF16), b.astype(BF16), (((0,), (0,)), ((), ())),
                           preferred_element_type=F32)


def _split2(x):
    hi = x.astype(BF16)
    lo = (x - hi.astype(F32)).astype(BF16)
    return hi, lo


def _split3(x):
    hi = x.astype(BF16)
    r = x - hi.astype(F32)
    mid = r.astype(BF16)
    lo = (r - mid.astype(F32)).astype(BF16)
    return hi, mid, lo


def _dot_exact01(x, sel):
    hi, mid, lo = _split3(x)
    d = lambda p: jnp.dot(p, sel, preferred_element_type=F32)
    return d(hi) + d(mid) + d(lo)


def _sigmoid(x):
    return 1.0 / (1.0 + jnp.exp(-x))


def _silu(x):
    return x * _sigmoid(x)


def _nm_kernel(x_ref, g_ref, w_ref, o_ref, xn_ref):
    @pl.when(pl.program_id(1) == 0)
    def _():
        xn_ref[...] = _rms(x_ref[...], g_ref[...]).astype(BF16)

    o_ref[...] = jnp.dot(xn_ref[...], w_ref[...], preferred_element_type=F32).astype(o_ref.dtype)


def norm_matmul(x, g, w, out_dtype, tm=512, tn=512):
    n, k = x.shape
    m = w.shape[1]
    return pl.pallas_call(
        _nm_kernel,
        grid=(n // tm, m // tn),
        in_specs=[pl.BlockSpec((tm, k), lambda i, j: (i, 0)),
                  pl.BlockSpec((1, k), lambda i, j: (0, 0)),
                  pl.BlockSpec((k, tn), lambda i, j: (0, j))],
        out_specs=pl.BlockSpec((tm, tn), lambda i, j: (i, j)),
        out_shape=jax.ShapeDtypeStruct((n, m), out_dtype),
        scratch_shapes=[pltpu.VMEM((tm, k), BF16)],
        compiler_params=pltpu.CompilerParams(dimension_semantics=("parallel", "arbitrary"),
                                             vmem_limit_bytes=VMEM_LIMIT),
        name="norm_matmul",
    )(x, g.reshape(1, k), w)


def _dn_inproj_kernel(x_ref, g_ref, w_ref, wgt_ref, o_ref, gt_ref, xn_ref):
    @pl.when(pl.program_id(1) == 0)
    def _():
        xn = _rms(x_ref[...], g_ref[...])
        xh, xl = _split2(xn)
        xn_ref[...] = xh
        wh, wl = _split2(wgt_ref[...])
        gt_ref[...] = _dot_nt(wh, xh) + _dot_nt(wh, xl) + _dot_nt(wl, xh)

    o_ref[...] = jnp.dot(xn_ref[...], w_ref[...], preferred_element_type=F32).astype(o_ref.dtype)


def dn_inproj(x, g, w_main, w_gates_t, tm=512, tn=512):
    n, k = x.shape
    m = w_main.shape[1]
    ng = w_gates_t.shape[0]
    return pl.pallas_call(
        _dn_inproj_kernel,
        grid=(n // tm, m // tn),
        in_specs=[pl.BlockSpec((tm, k), lambda i, j: (i, 0)),
                  pl.BlockSpec((1, k), lambda i, j: (0, 0)),
                  pl.BlockSpec((k, tn), lambda i, j: (0, j)),
                  pl.BlockSpec((ng, k), lambda i, j: (0, 0))],
        out_specs=[pl.BlockSpec((tm, tn), lambda i, j: (i, j)),
                   pl.BlockSpec((ng, tm), lambda i, j: (0, i))],
        out_shape=[jax.ShapeDtypeStruct((n, m), BF16), jax.ShapeDtypeStruct((ng, n), F32)],
        scratch_shapes=[pltpu.VMEM((tm, k), BF16)],
        compiler_params=pltpu.CompilerParams(dimension_semantics=("parallel", "arbitrary"),
                                             vmem_limit_bytes=VMEM_LIMIT),
        name="dn_inproj",
    )(x, g.reshape(1, k), w_main, w_gates_t)


def _mm_res_kernel(a_ref, w_ref, r_ref, o_ref):
    o_ref[...] = r_ref[...] + jnp.dot(a_ref[...], w_ref[...], preferred_element_type=F32)


def matmul_residual(a, w, res, tm=512, tn=512):
    n, k = a.shape
    m = w.shape[1]
    return pl.pallas_call(
        _mm_res_kernel,
        grid=(n // tm, m // tn),
        in_specs=[pl.BlockSpec((tm, k), lambda i, j: (i, 0)),
                  pl.BlockSpec((k, tn), lambda i, j: (0, j)),
                  pl.BlockSpec((tm, tn), lambda i, j: (i, j))],
        out_specs=pl.BlockSpec((tm, tn), lambda i, j: (i, j)),
        out_shape=jax.ShapeDtypeStruct((n, m), F32),
        compiler_params=pltpu.CompilerParams(dimension_semantics=("parallel", "parallel"),
                                             vmem_limit_bytes=VMEM_LIMIT),
        name="matmul_residual",
    )(a, w, res)


DN_PIECE = 256
DN_HALO = 16
DN_HB = 4
assert DN_HEADS % DN_HB == 0


def _deltanet_kernel(alog_ref, dtb_ref, q_ref, k_ref, v_ref, z_ref, cq_ref, ck_ref, cv_ref,
                     bpre_ref, apre_ref, onorm_ref, o_ref,
                     qs_ref, ks_ref, vs_ref, gcum_ref, betac_ref, gc_ref,
                     pm_ref, rq_ref, qq_ref, o0_ref, elast_ref):
    seq = q_ref.shape[1]
    c = DN_CHUNK
    n_chunks = seq // c
    assert 2 * n_chunks <= c
    head0 = pl.program_id(1) * DN_HB
    ki = lax.broadcasted_iota(jnp.int32, (c, c), 0)
    ji = lax.broadcasted_iota(jnp.int32, (c, c), 1)
    upper = jnp.where(ki <= ji, 1.0, 0.0).astype(BF16)
    causal = ki >= ji
    strict = ki > ji
    onorm = onorm_ref[...]

    def conv_piece(x_ref, w_ref, hb, p):
        cols = slice(hb * DN_DK, (hb + 1) * DN_DK)
        w = w_ref[:, cols]
        if p == 0:
            x = x_ref[0, 0:DN_PIECE, cols].astype(F32)
            halo = 0
        else:
            x = x_ref[0, p * DN_PIECE - DN_HALO:(p + 1) * DN_PIECE, cols].astype(F32)
            halo = DN_HALO
        acc = x * w[DN_CONV - 1:DN_CONV, :]
        for j in range(1, DN_CONV):
            xs = pltpu.roll(x, j, axis=0)
            if p == 0:
                xs = jnp.where(lax.broadcasted_iota(jnp.int32, x.shape, 0) >= j, xs, 0.0)
            acc = acc + xs * w[DN_CONV - 1 - j:DN_CONV - j, :]
        return _silu(acc[halo:, :])

    def l2n(x):
        return x * lax.rsqrt(jnp.sum(x * x, axis=-1, keepdims=True) + NORM_EPS)

    def prologue(hb):
        for p in range(seq // DN_PIECE):
            rows = slice(p * DN_PIECE, (p + 1) * DN_PIECE)
            qs_ref[rows, :] = l2n(conv_piece(q_ref, cq_ref, hb, p)) * (DN_DK ** -0.5)
            ks_ref[rows, :] = l2n(conv_piece(k_ref, ck_ref, hb, p))
            vs_ref[rows, :] = conv_piece(v_ref, cv_ref, hb, p)
        beta = _sigmoid(bpre_ref[hb, 0])
        a = apre_ref[hb, 0] + dtb_ref[head0 + hb]
        softplus = jnp.maximum(a, 0.0) + jnp.log(1.0 + jnp.exp(-jnp.abs(a)))
        g_log = -jnp.exp(jnp.full(a.shape, alog_ref[head0 + hb], F32)) * softplus
        gcum = _dot_exact01(g_log, upper)
        gcum_ref[...] = gcum
        t = jnp.concatenate([beta, gcum, jnp.zeros((c - 2 * n_chunks, c), F32)], axis=0).T
        for ci in range(n_chunks):
            betac_ref[ci * c:(ci + 1) * c, :] = jnp.broadcast_to(t[:, ci:ci + 1], (c, DN_DV))
            gc_ref[ci * c:(ci + 1) * c, :] = jnp.broadcast_to(t[:, n_chunks + ci:n_chunks + ci + 1], (c, DN_DV))

    def prepare(hb, cis):
        each = lambda f, *ls: [f(*xs) for xs in zip(*ls)]
        rows = [pl.ds(pl.multiple_of(ci * c, c), c) for ci in cis]
        qc = [qs_ref[r, :] for r in rows]
        kc = [ks_ref[r, :] for r in rows]
        vc = [vs_ref[r, :] for r in rows]
        beta_c = [betac_ref[r, :] for r in rows]
        g_c = [gc_ref[r, :] for r in rows]
        g_j = [jnp.broadcast_to(gcum_ref[pl.ds(ci, 1), :], (c, c)) for ci in cis]
        decay = each(lambda gi, gj: jnp.exp(jnp.where(causal, gi - gj, -jnp.inf)), g_c, g_j)
        kq = each(lambda k, q: _dot_nt(jnp.concatenate([k, q], axis=0), k), kc, qc)
        m = each(lambda b, x, d: jnp.where(strict, -(b * x[:c, :] * d), 0.0), beta_c, kq, decay)
        pw = each(lambda x: _dot(x, x), m)
        r = m
        for _ in range(DN_SQUARINGS - 1):
            xs = each(lambda p_, r_: _dot(p_, jnp.concatenate([p_, r_], axis=1)), pw, r)
            r = each(lambda r_, p_, x: r_ + p_ + x[:, c:], r, pw, xs)
            pw = [x[:, :c] for x in xs]
        xs = each(_dot, pw, r)
        r = each(lambda r_, p_, x: r_ + p_ + x, r, pw, xs)
        e_g = [jnp.exp(g) for g in g_c]
        rhs = each(lambda b, v, e, k: jnp.concatenate([b * v, b * e * k], axis=1), beta_c, vc, e_g, kc)
        sol = each(lambda rh, r_: rh + _dot(r_, rh), rhs, r)
        attn = each(lambda x, d: jnp.where(causal, x[c:, :] * d, 0.0), kq, decay)
        k_d = each(lambda k, g: k * jnp.exp(jnp.broadcast_to(g[c - 1:c, :], (c, DN_DV)) - g), kc, g_c)
        kt = each(_dot_tn, k_d, sol)
        at = each(_dot, attn, sol)
        for i, (ci, r_) in enumerate(zip(cis, rows)):
            qq_ref[hb, r_, :] = kt[i][:, :DN_DV]
            pm_ref[hb, r_, :] = kt[i][:, DN_DV:].astype(BF16)
            o0_ref[hb, r_, :] = at[i][:, :DN_DV]
            rq_ref[hb, r_, :] = (qc[i] * e_g[i] - at[i][:, DN_DV:]).astype(BF16)
            elast_ref[hb, pl.ds(pl.multiple_of(ci * 8, 8), 8), :] = jnp.exp(
                jnp.broadcast_to(g_c[i][c - 1:c, :], (8, DN_DV)))

    for hb in range(DN_HB):
        prologue(hb)

        def prepare_group(gi, carry, hb=hb):
            prepare(hb, [gi * DN_GROUP + k for k in range(DN_GROUP)])
            return carry

        lax.fori_loop(0, n_chunks // DN_GROUP, prepare_group, 0)

    def chunk_step(ci, states):
        rows = pl.ds(pl.multiple_of(ci * c, c), c)
        xs = [_dot(jnp.concatenate([pm_ref[hb, rows, :], rq_ref[hb, rows, :]], axis=0), states[hb])
              for hb in range(DN_HB)]
        new_states = []
        for hb in range(DN_HB):
            cols = slice(hb * DN_DV, (hb + 1) * DN_DV)
            e_last = jnp.broadcast_to(elast_ref[hb, pl.ds(pl.multiple_of(ci * 8, 8), 1), :], (DN_DK, DN_DV))
            new_states.append(e_last * states[hb] - xs[hb][:c, :] + qq_ref[hb, rows, :])
            o = xs[hb][c:, :] + o0_ref[hb, rows, :]
            zc = z_ref[0, rows, cols].astype(F32)
            o_ref[0, rows, cols] = (_rms(o, onorm) * _silu(zc)).astype(o_ref.dtype)
        return tuple(new_states)

    lax.fori_loop(0, n_chunks, chunk_step, tuple(jnp.zeros((DN_DK, DN_DV), F32) for _ in range(DN_HB)))


def deltanet(proj, gates_t, conv_w, a_log, dt_bias, o_norm):
    b, s, _ = proj.shape
    ng = DN_HEADS // DN_HB
    nc = s // DN_CHUNK
    wide = DN_HB * DN_DK
    col = lambda off: pl.BlockSpec((1, s, wide), lambda bi, hi: (bi, 0, off + hi))
    cw = lambda off: pl.BlockSpec((DN_CONV, wide), lambda bi, hi: (0, off + hi))
    gate = lambda off: pl.BlockSpec((DN_HB, 1, nc, DN_CHUNK), lambda bi, hi: (off + hi, bi, 0, 0))
    smem = pl.BlockSpec(memory_space=pltpu.SMEM)
    per_head = lambda dt: pltpu.VMEM((DN_HB, s, DN_DV), dt)
    return pl.pallas_call(
        _deltanet_kernel,
        grid=(b, ng),
        in_specs=[smem, smem, col(0), col(ng), col(2 * ng), col(3 * ng), cw(0), cw(ng), cw(2 * ng),
                  gate(0), gate(ng), pl.BlockSpec((1, DN_DV), lambda bi, hi: (0, 0))],
        out_specs=pl.BlockSpec((1, s, wide), lambda bi, hi: (bi, 0, hi)),
        out_shape=jax.ShapeDtypeStruct((b, s, DN_V), BF16),
        scratch_shapes=[pltpu.VMEM((s, DN_DK), F32), pltpu.VMEM((s, DN_DK), F32), pltpu.VMEM((s, DN_DV), F32),
                        pltpu.VMEM((nc, DN_CHUNK), F32), pltpu.VMEM((s, DN_DV), F32), pltpu.VMEM((s, DN_DV), F32),
                        per_head(BF16), per_head(BF16), per_head(F32), per_head(F32),
                        pltpu.VMEM((DN_HB, nc * 8, DN_DV), F32)],
        compiler_params=pltpu.CompilerParams(dimension_semantics=("parallel", "parallel"),
                                             vmem_limit_bytes=VMEM_LIMIT),
        name="deltanet",
    )(a_log, dt_bias, proj, proj, proj, proj, conv_w, conv_w, conv_w, gates_t, gates_t, o_norm.reshape(1, DN_DV))


ATT_PIECE = 256
ATT_M_INIT = -1e30


def _attention_kernel(q_ref, kv_ref, qn_ref, kn_ref, o_ref, qf_ref, kf_ref, vf_ref, acc_ref, m_ref, l_ref):
    seq = q_ref.shape[1]
    grp = pl.program_id(1)
    hd = HEAD_DIM
    blk = ATT_BLOCK
    slopes = _alibi_slopes()

    qg = qn_ref[0] * (hd ** -0.5)
    kg = kn_ref[0]

    def prep(pi, carry):
        r0 = pl.multiple_of(pi * ATT_PIECE, ATT_PIECE)
        rows = pl.ds(r0, ATT_PIECE)
        for j in range(Q_PER_GROUP):
            cols = slice(j * hd, (j + 1) * hd)
            qf_ref[j, rows, :] = _rms(q_ref[0, rows, cols].astype(F32), 1.0) * qg
        for j in range(KV_PER_GROUP):
            cols = slice(j * hd, (j + 1) * hd)
            kf_ref[j, rows, :] = _rms(kv_ref[0, rows, cols].astype(F32), 1.0) * kg
            vcols = slice((KV_PER_GROUP + j) * hd, (KV_PER_GROUP + j + 1) * hd)
            vf_ref[j, rows, :] = kv_ref[0, rows, vcols].astype(F32)
        return carry

    lax.fori_loop(0, seq // ATT_PIECE, prep, 0)

    @pl.when(grp == 0)
    def _():
        def init(pi, carry):
            r0 = pl.multiple_of(pi * ATT_PIECE, ATT_PIECE)
            rows = pl.ds(r0, ATT_PIECE)
            for j in range(Q_PER_GROUP):
                acc_ref[j, rows, :] = jnp.zeros((ATT_PIECE, hd), F32)
                l_ref[j, rows, :] = jnp.zeros((ATT_PIECE, hd), F32)
                m_ref[j, rows, :] = jnp.full((ATT_PIECE, hd), ATT_M_INIT, F32)
            return carry

        lax.fori_loop(0, seq // ATT_PIECE, init, 0)

    def rows_of(start, size, dil):
        return pl.ds(start, size) if dil == 1 else pl.ds(start, size, stride=dil)

    def attend(g, dil, q_start, k_start, nk):
        qrows = rows_of(q_start, blk, dil)
        krows = rows_of(k_start, nk, dil)
        qi = lax.broadcasted_iota(jnp.int32, (blk, nk), 0)
        kidx = lax.broadcasted_iota(jnp.int32, (blk, nk), 1)
        dist = (nk - blk) + qi - kidx
        valid = (dist >= 0) & (dist <= blk)
        distf = dist.astype(F32)
        for kvh in range(KV_PER_GROUP):
            kb = kf_ref[kvh, krows, :].astype(BF16)
            vb = vf_ref[kvh, krows, :].astype(BF16)
            for rep in range(Q_REP):
                j = kvh * Q_REP + rep
                sc = _dot_nt(qf_ref[j, qrows, :], kb)
                sc = jnp.where(valid, sc - float(slopes[g, kvh, rep] * dil) * distf, -jnp.inf)
                m_old = m_ref[j, qrows, :]
                m_new = jnp.maximum(m_old, jnp.max(sc, axis=-1, keepdims=True))
                alpha = jnp.exp(m_old - m_new)
                p = jnp.exp(sc - m_new[:, 0:1])
                l_ref[j, qrows, :] = alpha * l_ref[j, qrows, :] + jnp.sum(p, axis=-1, keepdims=True)
                acc_ref[j, qrows, :] = alpha * acc_ref[j, qrows, :] + jnp.dot(p.astype(BF16), vb,
                                                                              preferred_element_type=F32)
                m_ref[j, qrows, :] = m_new

    for g, (window, dil) in enumerate(DIL_CONFIGS):
        sub_len = seq // dil
        nblk = sub_len // blk

        @pl.when(grp == g)
        def _(g=g, dil=dil, nblk=nblk):
            def residue(res, carry):
                attend(g, dil, res, res, blk)
                if nblk > 1:
                    def later(n, c2):
                        attend(g, dil, res + n * blk * dil, res + (n - 1) * blk * dil, 2 * blk)
                        return c2
                    lax.fori_loop(1, nblk, later, 0)
                return carry

            lax.fori_loop(0, dil, residue, 0)

    @pl.when(grp == N_ATT_GROUPS - 1)
    def _():
        def finish(pi, carry):
            r0 = pl.multiple_of(pi * ATT_PIECE, ATT_PIECE)
            rows = pl.ds(r0, ATT_PIECE)
            for j in range(Q_PER_GROUP):
                cols = slice(j * hd, (j + 1) * hd)
                o_ref[0, rows, cols] = (acc_ref[j, rows, :] / l_ref[j, rows, :]).astype(o_ref.dtype)
            return carry

        lax.fori_loop(0, seq // ATT_PIECE, finish, 0)


def dilated_attention(q, kv, q_norm, k_norm):
    b, s, _ = q.shape
    gw = GROUP_WIDTH
    return pl.pallas_call(
        _attention_kernel,
        grid=(b, N_ATT_GROUPS),
        in_specs=[pl.BlockSpec((1, s, gw), lambda bi, gi: (bi, 0, gi)),
                  pl.BlockSpec((1, s, gw), lambda bi, gi: (bi, 0, gi)),
                  pl.BlockSpec((1, 1, HEAD_DIM), lambda bi, gi: (gi, 0, 0)),
                  pl.BlockSpec((1, 1, HEAD_DIM), lambda bi, gi: (gi, 0, 0))],
        out_specs=pl.BlockSpec((1, s, gw), lambda bi, gi: (bi, 0, 0)),
        out_shape=jax.ShapeDtypeStruct((b, s, gw), BF16),
        scratch_shapes=[pltpu.VMEM((Q_PER_GROUP, s, HEAD_DIM), F32),
                        pltpu.VMEM((KV_PER_GROUP, s, HEAD_DIM), F32),
                        pltpu.VMEM((KV_PER_GROUP, s, HEAD_DIM), F32),
                        pltpu.VMEM((Q_PER_GROUP, s, HEAD_DIM), F32),
                        pltpu.VMEM((Q_PER_GROUP, s, HEAD_DIM), F32),
                        pltpu.VMEM((Q_PER_GROUP, s, HEAD_DIM), F32)],
        compiler_params=pltpu.CompilerParams(dimension_semantics=("parallel", "arbitrary"),
                                             vmem_limit_bytes=VMEM_LIMIT),
        name="dilated_attention",
    )(q, kv, q_norm.reshape(N_ATT_GROUPS, 1, HEAD_DIM), k_norm.reshape(N_ATT_GROUPS, 1, HEAD_DIM))


ROUTER_TM = 512
SEG_ALIGN = 8
SORT_ROWS = 2 * ROUTER_TM + 256
assert SORT_ROWS >= 2 * ROUTER_TM + MOE_EXPERTS * (SEG_ALIGN - 1) and SORT_ROWS % LANES == 0
META_W0, META_W1, META_P0, META_P1 = 0, 1, 2, 3
TAB_CNT, TAB_OFF, TAB_SEG = 0, 1, 2


def _router_kernel(h_ref, g_ref, w_ref, b_ref, xn_ref, meta_ref, post_ref, tab_ref, cnt_ref, carry_ref):
    tm = h_ref.shape[0]

    @pl.when(pl.program_id(0) == 0)
    def _():
        carry_ref[...] = jnp.zeros_like(carry_ref)

    xn = _rms(h_ref[...], g_ref[...])
    xn_ref[...] = xn.astype(BF16)
    xh, xl = _split2(xn)
    wh, wl = _split2(w_ref[...])
    d = lambda a, bb: jnp.dot(a, bb, preferred_element_type=F32)
    logits = d(xh, wh) + d(xh, wl) + d(xl, wh) + b_ref[...]

    lane = lax.broadcasted_iota(jnp.int32, (tm, ROUTER_LANES), 1)
    big = jnp.int32(ROUTER_LANES)
    first_where = lambda cond: jnp.min(jnp.where(cond, lane, big), axis=-1, keepdims=True)

    gl = jnp.where(lane < MOE_GROUPS, logits, -jnp.inf)
    ge = jnp.exp(gl - jnp.max(gl, axis=-1, keepdims=True))
    gp = ge / jnp.sum(ge, axis=-1, keepdims=True)
    g_w = jnp.max(gp, axis=-1, keepdims=True)
    g_idx = first_where(gp == g_w)

    lo = EXP_LANE0 + g_idx * MOE_EPG
    in_group = (lane >= lo) & (lane < lo + MOE_EPG)
    el = jnp.where(in_group, logits, -jnp.inf)
    ee = jnp.exp(el - jnp.max(el, axis=-1, keepdims=True))
    ep = ee / jnp.sum(ee, axis=-1, keepdims=True)
    p0 = jnp.max(jnp.where(in_group, ep, -1.0), axis=-1, keepdims=True)
    i0 = first_where(in_group & (ep == p0))
    rest = in_group & (lane != i0)
    p1 = jnp.max(jnp.where(rest, ep, -1.0), axis=-1, keepdims=True)
    i1 = first_where(rest & (ep == p1))
    w0 = g_w * p0 / (p0 + p1)
    w1 = g_w * p1 / (p0 + p1)

    oh0 = jnp.where(lane == i0, 1.0, 0.0)
    oh1 = jnp.where(lane == i1, 1.0, 0.0)
    both = oh0 + oh1
    ti = lax.broadcasted_iota(jnp.int32, (tm, tm), 0)
    tj = lax.broadcasted_iota(jnp.int32, (tm, tm), 1)
    before = jnp.where(tj < ti, 1.0, 0.0).astype(BF16)
    within = jnp.dot(before, both.astype(BF16), preferred_element_type=F32)
    cnt = jnp.sum(both, axis=0, keepdims=True)
    cnt_pad = jnp.floor((cnt + (SEG_ALIGN - 1)) * (1.0 / SEG_ALIGN)) * SEG_ALIGN
    li = lax.broadcasted_iota(jnp.int32, (ROUTER_LANES, ROUTER_LANES), 0)
    lj = lax.broadcasted_iota(jnp.int32, (ROUTER_LANES, ROUTER_LANES), 1)
    earlier = jnp.where(li < lj, 1.0, 0.0).astype(BF16)
    tile_off = _dot_exact01(jnp.broadcast_to(cnt_pad, (8, ROUTER_LANES)), earlier)[0:1, :]
    row = tile_off + within
    pos0 = jnp.sum(row * oh0, axis=-1, keepdims=True)
    pos1 = jnp.sum(row * oh1, axis=-1, keepdims=True)
    seg_off = carry_ref[...]
    total = seg_off + cnt_pad
    carry_ref[...] = total
    cnt_ref[...] = jnp.broadcast_to(total, cnt_ref.shape)

    sub = lax.broadcasted_iota(jnp.int32, (8, ROUTER_LANES), 0)
    tab_ref[...] = jnp.where(sub == TAB_CNT, cnt_pad, jnp.where(sub == TAB_OFF, tile_off,
                                                                jnp.where(sub == TAB_SEG, seg_off, 0.0)))
    meta = jnp.zeros((tm, ROUTER_LANES), F32)
    for idx, val in ((META_W0, w0), (META_W1, w1), (META_P0, pos0), (META_P1, pos1)):
        meta = jnp.where(lane == idx, val, meta)
    meta_ref[...] = meta
    post_ref[0] = meta.T[0:8, :]


def moe_router(h, g, w_router, b_router):
    n, k = h.shape
    tm = ROUTER_TM
    nt = n // tm
    return pl.pallas_call(
        _router_kernel,
        grid=(nt,),
        in_specs=[pl.BlockSpec((tm, k), lambda i: (i, 0)),
                  pl.BlockSpec((1, k), lambda i: (0, 0)),
                  pl.BlockSpec((k, ROUTER_LANES), lambda i: (0, 0)),
                  pl.BlockSpec((1, ROUTER_LANES), lambda i: (0, 0))],
        out_specs=[pl.BlockSpec((tm, k), lambda i: (i, 0)),
                   pl.BlockSpec((tm, ROUTER_LANES), lambda i: (i, 0)),
                   pl.BlockSpec((1, 8, tm), lambda i: (i, 0, 0)),
                   pl.BlockSpec((8, ROUTER_LANES), lambda i: (i, 0)),
                   pl.BlockSpec((8, ROUTER_LANES), lambda i: (0, 0))],
        out_shape=[jax.ShapeDtypeStruct((n, k), BF16),
                   jax.ShapeDtypeStruct((n, ROUTER_LANES), F32),
                   jax.ShapeDtypeStruct((nt, 8, tm), F32),
                   jax.ShapeDtypeStruct((nt * 8, ROUTER_LANES), F32),
                   jax.ShapeDtypeStruct((8, ROUTER_LANES), F32)],
        scratch_shapes=[pltpu.VMEM((1, ROUTER_LANES), F32)],
        compiler_params=pltpu.CompilerParams(dimension_semantics=("arbitrary",),
                                             vmem_limit_bytes=VMEM_LIMIT),
        name="moe_router",
    )(h, g.reshape(1, k), w_router, b_router)


def _pack_halves(x):
    k = x.shape[1] // 2
    lo = pltpu.bitcast(x[:, :k].astype(BF16).astype(F32), jnp.uint32)
    hi = pltpu.bitcast(x[:, k:].astype(BF16).astype(F32), jnp.uint32)
    return (hi & jnp.uint32(0xFFFF0000)) | (lo >> 16)


def _unpack_halves(w):
    lo = pltpu.bitcast(w << 16, F32)
    hi = pltpu.bitcast(w & jnp.uint32(0xFFFF0000), F32)
    return lo.astype(BF16), hi.astype(BF16)


SEG_PIECE = 2 * SEG_ALIGN


def _segment_copies(cnt_ref, tile, src_ref, src_off_ref, dst_ref, dst_off_ref, sem, wait):
    def piece(s, d, rows):
        cp = pltpu.make_async_copy(src_ref.at[pl.ds(pl.multiple_of(s, SEG_ALIGN), rows), :],
                                   dst_ref.at[pl.ds(pl.multiple_of(d, SEG_ALIGN), rows), :], sem)
        cp.wait() if wait else cp.start()

    def per_expert(e, carry):
        k = tile * MOE_EXPERTS + e
        cnt, s0, d0 = cnt_ref[k], src_off_ref[k], dst_off_ref[k]
        n_full = cnt // SEG_PIECE

        def full_piece(j, c2):
            piece(s0 + j * SEG_PIECE, d0 + j * SEG_PIECE, SEG_PIECE)
            return c2

        lax.fori_loop(0, n_full, full_piece, 0)

        @pl.when(cnt % SEG_PIECE != 0)
        def _():
            piece(s0 + n_full * SEG_PIECE, d0 + n_full * SEG_PIECE, SEG_ALIGN)

        return carry

    lax.fori_loop(0, MOE_EXPERTS, per_expert, 0)


def _dispatch_kernel(cnt_ref, off_ref, seg_ref, post_ref, x_ref, init_hbm, out_hbm, xs_ref, sem):
    del init_hbm
    tile = pl.program_id(0)
    tm = x_ref.shape[0]
    post = post_ref[0]
    p0 = post[META_P0:META_P0 + 1, :].astype(jnp.int32)
    p1 = post[META_P1:META_P1 + 1, :].astype(jnp.int32)
    r = lax.broadcasted_iota(jnp.int32, (SORT_ROWS, tm), 0)
    sel = jnp.where(r == p0, 1.0, jnp.where(r == p1, 1.0, 0.0)).astype(BF16)
    xs_ref[...] = _pack_halves(jnp.dot(sel, x_ref[...], preferred_element_type=F32))
    _segment_copies(cnt_ref, tile, xs_ref, off_ref, out_hbm, seg_ref, sem, wait=False)
    _segment_copies(cnt_ref, tile, xs_ref, off_ref, out_hbm, seg_ref, sem, wait=True)


def moe_dispatch(tab_cnt, tab_off, tab_seg, post, xn, rows):
    n, d = xn.shape
    tm = ROUTER_TM
    grid_spec = pltpu.PrefetchScalarGridSpec(
        num_scalar_prefetch=3,
        grid=(n // tm,),
        in_specs=[pl.BlockSpec((1, 8, tm), lambda i, *_: (i, 0, 0)),
                  pl.BlockSpec((tm, d), lambda i, *_: (i, 0)),
                  pl.BlockSpec(memory_space=pl.ANY)],
        out_specs=pl.BlockSpec(memory_space=pl.ANY),
        scratch_shapes=[pltpu.VMEM((SORT_ROWS, d // 2), jnp.uint32), pltpu.SemaphoreType.DMA(())],
    )
    return pl.pallas_call(
        _dispatch_kernel,
        grid_spec=grid_spec,
        out_shape=jax.ShapeDtypeStruct((rows, d // 2), jnp.uint32),
        input_output_aliases={5: 0},
        compiler_params=pltpu.CompilerParams(dimension_semantics=("arbitrary",),
                                             vmem_limit_bytes=VMEM_LIMIT),
        name="moe_dispatch",
    )(tab_cnt, tab_off, tab_seg, post, xn, jnp.zeros((rows, d // 2), jnp.uint32))


def _expert_kernel(be_ref, nvalid_ref, x_ref, w1_ref, w3_ref, w2_ref, y_ref):
    i = pl.program_id(0)

    @pl.when(i < nvalid_ref[0])
    def _():
        xlo, xhi = _unpack_halves(x_ref[...])
        half = xlo.shape[1]
        up = lambda w: (jnp.dot(xlo, w[:half, :], preferred_element_type=F32)
                        + jnp.dot(xhi, w[half:, :], preferred_element_type=F32))
        hdn = (_silu(up(w1_ref[0])) * up(w3_ref[0])).astype(BF16)
        y_ref[...] = _pack_halves(jnp.dot(hdn, w2_ref[0], preferred_element_type=F32))

    @pl.when(i >= nvalid_ref[0])
    def _():
        y_ref[...] = jnp.zeros_like(y_ref)


def moe_experts(x_sorted, block_e, n_valid, w1, w3, w2):
    rows, half = x_sorted.shape
    n_blocks = block_e.shape[0]
    bm = MOE_BM
    d, hid = w1.shape[1], w1.shape[2]
    grid_spec = pltpu.PrefetchScalarGridSpec(
        num_scalar_prefetch=2,
        grid=(n_blocks,),
        in_specs=[pl.BlockSpec((bm, half), lambda i, be, nv: (jnp.minimum(i, nv[0] - 1), 0)),
                  pl.BlockSpec((1, d, hid), lambda i, be, nv: (be[i], 0, 0)),
                  pl.BlockSpec((1, d, hid), lambda i, be, nv: (be[i], 0, 0)),
                  pl.BlockSpec((1, hid, d), lambda i, be, nv: (be[i], 0, 0))],
        out_specs=pl.BlockSpec((bm, half), lambda i, be, nv: (i, 0)),
    )
    return pl.pallas_call(
        _expert_kernel,
        grid_spec=grid_spec,
        out_shape=jax.ShapeDtypeStruct((rows, half), jnp.uint32),
        compiler_params=pltpu.CompilerParams(dimension_semantics=("arbitrary",),
                                             vmem_limit_bytes=VMEM_LIMIT),
        name="moe_experts",
    )(block_e, n_valid, x_sorted, w1, w3, w2)


def _combine_kernel(cnt_ref, off_ref, seg_ref, y_hbm, h_ref, meta_ref, o_ref, ys_ref, sem):
    tile = pl.program_id(0)
    tm, d = h_ref.shape

    @pl.when(tile == 0)
    def _():
        ys_ref[...] = jnp.zeros_like(ys_ref)

    _segment_copies(cnt_ref, tile, y_hbm, seg_ref, ys_ref, off_ref, sem, wait=False)
    _segment_copies(cnt_ref, tile, y_hbm, seg_ref, ys_ref, off_ref, sem, wait=True)
    meta = meta_ref[...]
    w0 = meta[:, META_W0:META_W0 + 1]
    w1 = meta[:, META_W1:META_W1 + 1]
    p0 = meta[:, META_P0:META_P0 + 1].astype(jnp.int32)
    p1 = meta[:, META_P1:META_P1 + 1].astype(jnp.int32)
    r = lax.broadcasted_iota(jnp.int32, (tm, SORT_ROWS), 1)
    wh, wl = _split2(jnp.where(r == p0, w0, jnp.where(r == p1, w1, 0.0)))
    ylo, yhi = _unpack_halves(ys_ref[...])
    mix = lambda y: jnp.dot(wh, y, preferred_element_type=F32) + jnp.dot(wl, y, preferred_element_type=F32)
    half = d // 2
    o_ref[:, :half] = h_ref[:, :half] + mix(ylo)
    o_ref[:, half:] = h_ref[:, half:] + mix(yhi)


def moe_combine(tab_cnt, tab_off, tab_seg, y, h, meta):
    n, d = h.shape
    tm = ROUTER_TM
    grid_spec = pltpu.PrefetchScalarGridSpec(
        num_scalar_prefetch=3,
        grid=(n // tm,),
        in_specs=[pl.BlockSpec(memory_space=pl.ANY),
                  pl.BlockSpec((tm, d), lambda i, *_: (i, 0)),
                  pl.BlockSpec((tm, ROUTER_LANES), lambda i, *_: (i, 0))],
        out_specs=pl.BlockSpec((tm, d), lambda i, *_: (i, 0)),
        scratch_shapes=[pltpu.VMEM((SORT_ROWS, d // 2), jnp.uint32), pltpu.SemaphoreType.DMA(())],
    )
    return pl.pallas_call(
        _combine_kernel,
        grid_spec=grid_spec,
        out_shape=jax.ShapeDtypeStruct((n, d), F32),
        compiler_params=pltpu.CompilerParams(dimension_semantics=("arbitrary",),
                                             vmem_limit_bytes=VMEM_LIMIT),
        name="moe_combine",
    )(tab_cnt, tab_off, tab_seg, y, h, meta)


def hierarchical_moe(h, ffn_norm, w_rg, b_rg, w_re, b_re, w1, w3, w2):
    n, d = h.shape
    pad = ROUTER_LANES - MOE_GROUPS - MOE_EXPERTS
    w_router = jnp.concatenate([w_rg, w_re, jnp.zeros((d, pad), F32)], axis=1)
    b_router = jnp.concatenate([b_rg, b_re, jnp.zeros((pad,), F32)]).reshape(1, ROUTER_LANES)
    xn, meta, post, tabs, cnt = moe_router(h, ffn_norm, w_router, b_router)

    bm = MOE_BM
    nt = n // ROUTER_TM
    lanes = slice(EXP_LANE0, EXP_LANE0 + MOE_EXPERTS)
    totals = cnt[0, lanes].astype(jnp.int32)
    region = (totals + bm - 1) // bm * bm
    region_end = jnp.cumsum(region)
    region_start = region_end - region
    n_blocks = -(-(2 * n + nt * MOE_EXPERTS * (SEG_ALIGN - 1)) // bm) + MOE_EXPERTS
    block_row0 = jnp.arange(n_blocks, dtype=jnp.int32) * bm
    block_e = jnp.minimum(jnp.sum((block_row0[:, None] >= region_end[None, :]).astype(jnp.int32), axis=1),
                          MOE_EXPERTS - 1).astype(jnp.int32)
    n_valid = (region_end[-1:] // bm).astype(jnp.int32)
    tabs = tabs.reshape(nt, 8, ROUTER_LANES)[:, :, lanes].astype(jnp.int32)
    tab_cnt = tabs[:, TAB_CNT].reshape(-1)
    tab_off = tabs[:, TAB_OFF].reshape(-1)
    tab_seg = (tabs[:, TAB_SEG] + region_start[None, :]).reshape(-1)

    x_sorted = moe_dispatch(tab_cnt, tab_off, tab_seg, post, xn, n_blocks * bm)
    y = moe_experts(x_sorted, block_e, n_valid, w1, w3, w2)
    return moe_combine(tab_cnt, tab_off, tab_seg, y, h, meta)


def _ple_kernel(h_ref, hres_ref, g_ref, wg_ref, p_ref, wp_ref, o_ref, xn_ref, pb_ref):
    @pl.when(pl.program_id(1) == 0)
    def _():
        xn_ref[...] = _rms(h_ref[...], g_ref[...]).astype(BF16)
        pb_ref[...] = p_ref[...].astype(BF16)

    gate = _sigmoid(jnp.dot(xn_ref[...], wg_ref[...], preferred_element_type=F32))
    proj = jnp.dot(pb_ref[...], wp_ref[...], preferred_element_type=F32)
    o_ref[...] = hres_ref[...] + gate * proj


def per_layer_embedding(h, p_i, g, w_gate, w_proj, tm=512, tn=512):
    n, d = h.shape
    pd = p_i.shape[1]
    return pl.pallas_call(
        _ple_kernel,
        grid=(n // tm, d // tn),
        in_specs=[pl.BlockSpec((tm, d), lambda i, j: (i, 0)),
                  pl.BlockSpec((tm, tn), lambda i, j: (i, j)),
                  pl.BlockSpec((1, d), lambda i, j: (0, 0)),
                  pl.BlockSpec((d, tn), lambda i, j: (0, j)),
                  pl.BlockSpec((tm, pd), lambda i, j: (i, 0)),
                  pl.BlockSpec((pd, tn), lambda i, j: (0, j))],
        out_specs=pl.BlockSpec((tm, tn), lambda i, j: (i, j)),
        out_shape=jax.ShapeDtypeStruct((n, d), F32),
        scratch_shapes=[pltpu.VMEM((tm, d), BF16), pltpu.VMEM((tm, pd), BF16)],
        compiler_params=pltpu.CompilerParams(dimension_semantics=("parallel", "arbitrary"),
                                             vmem_limit_bytes=VMEM_LIMIT),
        name="per_layer_embedding",
    )(h, h, g.reshape(1, d), w_gate, p_i, w_proj)


def kernel(x, p, a_norm, a_w_in, a_conv, a_A_log, a_dt_bias, a_o_norm, a_w_out, kv_norm, w_kv, k_norm, b_norm, b_w_q, b_q_norm, b_w_out, ffn_norm, w_router_group, b_router_group, w_router_expert, b_router_expert, w1, w3, w2, ple_norm, w_ple_gate, w_ple_proj):
    b, s, d = x.shape
    n = b * s
    depth = p.shape[0]
    n_a = a_norm.shape[0]
    h = x.reshape(n, d)
    kv = None
    for i in range(depth):
        if i < n_a:
            w_in = a_w_in[i]
            proj, gates_t = dn_inproj(h, a_norm[i], w_in[:, :DN_MAIN].astype(BF16), w_in[:, DN_MAIN:].T)
            o = deltanet(proj.reshape(b, s, DN_MAIN), gates_t.reshape(2 * DN_HEADS, b, s // DN_CHUNK, DN_CHUNK),
                         a_conv[i], a_A_log[i], a_dt_bias[i], a_o_norm[i])
            h = matmul_residual(o.reshape(n, DN_V), a_w_out[i].astype(BF16), h)
        else:
            bl = i - n_a
            q = norm_matmul(h, b_norm[bl], b_w_q[bl].astype(BF16), BF16)
            o = dilated_attention(q.reshape(b, s, -1), kv.reshape(b, s, -1), b_q_norm[bl], k_norm)
            h = matmul_residual(o.reshape(n, GROUP_WIDTH), b_w_out[bl].astype(BF16), h)
        h = hierarchical_moe(h, ffn_norm[i], w_router_group[i], b_router_group[i], w_router_expert[i],
                             b_router_expert[i], w1[i].astype(BF16), w3[i].astype(BF16), w2[i].astype(BF16))
        h = per_layer_embedding(h, p[i].reshape(n, -1), ple_norm[i], w_ple_gate[i].astype(BF16),
                                w_ple_proj[i].astype(BF16))
        if i == n_a - 1:
            kv = norm_matmul(h, kv_norm, w_kv.astype(BF16), BF16)
    return h.reshape(b, s, d)
```

```python
import numpy as np
import jax
import jax.numpy as jnp
from jax import lax
from jax.experimental import pallas as pl
from jax.experimental.pallas import tpu as pltpu

F32 = jnp.float32
BF16 = jnp.bfloat16

NORM_EPS = 1e-6

DN_HEADS = 8
DN_DK = 128
DN_DV = 128
DN_CONV = 4
DN_CHUNK = 128
DN_SQUARINGS = DN_CHUNK.bit_length() - 2
DN_GROUP = 4
assert DN_CHUNK == DN_DK == DN_DV
DN_QK = DN_HEADS * DN_DK
DN_V = DN_HEADS * DN_DV
DN_MAIN = 2 * DN_QK + 2 * DN_V

DIL_CONFIGS = ((128, 1), (512, 4), (2048, 16))
N_ATT_GROUPS = len(DIL_CONFIGS)
HEAD_DIM = 128
Q_PER_GROUP = 4
KV_PER_GROUP = 2
Q_REP = Q_PER_GROUP // KV_PER_GROUP
ATT_BLOCK = 128
ALIBI_MAX = 8.0
GROUP_WIDTH = Q_PER_GROUP * HEAD_DIM

MOE_GROUPS = 4
MOE_EPG = 8
MOE_EXPERTS = MOE_GROUPS * MOE_EPG
MOE_HIDDEN = 512
MOE_BM = 256
ROUTER_LANES = 128
EXP_LANE0 = MOE_GROUPS

LANES = 128
VMEM_LIMIT = 48 * 1024 * 1024


def _alibi_slopes():
    n = N_ATT_GROUPS * Q_PER_GROUP
    s = 2.0 ** (-ALIBI_MAX * np.arange(1, n + 1) / n)
    return s.reshape(N_ATT_GROUPS, KV_PER_GROUP, Q_REP)


def _rms(x, g):
    ms = jnp.mean(x * x, axis=-1, keepdims=True)
    return x * lax.rsqrt(ms + NORM_EPS) * g


def _dot(a, b):
    return jnp.dot(a.astype(BF16), b.astype(BF16), preferred_element_type=F32)


def _dot_nt(a, b):
    return lax.dot_general(a.astype(BF16), b.astype(BF16), (((1,), (1,)), ((), ())),
                           preferred_element_type=F32)


def _dot_tn(a, b):
    return lax.dot_general(a.astype(BF16), b.astype(BF16), (((0,), (0,)), ((), ())),
                           preferred_element_type=F32)


def _split2(x):
    hi = x.astype(BF16)
    lo = (x - hi.astype(F32)).astype(BF16)
    return hi, lo


def _split3(x):
    hi = x.astype(BF16)
    r = x - hi.astype(F32)
    mid = r.astype(BF16)
    lo = (r - mid.astype(F32)).astype(BF16)
    return hi, mid, lo


def _dot_exact01(x, sel):
    hi, mid, lo = _split3(x)
    d = lambda p: jnp.dot(p, sel, preferred_element_type=F32)
    return d(hi) + d(mid) + d(lo)


def _sigmoid(x):
    return 1.0 / (1.0 + jnp.exp(-x))


def _silu(x):
    return x * _sigmoid(x)


ROW_TILE = 512


def _row_tiled_call(body, name, n, row_inputs, resident_inputs, out_widths, out_dtypes, extra_out_specs=(),
                    extra_out_shapes=()):
    tm = ROW_TILE
    row_spec = lambda width: pl.BlockSpec((tm, width), lambda i: (i, 0))
    whole = lambda a: pl.BlockSpec(a.shape, lambda i: (0,) * a.ndim)
    return pl.pallas_call(
        body,
        grid=(n // tm,),
        in_specs=[row_spec(a.shape[1]) for a in row_inputs] + [whole(a) for a in resident_inputs],
        out_specs=[row_spec(w) for w in out_widths] + list(extra_out_specs),
        out_shape=[jax.ShapeDtypeStruct((n, w), dt) for w, dt in zip(out_widths, out_dtypes)]
        + list(extra_out_shapes),
        compiler_params=pltpu.CompilerParams(dimension_semantics=("parallel",), vmem_limit_bytes=VMEM_LIMIT),
        name=name,
    )(*row_inputs, *resident_inputs)


def _nm_kernel(x_ref, g_ref, w_ref, o_ref):
    xn = _rms(x_ref[...], g_ref[...]).astype(BF16)
    o_ref[...] = jnp.dot(xn, w_ref[...], preferred_element_type=F32).astype(o_ref.dtype)


def norm_matmul(x, g, w, out_dtype):
    n, k = x.shape
    return _row_tiled_call(_nm_kernel, "norm_matmul", n, [x], [g.reshape(1, k), w], [w.shape[1]], [out_dtype])[0]


def _dn_inproj_kernel(x_ref, g_ref, w_ref, wgt_ref, o_ref, gt_ref):
    xn = _rms(x_ref[...], g_ref[...])
    xh, xl = _split2(xn)
    wh, wl = _split2(wgt_ref[...])
    gt_ref[...] = _dot_nt(wh, xh) + _dot_nt(wh, xl) + _dot_nt(wl, xh)
    o_ref[...] = jnp.dot(xh, w_ref[...], preferred_element_type=F32).astype(o_ref.dtype)


def dn_inproj(x, g, w_main, w_gates_t):
    n, k = x.shape
    ng = w_gates_t.shape[0]
    return _row_tiled_call(_dn_inproj_kernel, "dn_inproj", n, [x], [g.reshape(1, k), w_main, w_gates_t],
                           [w_main.shape[1]], [BF16],
                           extra_out_specs=[pl.BlockSpec((ng, ROW_TILE), lambda i: (0, i))],
                           extra_out_shapes=[jax.ShapeDtypeStruct((ng, n), F32)])


def _mm_res_kernel(a_ref, r_ref, w_ref, o_ref):
    o_ref[...] = r_ref[...] + jnp.dot(a_ref[...], w_ref[...], preferred_element_type=F32)


def matmul_residual(a, w, res):
    return _row_tiled_call(_mm_res_kernel, "matmul_residual", a.shape[0], [a, res], [w], [w.shape[1]], [F32])[0]


DN_PIECE = 256
DN_HALO = 16
DN_HB = 4
assert DN_HEADS % DN_HB == 0


def _deltanet_kernel(alog_ref, dtb_ref, q_ref, k_ref, v_ref, z_ref, cq_ref, ck_ref, cv_ref,
                     bpre_ref, apre_ref, onorm_ref, o_ref,
                     qs_ref, ks_ref, vs_ref, gcum_ref, betac_ref, gc_ref,
                     pm_ref, rq_ref, qq_ref, o0_ref, elast_ref):
    seq = q_ref.shape[1]
    c = DN_CHUNK
    n_chunks = seq // c
    assert 2 * n_chunks <= c
    head0 = pl.program_id(1) * DN_HB
    ki = lax.broadcasted_iota(jnp.int32, (c, c), 0)
    ji = lax.broadcasted_iota(jnp.int32, (c, c), 1)
    upper = jnp.where(ki <= ji, 1.0, 0.0).astype(BF16)
    causal = ki >= ji
    strict = ki > ji
    onorm = onorm_ref[...]

    def conv_piece(x_ref, w_ref, hb, p):
        cols = slice(hb * DN_DK, (hb + 1) * DN_DK)
        w = w_ref[:, cols]
        if p == 0:
            x = x_ref[0, 0:DN_PIECE, cols].astype(F32)
            halo = 0
        else:
            x = x_ref[0, p * DN_PIECE - DN_HALO:(p + 1) * DN_PIECE, cols].astype(F32)
            halo = DN_HALO
        acc = x * w[DN_CONV - 1:DN_CONV, :]
        for j in range(1, DN_CONV):
            xs = pltpu.roll(x, j, axis=0)
            if p == 0:
                xs = jnp.where(lax.broadcasted_iota(jnp.int32, x.shape, 0) >= j, xs, 0.0)
            acc = acc + xs * w[DN_CONV - 1 - j:DN_CONV - j, :]
        return _silu(acc[halo:, :])

    def l2n(x):
        return x * lax.rsqrt(jnp.sum(x * x, axis=-1, keepdims=True) + NORM_EPS)

    def prologue(hb):
        for p in range(seq // DN_PIECE):
            rows = slice(p * DN_PIECE, (p + 1) * DN_PIECE)
            qs_ref[rows, :] = l2n(conv_piece(q_ref, cq_ref, hb, p)) * (DN_DK ** -0.5)
            ks_ref[rows, :] = l2n(conv_piece(k_ref, ck_ref, hb, p))
            vs_ref[rows, :] = conv_piece(v_ref, cv_ref, hb, p)
        beta = _sigmoid(bpre_ref[hb, 0])
        a = apre_ref[hb, 0] + dtb_ref[head0 + hb]
        softplus = jnp.maximum(a, 0.0) + jnp.log(1.0 + jnp.exp(-jnp.abs(a)))
        g_log = -jnp.exp(jnp.full(a.shape, alog_ref[head0 + hb], F32)) * softplus
        gcum = _dot_exact01(g_log, upper)
        gcum_ref[...] = gcum
        t = jnp.concatenate([beta, gcum, jnp.zeros((c - 2 * n_chunks, c), F32)], axis=0).T
        for ci in range(n_chunks):
            betac_ref[ci * c:(ci + 1) * c, :] = jnp.broadcast_to(t[:, ci:ci + 1], (c, DN_DV))
            gc_ref[ci * c:(ci + 1) * c, :] = jnp.broadcast_to(t[:, n_chunks + ci:n_chunks + ci + 1], (c, DN_DV))

    def prepare(hb, cis):
        each = lambda f, *ls: [f(*xs) for xs in zip(*ls)]
        rows = [pl.ds(pl.multiple_of(ci * c, c), c) for ci in cis]
        qc = [qs_ref[r, :] for r in rows]
        kc = [ks_ref[r, :] for r in rows]
        vc = [vs_ref[r, :] for r in rows]
        beta_c = [betac_ref[r, :] for r in rows]
        g_c = [gc_ref[r, :] for r in rows]
        g_j = [jnp.broadcast_to(gcum_ref[pl.ds(ci, 1), :], (c, c)) for ci in cis]
        decay = each(lambda gi, gj: jnp.exp(jnp.where(causal, gi - gj, -jnp.inf)), g_c, g_j)
        kq = each(lambda k, q: _dot_nt(jnp.concatenate([k, q], axis=0), k), kc, qc)
        m = each(lambda b, x, d: jnp.where(strict, -(b * x[:c, :] * d), 0.0), beta_c, kq, decay)
        pw = each(lambda x: _dot(x, x), m)
        r = m
        for _ in range(DN_SQUARINGS - 1):
            xs = each(lambda p_, r_: _dot(p_, jnp.concatenate([p_, r_], axis=1)), pw, r)
            r = each(lambda r_, p_, x: r_ + p_ + x[:, c:], r, pw, xs)
            pw = [x[:, :c] for x in xs]
        xs = each(_dot, pw, r)
        r = each(lambda r_, p_, x: r_ + p_ + x, r, pw, xs)
        e_g = [jnp.exp(g) for g in g_c]
        rhs = each(lambda b, v, e, k: jnp.concatenate([b * v, b * e * k], axis=1), beta_c, vc, e_g, kc)
        sol = each(lambda rh, r_: rh + _dot(r_, rh), rhs, r)
        attn = each(lambda x, d: jnp.where(causal, x[c:, :] * d, 0.0), kq, decay)
        k_d = each(lambda k, g: k * jnp.exp(jnp.broadcast_to(g[c - 1:c, :], (c, DN_DV)) - g), kc, g_c)
        kt = each(_dot_tn, k_d, sol)
        at = each(_dot, attn, sol)
        for i, (ci, r_) in enumerate(zip(cis, rows)):
            qq_ref[hb, r_, :] = kt[i][:, :DN_DV]
            pm_ref[hb, r_, :] = kt[i][:, DN_DV:].astype(BF16)
            o0_ref[hb, r_, :] = at[i][:, :DN_DV]
            rq_ref[hb, r_, :] = (qc[i] * e_g[i] - at[i][:, DN_DV:]).astype(BF16)
            elast_ref[hb, pl.ds(pl.multiple_of(ci * 8, 8), 8), :] = jnp.exp(
                jnp.broadcast_to(g_c[i][c - 1:c, :], (8, DN_DV)))

    for hb in range(DN_HB):
        prologue(hb)

        def prepare_group(gi, carry, hb=hb):
            prepare(hb, [gi * DN_GROUP + k for k in range(DN_GROUP)])
            return carry

        lax.fori_loop(0, n_chunks // DN_GROUP, prepare_group, 0)

    def chunk_step(ci, states):
        rows = pl.ds(pl.multiple_of(ci * c, c), c)
        xs = [_dot(jnp.concatenate([pm_ref[hb, rows, :], rq_ref[hb, rows, :]], axis=0), states[hb])
              for hb in range(DN_HB)]
        new_states = []
        for hb in range(DN_HB):
            cols = slice(hb * DN_DV, (hb + 1) * DN_DV)
            e_last = jnp.broadcast_to(elast_ref[hb, pl.ds(pl.multiple_of(ci * 8, 8), 1), :], (DN_DK, DN_DV))
            new_states.append(e_last * states[hb] - xs[hb][:c, :] + qq_ref[hb, rows, :])
            o = xs[hb][c:, :] + o0_ref[hb, rows, :]
            zc = z_ref[0, rows, cols].astype(F32)
            o_ref[0, rows, cols] = (_rms(o, onorm) * _silu(zc)).astype(o_ref.dtype)
        return tuple(new_states)

    lax.fori_loop(0, n_chunks, chunk_step, tuple(jnp.zeros((DN_DK, DN_DV), F32) for _ in range(DN_HB)))


def deltanet(proj, gates_t, conv_w, a_log, dt_bias, o_norm):
    b, s, _ = proj.shape
    ng = DN_HEADS // DN_HB
    nc = s // DN_CHUNK
    wide = DN_HB * DN_DK
    col = lambda off: pl.BlockSpec((1, s, wide), lambda bi, hi: (bi, 0, off + hi))
    cw = lambda off: pl.BlockSpec((DN_CONV, wide), lambda bi, hi: (0, off + hi))
    gate = lambda off: pl.BlockSpec((DN_HB, 1, nc, DN_CHUNK), lambda bi, hi: (off + hi, bi, 0, 0))
    smem = pl.BlockSpec(memory_space=pltpu.SMEM)
    per_head = lambda dt: pltpu.VMEM((DN_HB, s, DN_DV), dt)
    return pl.pallas_call(
        _deltanet_kernel,
        grid=(b, ng),
        in_specs=[smem, smem, col(0), col(ng), col(2 * ng), col(3 * ng), cw(0), cw(ng), cw(2 * ng),
                  gate(0), gate(ng), pl.BlockSpec((1, DN_DV), lambda bi, hi: (0, 0))],
        out_specs=pl.BlockSpec((1, s, wide), lambda bi, hi: (bi, 0, hi)),
        out_shape=jax.ShapeDtypeStruct((b, s, DN_V), BF16),
        scratch_shapes=[pltpu.VMEM((s, DN_DK), F32), pltpu.VMEM((s, DN_DK), F32), pltpu.VMEM((s, DN_DV), F32),
                        pltpu.VMEM((nc, DN_CHUNK), F32), pltpu.VMEM((s, DN_DV), F32), pltpu.VMEM((s, DN_DV), F32),
                        per_head(BF16), per_head(BF16), per_head(F32), per_head(F32),
                        pltpu.VMEM((DN_HB, nc * 8, DN_DV), F32)],
        compiler_params=pltpu.CompilerParams(dimension_semantics=("parallel", "parallel"),
                                             vmem_limit_bytes=VMEM_LIMIT),
        name="deltanet",
    )(a_log, dt_bias, proj, proj, proj, proj, conv_w, conv_w, conv_w, gates_t, gates_t, o_norm.reshape(1, DN_DV))


ATT_PIECE = 256
ATT_M_INIT = -1e30


def _attention_kernel(q_ref, kv_ref, qn_ref, kn_ref, o_ref, qf_ref, kf_ref, vf_ref, acc_ref, m_ref, l_ref):
    seq = q_ref.shape[1]
    grp = pl.program_id(1)
    hd = HEAD_DIM
    blk = ATT_BLOCK
    slopes = _alibi_slopes()

    qg = qn_ref[0] * (hd ** -0.5)
    kg = kn_ref[0]

    def prep(pi, carry):
        r0 = pl.multiple_of(pi * ATT_PIECE, ATT_PIECE)
        rows = pl.ds(r0, ATT_PIECE)
        for j in range(Q_PER_GROUP):
            cols = slice(j * hd, (j + 1) * hd)
            qf_ref[j, rows, :] = _rms(q_ref[0, rows, cols].astype(F32), 1.0) * qg
        for j in range(KV_PER_GROUP):
            cols = slice(j * hd, (j + 1) * hd)
            kf_ref[j, rows, :] = _rms(kv_ref[0, rows, cols].astype(F32), 1.0) * kg
            vcols = slice((KV_PER_GROUP + j) * hd, (KV_PER_GROUP + j + 1) * hd)
            vf_ref[j, rows, :] = kv_ref[0, rows, vcols].astype(F32)
        return carry

    lax.fori_loop(0, seq // ATT_PIECE, prep, 0)

    @pl.when(grp == 0)
    def _():
        def init(pi, carry):
            r0 = pl.multiple_of(pi * ATT_PIECE, ATT_PIECE)
            rows = pl.ds(r0, ATT_PIECE)
            for j in range(Q_PER_GROUP):
                acc_ref[j, rows, :] = jnp.zeros((ATT_PIECE, hd), F32)
                l_ref[j, rows, :] = jnp.zeros((ATT_PIECE, hd), F32)
                m_ref[j, rows, :] = jnp.full((ATT_PIECE, hd), ATT_M_INIT, F32)
            return carry

        lax.fori_loop(0, seq // ATT_PIECE, init, 0)

    def rows_of(start, size, dil):
        return pl.ds(start, size) if dil == 1 else pl.ds(start, size, stride=dil)

    def attend(g, dil, q_start, k_start, nk):
        qrows = rows_of(q_start, blk, dil)
        krows = rows_of(k_start, nk, dil)
        qi = lax.broadcasted_iota(jnp.int32, (blk, nk), 0)
        kidx = lax.broadcasted_iota(jnp.int32, (blk, nk), 1)
        dist = (nk - blk) + qi - kidx
        valid = (dist >= 0) & (dist <= blk)
        distf = dist.astype(F32)
        for kvh in range(KV_PER_GROUP):
            kb = kf_ref[kvh, krows, :].astype(BF16)
            vb = vf_ref[kvh, krows, :].astype(BF16)
            for rep in range(Q_REP):
                j = kvh * Q_REP + rep
                sc = _dot_nt(qf_ref[j, qrows, :], kb)
                sc = jnp.where(valid, sc - float(slopes[g, kvh, rep] * dil) * distf, -jnp.inf)
                m_old = m_ref[j, qrows, :]
                m_new = jnp.maximum(m_old, jnp.max(sc, axis=-1, keepdims=True))
                alpha = jnp.exp(m_old - m_new)
                p = jnp.exp(sc - m_new[:, 0:1])
                l_ref[j, qrows, :] = alpha * l_ref[j, qrows, :] + jnp.sum(p, axis=-1, keepdims=True)
                acc_ref[j, qrows, :] = alpha * acc_ref[j, qrows, :] + jnp.dot(p.astype(BF16), vb,
                                                                              preferred_element_type=F32)
                m_ref[j, qrows, :] = m_new

    for g, (window, dil) in enumerate(DIL_CONFIGS):
        sub_len = seq // dil
        nblk = sub_len // blk

        @pl.when(grp == g)
        def _(g=g, dil=dil, nblk=nblk):
            def residue(res, carry):
                attend(g, dil, res, res, blk)
                if nblk > 1:
                    def later(n, c2):
                        attend(g, dil, res + n * blk * dil, res + (n - 1) * blk * dil, 2 * blk)
                        return c2
                    lax.fori_loop(1, nblk, later, 0)
                return carry

            lax.fori_loop(0, dil, residue, 0)

    @pl.when(grp == N_ATT_GROUPS - 1)
    def _():
        def finish(pi, carry):
            r0 = pl.multiple_of(pi * ATT_PIECE, ATT_PIECE)
            rows = pl.ds(r0, ATT_PIECE)
            for j in range(Q_PER_GROUP):
                cols = slice(j * hd, (j + 1) * hd)
                o_ref[0, rows, cols] = (acc_ref[j, rows, :] / l_ref[j, rows, :]).astype(o_ref.dtype)
            return carry

        lax.fori_loop(0, seq // ATT_PIECE, finish, 0)


def dilated_attention(q, kv, q_norm, k_norm):
    b, s, _ = q.shape
    gw = GROUP_WIDTH
    return pl.pallas_call(
        _attention_kernel,
        grid=(b, N_ATT_GROUPS),
        in_specs=[pl.BlockSpec((1, s, gw), lambda bi, gi: (bi, 0, gi)),
                  pl.BlockSpec((1, s, gw), lambda bi, gi: (bi, 0, gi)),
                  pl.BlockSpec((1, 1, HEAD_DIM), lambda bi, gi: (gi, 0, 0)),
                  pl.BlockSpec((1, 1, HEAD_DIM), lambda bi, gi: (gi, 0, 0))],
        out_specs=pl.BlockSpec((1, s, gw), lambda bi, gi: (bi, 0, 0)),
        out_shape=jax.ShapeDtypeStruct((b, s, gw), BF16),
        scratch_shapes=[pltpu.VMEM((Q_PER_GROUP, s, HEAD_DIM), F32),
                        pltpu.VMEM((KV_PER_GROUP, s, HEAD_DIM), F32),
                        pltpu.VMEM((KV_PER_GROUP, s, HEAD_DIM), F32),
                        pltpu.VMEM((Q_PER_GROUP, s, HEAD_DIM), F32),
                        pltpu.VMEM((Q_PER_GROUP, s, HEAD_DIM), F32),
                        pltpu.VMEM((Q_PER_GROUP, s, HEAD_DIM), F32)],
        compiler_params=pltpu.CompilerParams(dimension_semantics=("parallel", "arbitrary"),
                                             vmem_limit_bytes=VMEM_LIMIT),
        name="dilated_attention",
    )(q, kv, q_norm.reshape(N_ATT_GROUPS, 1, HEAD_DIM), k_norm.reshape(N_ATT_GROUPS, 1, HEAD_DIM))


ROUTER_TM = 512
SEG_ALIGN = 8
SORT_ROWS = 2 * ROUTER_TM + 256
assert SORT_ROWS >= 2 * ROUTER_TM + MOE_EXPERTS * (SEG_ALIGN - 1) and SORT_ROWS % LANES == 0
META_W0, META_W1, META_P0, META_P1 = 0, 1, 2, 3
TAB_CNT, TAB_OFF, TAB_SEG = 0, 1, 2


def _router_kernel(h_ref, g_ref, w_ref, b_ref, xn_ref, meta_ref, post_ref, tab_ref, cnt_ref, carry_ref):
    tm = h_ref.shape[0]

    @pl.when(pl.program_id(0) == 0)
    def _():
        carry_ref[...] = jnp.zeros_like(carry_ref)

    xn = _rms(h_ref[...], g_ref[...])
    xn_ref[...] = xn.astype(BF16)
    xh, xl = _split2(xn)
    wh, wl = _split2(w_ref[...])
    d = lambda a, bb: jnp.dot(a, bb, preferred_element_type=F32)
    logits = d(xh, wh) + d(xh, wl) + d(xl, wh) + b_ref[...]

    lane = lax.broadcasted_iota(jnp.int32, (tm, ROUTER_LANES), 1)
    big = jnp.int32(ROUTER_LANES)
    first_where = lambda cond: jnp.min(jnp.where(cond, lane, big), axis=-1, keepdims=True)

    gl = jnp.where(lane < MOE_GROUPS, logits, -jnp.inf)
    ge = jnp.exp(gl - jnp.max(gl, axis=-1, keepdims=True))
    gp = ge / jnp.sum(ge, axis=-1, keepdims=True)
    g_w = jnp.max(gp, axis=-1, keepdims=True)
    g_idx = first_where(gp == g_w)

    lo = EXP_LANE0 + g_idx * MOE_EPG
    in_group = (lane >= lo) & (lane < lo + MOE_EPG)
    el = jnp.where(in_group, logits, -jnp.inf)
    ee = jnp.exp(el - jnp.max(el, axis=-1, keepdims=True))
    ep = ee / jnp.sum(ee, axis=-1, keepdims=True)
    p0 = jnp.max(jnp.where(in_group, ep, -1.0), axis=-1, keepdims=True)
    i0 = first_where(in_group & (ep == p0))
    rest = in_group & (lane != i0)
    p1 = jnp.max(jnp.where(rest, ep, -1.0), axis=-1, keepdims=True)
    i1 = first_where(rest & (ep == p1))
    w0 = g_w * p0 / (p0 + p1)
    w1 = g_w * p1 / (p0 + p1)

    oh0 = jnp.where(lane == i0, 1.0, 0.0)
    oh1 = jnp.where(lane == i1, 1.0, 0.0)
    both = oh0 + oh1
    ti = lax.broadcasted_iota(jnp.int32, (tm, tm), 0)
    tj = lax.broadcasted_iota(jnp.int32, (tm, tm), 1)
    before = jnp.where(tj < ti, 1.0, 0.0).astype(BF16)
    within = jnp.dot(before, both.astype(BF16), preferred_element_type=F32)
    cnt = jnp.sum(both, axis=0, keepdims=True)
    cnt_pad = jnp.floor((cnt + (SEG_ALIGN - 1)) * (1.0 / SEG_ALIGN)) * SEG_ALIGN
    li = lax.broadcasted_iota(jnp.int32, (ROUTER_LANES, ROUTER_LANES), 0)
    lj = lax.broadcasted_iota(jnp.int32, (ROUTER_LANES, ROUTER_LANES), 1)
    earlier = jnp.where(li < lj, 1.0, 0.0).astype(BF16)
    tile_off = _dot_exact01(jnp.broadcast_to(cnt_pad, (8, ROUTER_LANES)), earlier)[0:1, :]
    row = tile_off + within
    pos0 = jnp.sum(row * oh0, axis=-1, keepdims=True)
    pos1 = jnp.sum(row * oh1, axis=-1, keepdims=True)
    seg_off = carry_ref[...]
    total = seg_off + cnt_pad
    carry_ref[...] = total
    cnt_ref[...] = jnp.broadcast_to(total, cnt_ref.shape)

    sub = lax.broadcasted_iota(jnp.int32, (8, ROUTER_LANES), 0)
    tab_ref[...] = jnp.where(sub == TAB_CNT, cnt_pad, jnp.where(sub == TAB_OFF, tile_off,
                                                                jnp.where(sub == TAB_SEG, seg_off, 0.0)))
    meta = jnp.zeros((tm, ROUTER_LANES), F32)
    for idx, val in ((META_W0, w0), (META_W1, w1), (META_P0, pos0), (META_P1, pos1)):
        meta = jnp.where(lane == idx, val, meta)
    meta_ref[...] = meta
    post_ref[0] = meta.T[0:8, :]


def moe_router(h, g, w_router, b_router):
    n, k = h.shape
    tm = ROUTER_TM
    nt = n // tm
    return pl.pallas_call(
        _router_kernel,
        grid=(nt,),
        in_specs=[pl.BlockSpec((tm, k), lambda i: (i, 0)),
                  pl.BlockSpec((1, k), lambda i: (0, 0)),
                  pl.BlockSpec((k, ROUTER_LANES), lambda i: (0, 0)),
                  pl.BlockSpec((1, ROUTER_LANES), lambda i: (0, 0))],
        out_specs=[pl.BlockSpec((tm, k), lambda i: (i, 0)),
                   pl.BlockSpec((tm, ROUTER_LANES), lambda i: (i, 0)),
                   pl.BlockSpec((1, 8, tm), lambda i: (i, 0, 0)),
                   pl.BlockSpec((8, ROUTER_LANES), lambda i: (i, 0)),
                   pl.BlockSpec((8, ROUTER_LANES), lambda i: (0, 0))],
        out_shape=[jax.ShapeDtypeStruct((n, k), BF16),
                   jax.ShapeDtypeStruct((n, ROUTER_LANES), F32),
                   jax.ShapeDtypeStruct((nt, 8, tm), F32),
                   jax.ShapeDtypeStruct((nt * 8, ROUTER_LANES), F32),
                   jax.ShapeDtypeStruct((8, ROUTER_LANES), F32)],
        scratch_shapes=[pltpu.VMEM((1, ROUTER_LANES), F32)],
        compiler_params=pltpu.CompilerParams(dimension_semantics=("arbitrary",),
                                             vmem_limit_bytes=VMEM_LIMIT),
        name="moe_router",
    )(h, g.reshape(1, k), w_router, b_router)


def _pack_halves(x):
    k = x.shape[1] // 2
    lo = pltpu.bitcast(x[:, :k].astype(BF16).astype(F32), jnp.uint32)
    hi = pltpu.bitcast(x[:, k:].astype(BF16).astype(F32), jnp.uint32)
    return (hi & jnp.uint32(0xFFFF0000)) | (lo >> 16)


def _unpack_halves(w):
    lo = pltpu.bitcast(w << 16, F32)
    hi = pltpu.bitcast(w & jnp.uint32(0xFFFF0000), F32)
    return lo.astype(BF16), hi.astype(BF16)


SEG_PIECE = 2 * SEG_ALIGN


def _segment_copies(cnt_ref, tile, src_ref, src_off_ref, dst_ref, dst_off_ref, sem, wait):
    def piece(s, d, rows):
        cp = pltpu.make_async_copy(src_ref.at[pl.ds(pl.multiple_of(s, SEG_ALIGN), rows), :],
                                   dst_ref.at[pl.ds(pl.multiple_of(d, SEG_ALIGN), rows), :], sem)
        cp.wait() if wait else cp.start()

    def per_expert(e, carry):
        k = tile * MOE_EXPERTS + e
        cnt, s0, d0 = cnt_ref[k], src_off_ref[k], dst_off_ref[k]
        n_full = cnt // SEG_PIECE

        def full_piece(j, c2):
            piece(s0 + j * SEG_PIECE, d0 + j * SEG_PIECE, SEG_PIECE)
            return c2

        lax.fori_loop(0, n_full, full_piece, 0)

        @pl.when(cnt % SEG_PIECE != 0)
        def _():
            piece(s0 + n_full * SEG_PIECE, d0 + n_full * SEG_PIECE, SEG_ALIGN)

        return carry

    lax.fori_loop(0, MOE_EXPERTS, per_expert, 0)


def _dispatch_kernel(cnt_ref, off_ref, seg_ref, post_ref, x_ref, init_hbm, out_hbm, xs_ref, sem):
    del init_hbm
    tile = pl.program_id(0)
    tm = x_ref.shape[0]
    post = post_ref[0]
    p0 = post[META_P0:META_P0 + 1, :].astype(jnp.int32)
    p1 = post[META_P1:META_P1 + 1, :].astype(jnp.int32)
    r = lax.broadcasted_iota(jnp.int32, (SORT_ROWS, tm), 0)
    sel = jnp.where(r == p0, 1.0, jnp.where(r == p1, 1.0, 0.0)).astype(BF16)
    xs_ref[...] = _pack_halves(jnp.dot(sel, x_ref[...], preferred_element_type=F32))
    _segment_copies(cnt_ref, tile, xs_ref, off_ref, out_hbm, seg_ref, sem, wait=False)
    _segment_copies(cnt_ref, tile, xs_ref, off_ref, out_hbm, seg_ref, sem, wait=True)


def moe_dispatch(tab_cnt, tab_off, tab_seg, post, xn, rows):
    n, d = xn.shape
    tm = ROUTER_TM
    grid_spec = pltpu.PrefetchScalarGridSpec(
        num_scalar_prefetch=3,
        grid=(n // tm,),
        in_specs=[pl.BlockSpec((1, 8, tm), lambda i, *_: (i, 0, 0)),
                  pl.BlockSpec((tm, d), lambda i, *_: (i, 0)),
                  pl.BlockSpec(memory_space=pl.ANY)],
        out_specs=pl.BlockSpec(memory_space=pl.ANY),
        scratch_shapes=[pltpu.VMEM((SORT_ROWS, d // 2), jnp.uint32), pltpu.SemaphoreType.DMA(())],
    )
    return pl.pallas_call(
        _dispatch_kernel,
        grid_spec=grid_spec,
        out_shape=jax.ShapeDtypeStruct((rows, d // 2), jnp.uint32),
        input_output_aliases={5: 0},
        compiler_params=pltpu.CompilerParams(dimension_semantics=("arbitrary",),
                                             vmem_limit_bytes=VMEM_LIMIT),
        name="moe_dispatch",
    )(tab_cnt, tab_off, tab_seg, post, xn, jnp.zeros((rows, d // 2), jnp.uint32))


def _expert_kernel(be_ref, nvalid_ref, x_ref, w1_ref, w3_ref, w2_ref, y_ref):
    i = pl.program_id(0)

    @pl.when(i < nvalid_ref[0])
    def _():
        xlo, xhi = _unpack_halves(x_ref[...])
        half = xlo.shape[1]
        up = lambda w: (jnp.dot(xlo, w[:half, :], preferred_element_type=F32)
                        + jnp.dot(xhi, w[half:, :], preferred_element_type=F32))
        hdn = (_silu(up(w1_ref[0])) * up(w3_ref[0])).astype(BF16)
        y_ref[...] = _pack_halves(jnp.dot(hdn, w2_ref[0], preferred_element_type=F32))

    @pl.when(i >= nvalid_ref[0])
    def _():
        y_ref[...] = jnp.zeros_like(y_ref)


def moe_experts(x_sorted, block_e, n_valid, w1, w3, w2):
    rows, half = x_sorted.shape
    n_blocks = block_e.shape[0]
    bm = MOE_BM
    d, hid = w1.shape[1], w1.shape[2]
    grid_spec = pltpu.PrefetchScalarGridSpec(
        num_scalar_prefetch=2,
        grid=(n_blocks,),
        in_specs=[pl.BlockSpec((bm, half), lambda i, be, nv: (jnp.minimum(i, nv[0] - 1), 0)),
                  pl.BlockSpec((1, d, hid), lambda i, be, nv: (be[i], 0, 0)),
                  pl.BlockSpec((1, d, hid), lambda i, be, nv: (be[i], 0, 0)),
                  pl.BlockSpec((1, hid, d), lambda i, be, nv: (be[i], 0, 0))],
        out_specs=pl.BlockSpec((bm, half), lambda i, be, nv: (i, 0)),
    )
    return pl.pallas_call(
        _expert_kernel,
        grid_spec=grid_spec,
        out_shape=jax.ShapeDtypeStruct((rows, half), jnp.uint32),
        compiler_params=pltpu.CompilerParams(dimension_semantics=("arbitrary",),
                                             vmem_limit_bytes=VMEM_LIMIT),
        name="moe_experts",
    )(block_e, n_valid, x_sorted, w1, w3, w2)


def _combine_kernel(cnt_ref, off_ref, seg_ref, y_hbm, h_ref, meta_ref, o_ref, ys_ref, sem):
    tile = pl.program_id(0)
    tm, d = h_ref.shape

    @pl.when(tile == 0)
    def _():
        ys_ref[...] = jnp.zeros_like(ys_ref)

    _segment_copies(cnt_ref, tile, y_hbm, seg_ref, ys_ref, off_ref, sem, wait=False)
    _segment_copies(cnt_ref, tile, y_hbm, seg_ref, ys_ref, off_ref, sem, wait=True)
    meta = meta_ref[...]
    w0 = meta[:, META_W0:META_W0 + 1]
    w1 = meta[:, META_W1:META_W1 + 1]
    p0 = meta[:, META_P0:META_P0 + 1].astype(jnp.int32)
    p1 = meta[:, META_P1:META_P1 + 1].astype(jnp.int32)
    r = lax.broadcasted_iota(jnp.int32, (tm, SORT_ROWS), 1)
    wh, wl = _split2(jnp.where(r == p0, w0, jnp.where(r == p1, w1, 0.0)))
    ylo, yhi = _unpack_halves(ys_ref[...])
    mix = lambda y: jnp.dot(wh, y, preferred_element_type=F32) + jnp.dot(wl, y, preferred_element_type=F32)
    half = d // 2
    o_ref[:, :half] = h_ref[:, :half] + mix(ylo)
    o_ref[:, half:] = h_ref[:, half:] + mix(yhi)


def moe_combine(tab_cnt, tab_off, tab_seg, y, h, meta):
    n, d = h.shape
    tm = ROUTER_TM
    grid_spec = pltpu.PrefetchScalarGridSpec(
        num_scalar_prefetch=3,
        grid=(n // tm,),
        in_specs=[pl.BlockSpec(memory_space=pl.ANY),
                  pl.BlockSpec((tm, d), lambda i, *_: (i, 0)),
                  pl.BlockSpec((tm, ROUTER_LANES), lambda i, *_: (i, 0))],
        out_specs=pl.BlockSpec((tm, d), lambda i, *_: (i, 0)),
        scratch_shapes=[pltpu.VMEM((SORT_ROWS, d // 2), jnp.uint32), pltpu.SemaphoreType.DMA(())],
    )
    return pl.pallas_call(
        _combine_kernel,
        grid_spec=grid_spec,
        out_shape=jax.ShapeDtypeStruct((n, d), F32),
        compiler_params=pltpu.CompilerParams(dimension_semantics=("arbitrary",),
                                             vmem_limit_bytes=VMEM_LIMIT),
        name="moe_combine",
    )(tab_cnt, tab_off, tab_seg, y, h, meta)


def hierarchical_moe(h, ffn_norm, w_rg, b_rg, w_re, b_re, w1, w3, w2):
    n, d = h.shape
    pad = ROUTER_LANES - MOE_GROUPS - MOE_EXPERTS
    w_router = jnp.concatenate([w_rg, w_re, jnp.zeros((d, pad), F32)], axis=1)
    b_router = jnp.concatenate([b_rg, b_re, jnp.zeros((pad,), F32)]).reshape(1, ROUTER_LANES)
    xn, meta, post, tabs, cnt = moe_router(h, ffn_norm, w_router, b_router)

    bm = MOE_BM
    nt = n // ROUTER_TM
    lanes = slice(EXP_LANE0, EXP_LANE0 + MOE_EXPERTS)
    totals = cnt[0, lanes].astype(jnp.int32)
    region = (totals + bm - 1) // bm * bm
    region_end = jnp.cumsum(region)
    region_start = region_end - region
    n_blocks = -(-(2 * n + nt * MOE_EXPERTS * (SEG_ALIGN - 1)) // bm) + MOE_EXPERTS
    block_row0 = jnp.arange(n_blocks, dtype=jnp.int32) * bm
    block_e = jnp.minimum(jnp.sum((block_row0[:, None] >= region_end[None, :]).astype(jnp.int32), axis=1),
                          MOE_EXPERTS - 1).astype(jnp.int32)
    n_valid = (region_end[-1:] // bm).astype(jnp.int32)
    tabs = tabs.reshape(nt, 8, ROUTER_LANES)[:, :, lanes].astype(jnp.int32)
    tab_cnt = tabs[:, TAB_CNT].reshape(-1)
    tab_off = tabs[:, TAB_OFF].reshape(-1)
    tab_seg = (tabs[:, TAB_SEG] + region_start[None, :]).reshape(-1)

    x_sorted = moe_dispatch(tab_cnt, tab_off, tab_seg, post, xn, n_blocks * bm)
    y = moe_experts(x_sorted, block_e, n_valid, w1, w3, w2)
    return moe_combine(tab_cnt, tab_off, tab_seg, y, h, meta)


def _ple_kernel(h_ref, p_ref, g_ref, wg_ref, wp_ref, o_ref):
    h = h_ref[...]
    gate = _sigmoid(jnp.dot(_rms(h, g_ref[...]).astype(BF16), wg_ref[...], preferred_element_type=F32))
    proj = jnp.dot(p_ref[...].astype(BF16), wp_ref[...], preferred_element_type=F32)
    o_ref[...] = h + gate * proj


def per_layer_embedding(h, p_i, g, w_gate, w_proj):
    n, d = h.shape
    return _row_tiled_call(_ple_kernel, "per_layer_embedding", n, [h, p_i], [g.reshape(1, d), w_gate, w_proj],
                           [d], [F32])[0]


def kernel(x, p, a_norm, a_w_in, a_conv, a_A_log, a_dt_bias, a_o_norm, a_w_out, kv_norm, w_kv, k_norm, b_norm, b_w_q, b_q_norm, b_w_out, ffn_norm, w_router_group, b_router_group, w_router_expert, b_router_expert, w1, w3, w2, ple_norm, w_ple_gate, w_ple_proj):
    b, s, d = x.shape
    n = b * s
    depth = p.shape[0]
    n_a = a_norm.shape[0]
    h = x.reshape(n, d)
    kv = None
    for i in range(depth):
        if i < n_a:
            w_in = a_w_in[i]
            proj, gates_t = dn_inproj(h, a_norm[i], w_in[:, :DN_MAIN].astype(BF16), w_in[:, DN_MAIN:].T)
            o = deltanet(proj.reshape(b, s, DN_MAIN), gates_t.reshape(2 * DN_HEADS, b, s // DN_CHUNK, DN_CHUNK),
                         a_conv[i], a_A_log[i], a_dt_bias[i], a_o_norm[i])
            h = matmul_residual(o.reshape(n, DN_V), a_w_out[i].astype(BF16), h)
        else:
            bl = i - n_a
            q = norm_matmul(h, b_norm[bl], b_w_q[bl].astype(BF16), BF16)
            o = dilated_attention(q.reshape(b, s, -1), kv.reshape(b, s, -1), b_q_norm[bl], k_norm)
            h = matmul_residual(o.reshape(n, GROUP_WIDTH), b_w_out[bl].astype(BF16), h)
        h = hierarchical_moe(h, ffn_norm[i], w_router_group[i], b_router_group[i], w_router_expert[i],
                             b_router_expert[i], w1[i].astype(BF16), w3[i].astype(BF16), w2[i].astype(BF16))
        h = per_layer_embedding(h, p[i].reshape(n, -1), ple_norm[i], w_ple_gate[i].astype(BF16),
                                w_ple_proj[i].astype(BF16))
        if i == n_a - 1:
            kv = norm_matmul(h, kv_norm, w_kv.astype(BF16), BF16)
    return h.reshape(b, s, d)
```

```python
import numpy as np
import jax
import jax.numpy as jnp
from jax import lax
from jax.experimental import pallas as pl
from jax.experimental.pallas import tpu as pltpu

F32 = jnp.float32
BF16 = jnp.bfloat16

NORM_EPS = 1e-6

DN_HEADS = 8
DN_DK = 128
DN_DV = 128
DN_CONV = 4
DN_CHUNK = 128
DN_SQUARINGS = DN_CHUNK.bit_length() - 2
DN_GROUP = 4
assert DN_CHUNK == DN_DK == DN_DV
DN_QK = DN_HEADS * DN_DK
DN_V = DN_HEADS * DN_DV
DN_MAIN = 2 * DN_QK + 2 * DN_V

DIL_CONFIGS = ((128, 1), (512, 4), (2048, 16))
N_ATT_GROUPS = len(DIL_CONFIGS)
HEAD_DIM = 128
Q_PER_GROUP = 4
KV_PER_GROUP = 2
Q_REP = Q_PER_GROUP // KV_PER_GROUP
ATT_BLOCK = 128
ALIBI_MAX = 8.0
GROUP_WIDTH = Q_PER_GROUP * HEAD_DIM

MOE_GROUPS = 4
MOE_EPG = 8
MOE_EXPERTS = MOE_GROUPS * MOE_EPG
MOE_HIDDEN = 512
MOE_BM = 256
ROUTER_LANES = 128
EXP_LANE0 = MOE_GROUPS

LANES = 128
VMEM_LIMIT = 48 * 1024 * 1024


def _alibi_slopes():
    n = N_ATT_GROUPS * Q_PER_GROUP
    s = 2.0 ** (-ALIBI_MAX * np.arange(1, n + 1) / n)
    return s.reshape(N_ATT_GROUPS, KV_PER_GROUP, Q_REP)


def _rms(x, g):
    ms = jnp.mean(x * x, axis=-1, keepdims=True)
    return x * lax.rsqrt(ms + NORM_EPS) * g


def _dot(a, b):
    return jnp.dot(a.astype(BF16), b.astype(BF16), preferred_element_type=F32)


def _dot_nt(a, b):
    return lax.dot_general(a.astype(BF16), b.astype(BF16), (((1,), (1,)), ((), ())),
                           preferred_element_type=F32)


def _dot_tn(a, b):
    return lax.dot_general(a.astype(BF16), b.astype(BF16), (((0,), (0,)), ((), ())),
                           preferred_element_type=F32)


def _split2(x):
    hi = x.astype(BF16)
    lo = (x - hi.astype(F32)).astype(BF16)
    return hi, lo


def _split3(x):
    hi = x.astype(BF16)
    r = x - hi.astype(F32)
    mid = r.astype(BF16)
    lo = (r - mid.astype(F32)).astype(BF16)
    return hi, mid, lo


def _dot_exact01(x, sel):
    hi, mid, lo = _split3(x)
    d = lambda p: jnp.dot(p, sel, preferred_element_type=F32)
    return d(hi) + d(mid) + d(lo)


def _sigmoid(x):
    return 1.0 / (1.0 + jnp.exp(-x))


def _silu(x):
    return x * _sigmoid(x)


ROW_TILE = 512


def _row_tiled_call(body, name, n, row_inputs, resident_inputs, out_widths, out_dtypes, extra_out_specs=(),
                    extra_out_shapes=()):
    tm = ROW_TILE
    row_spec = lambda width: pl.BlockSpec((tm, width), lambda i: (i, 0))
    whole = lambda a: pl.BlockSpec(a.shape, lambda i: (0,) * a.ndim)

    def in_row_spec(a):
        if isinstance(a, tuple):
            arr, layer = a
            return pl.BlockSpec((None, tm, arr.shape[2]), lambda i: (layer, i, 0))
        return row_spec(a.shape[1])

    row_specs = [in_row_spec(a) for a in row_inputs]
    row_inputs = [a[0] if isinstance(a, tuple) else a for a in row_inputs]
    return pl.pallas_call(
        body,
        grid=(n // tm,),
        in_specs=row_specs + [whole(a) for a in resident_inputs],
        out_specs=[row_spec(w) for w in out_widths] + list(extra_out_specs),
        out_shape=[jax.ShapeDtypeStruct((n, w), dt) for w, dt in zip(out_widths, out_dtypes)]
        + list(extra_out_shapes),
        compiler_params=pltpu.CompilerParams(dimension_semantics=("parallel",), vmem_limit_bytes=VMEM_LIMIT),
        name=name,
    )(*row_inputs, *resident_inputs)


def _nm_kernel(x_ref, g_ref, w_ref, o_ref):
    xn = _rms(x_ref[...], g_ref[...]).astype(BF16)
    o_ref[...] = jnp.dot(xn, w_ref[...], preferred_element_type=F32).astype(o_ref.dtype)


def norm_matmul(x, g, w, out_dtype):
    n, k = x.shape
    return _row_tiled_call(_nm_kernel, "norm_matmul", n, [x], [g.reshape(1, k), w], [w.shape[1]], [out_dtype])[0]


def _dn_inproj_kernel(x_ref, g_ref, w_ref, wgt_ref, o_ref, gt_ref):
    xn = _rms(x_ref[...], g_ref[...])
    xh, xl = _split2(xn)
    wh, wl = _split2(wgt_ref[...])
    gt_ref[...] = _dot_nt(wh, xh) + _dot_nt(wh, xl) + _dot_nt(wl, xh)
    o_ref[...] = jnp.dot(xh, w_ref[...], preferred_element_type=F32).astype(o_ref.dtype)


def dn_inproj(x, g, w_main, w_gates_t):
    n, k = x.shape
    ng = w_gates_t.shape[0]
    return _row_tiled_call(_dn_inproj_kernel, "dn_inproj", n, [x], [g.reshape(1, k), w_main, w_gates_t],
                           [w_main.shape[1]], [BF16],
                           extra_out_specs=[pl.BlockSpec((ng, ROW_TILE), lambda i: (0, i))],
                           extra_out_shapes=[jax.ShapeDtypeStruct((ng, n), F32)])


def _mm_res_kernel(a_ref, r_ref, w_ref, o_ref):
    o_ref[...] = r_ref[...] + jnp.dot(a_ref[...], w_ref[...], preferred_element_type=F32)


def matmul_residual(a, w, res):
    return _row_tiled_call(_mm_res_kernel, "matmul_residual", a.shape[0], [a, res], [w], [w.shape[1]], [F32])[0]


DN_PIECE = 256
DN_HALO = 16
DN_HB = 4
assert DN_HEADS % DN_HB == 0


def _deltanet_kernel(alog_ref, dtb_ref, q_ref, k_ref, v_ref, z_ref, cq_ref, ck_ref, cv_ref,
                     bpre_ref, apre_ref, onorm_ref, o_ref,
                     qs_ref, ks_ref, vs_ref, gcum_ref, betac_ref, gc_ref,
                     pm_ref, rq_ref, qq_ref, o0_ref, elast_ref):
    seq = q_ref.shape[1]
    c = DN_CHUNK
    n_chunks = seq // c
    assert 2 * n_chunks <= c
    head0 = pl.program_id(1) * DN_HB
    ki = lax.broadcasted_iota(jnp.int32, (c, c), 0)
    ji = lax.broadcasted_iota(jnp.int32, (c, c), 1)
    upper = jnp.where(ki <= ji, 1.0, 0.0).astype(BF16)
    causal = ki >= ji
    strict = ki > ji
    onorm = onorm_ref[...]

    def conv_piece(x_ref, w_ref, hb, p):
        cols = slice(hb * DN_DK, (hb + 1) * DN_DK)
        w = w_ref[:, cols]
        if p == 0:
            x = x_ref[0, 0:DN_PIECE, cols].astype(F32)
            halo = 0
        else:
            x = x_ref[0, p * DN_PIECE - DN_HALO:(p + 1) * DN_PIECE, cols].astype(F32)
            halo = DN_HALO
        acc = x * w[DN_CONV - 1:DN_CONV, :]
        for j in range(1, DN_CONV):
            xs = pltpu.roll(x, j, axis=0)
            if p == 0:
                xs = jnp.where(lax.broadcasted_iota(jnp.int32, x.shape, 0) >= j, xs, 0.0)
            acc = acc + xs * w[DN_CONV - 1 - j:DN_CONV - j, :]
        return _silu(acc[halo:, :])

    def l2n(x):
        return x * lax.rsqrt(jnp.sum(x * x, axis=-1, keepdims=True) + NORM_EPS)

    def prologue(hb):
        for p in range(seq // DN_PIECE):
            rows = slice(p * DN_PIECE, (p + 1) * DN_PIECE)
            qs_ref[rows, :] = l2n(conv_piece(q_ref, cq_ref, hb, p)) * (DN_DK ** -0.5)
            ks_ref[rows, :] = l2n(conv_piece(k_ref, ck_ref, hb, p))
            vs_ref[rows, :] = conv_piece(v_ref, cv_ref, hb, p)
        beta = _sigmoid(bpre_ref[hb, 0])
        a = apre_ref[hb, 0] + dtb_ref[head0 + hb]
        softplus = jnp.maximum(a, 0.0) + jnp.log(1.0 + jnp.exp(-jnp.abs(a)))
        g_log = -jnp.exp(jnp.full(a.shape, alog_ref[head0 + hb], F32)) * softplus
        gcum = _dot_exact01(g_log, upper)
        gcum_ref[...] = gcum
        t = jnp.concatenate([beta, gcum, jnp.zeros((c - 2 * n_chunks, c), F32)], axis=0).T
        for ci in range(n_chunks):
            betac_ref[ci * c:(ci + 1) * c, :] = jnp.broadcast_to(t[:, ci:ci + 1], (c, DN_DV))
            gc_ref[ci * c:(ci + 1) * c, :] = jnp.broadcast_to(t[:, n_chunks + ci:n_chunks + ci + 1], (c, DN_DV))

    def prepare(hb, cis):
        each = lambda f, *ls: [f(*xs) for xs in zip(*ls)]
        rows = [pl.ds(pl.multiple_of(ci * c, c), c) for ci in cis]
        qc = [qs_ref[r, :] for r in rows]
        kc = [ks_ref[r, :] for r in rows]
        vc = [vs_ref[r, :] for r in rows]
        beta_c = [betac_ref[r, :] for r in rows]
        g_c = [gc_ref[r, :] for r in rows]
        g_j = [jnp.broadcast_to(gcum_ref[pl.ds(ci, 1), :], (c, c)) for ci in cis]
        decay = each(lambda gi, gj: jnp.exp(jnp.where(causal, gi - gj, -jnp.inf)), g_c, g_j)
        kq = each(lambda k, q: _dot_nt(jnp.concatenate([k, q], axis=0), k), kc, qc)
        m = each(lambda b, x, d: jnp.where(strict, -(b * x[:c, :] * d), 0.0), beta_c, kq, decay)
        pw = each(lambda x: _dot(x, x), m)
        r = m
        for _ in range(DN_SQUARINGS - 1):
            xs = each(lambda p_, r_: _dot(p_, jnp.concatenate([p_, r_], axis=1)), pw, r)
            r = each(lambda r_, p_, x: r_ + p_ + x[:, c:], r, pw, xs)
            pw = [x[:, :c] for x in xs]
        xs = each(_dot, pw, r)
        r = each(lambda r_, p_, x: r_ + p_ + x, r, pw, xs)
        e_g = [jnp.exp(g) for g in g_c]
        rhs = each(lambda b, v, e, k: jnp.concatenate([b * v, b * e * k], axis=1), beta_c, vc, e_g, kc)
        sol = each(lambda rh, r_: rh + _dot(r_, rh), rhs, r)
        attn = each(lambda x, d: jnp.where(causal, x[c:, :] * d, 0.0), kq, decay)
        k_d = each(lambda k, g: k * jnp.exp(jnp.broadcast_to(g[c - 1:c, :], (c, DN_DV)) - g), kc, g_c)
        kt = each(_dot_tn, k_d, sol)
        at = each(_dot, attn, sol)
        for i, (ci, r_) in enumerate(zip(cis, rows)):
            qq_ref[hb, r_, :] = kt[i][:, :DN_DV]
            pm_ref[hb, r_, :] = kt[i][:, DN_DV:].astype(BF16)
            o0_ref[hb, r_, :] = at[i][:, :DN_DV]
            rq_ref[hb, r_, :] = (qc[i] * e_g[i] - at[i][:, DN_DV:]).astype(BF16)
            elast_ref[hb, pl.ds(pl.multiple_of(ci * 8, 8), 8), :] = jnp.exp(
                jnp.broadcast_to(g_c[i][c - 1:c, :], (8, DN_DV)))

    for hb in range(DN_HB):
        prologue(hb)

        def prepare_group(gi, carry, hb=hb):
            prepare(hb, [gi * DN_GROUP + k for k in range(DN_GROUP)])
            return carry

        lax.fori_loop(0, n_chunks // DN_GROUP, prepare_group, 0)

    def chunk_step(ci, states):
        rows = pl.ds(pl.multiple_of(ci * c, c), c)
        xs = [_dot(jnp.concatenate([pm_ref[hb, rows, :], rq_ref[hb, rows, :]], axis=0), states[hb])
              for hb in range(DN_HB)]
        new_states = []
        for hb in range(DN_HB):
            cols = slice(hb * DN_DV, (hb + 1) * DN_DV)
            e_last = jnp.broadcast_to(elast_ref[hb, pl.ds(pl.multiple_of(ci * 8, 8), 1), :], (DN_DK, DN_DV))
            new_states.append(e_last * states[hb] - xs[hb][:c, :] + qq_ref[hb, rows, :])
            o = xs[hb][c:, :] + o0_ref[hb, rows, :]
            zc = z_ref[0, rows, cols].astype(F32)
            o_ref[0, rows, cols] = (_rms(o, onorm) * _silu(zc)).astype(o_ref.dtype)
        return tuple(new_states)

    lax.fori_loop(0, n_chunks, chunk_step, tuple(jnp.zeros((DN_DK, DN_DV), F32) for _ in range(DN_HB)))


def deltanet(proj, gates_t, conv_w, a_log, dt_bias, o_norm):
    b, s, _ = proj.shape
    ng = DN_HEADS // DN_HB
    nc = s // DN_CHUNK
    wide = DN_HB * DN_DK
    col = lambda off: pl.BlockSpec((1, s, wide), lambda bi, hi: (bi, 0, off + hi))
    cw = lambda off: pl.BlockSpec((DN_CONV, wide), lambda bi, hi: (0, off + hi))
    gate = lambda off: pl.BlockSpec((DN_HB, 1, nc, DN_CHUNK), lambda bi, hi: (off + hi, bi, 0, 0))
    smem = pl.BlockSpec(memory_space=pltpu.SMEM)
    per_head = lambda dt: pltpu.VMEM((DN_HB, s, DN_DV), dt)
    return pl.pallas_call(
        _deltanet_kernel,
        grid=(b, ng),
        in_specs=[smem, smem, col(0), col(ng), col(2 * ng), col(3 * ng), cw(0), cw(ng), cw(2 * ng),
                  gate(0), gate(ng), pl.BlockSpec((1, DN_DV), lambda bi, hi: (0, 0))],
        out_specs=pl.BlockSpec((1, s, wide), lambda bi, hi: (bi, 0, hi)),
        out_shape=jax.ShapeDtypeStruct((b, s, DN_V), BF16),
        scratch_shapes=[pltpu.VMEM((s, DN_DK), F32), pltpu.VMEM((s, DN_DK), F32), pltpu.VMEM((s, DN_DV), F32),
                        pltpu.VMEM((nc, DN_CHUNK), F32), pltpu.VMEM((s, DN_DV), F32), pltpu.VMEM((s, DN_DV), F32),
                        per_head(BF16), per_head(BF16), per_head(F32), per_head(F32),
                        pltpu.VMEM((DN_HB, nc * 8, DN_DV), F32)],
        compiler_params=pltpu.CompilerParams(dimension_semantics=("parallel", "parallel"),
                                             vmem_limit_bytes=VMEM_LIMIT),
        name="deltanet",
    )(a_log, dt_bias, proj, proj, proj, proj, conv_w, conv_w, conv_w, gates_t, gates_t, o_norm.reshape(1, DN_DV))


ATT_PIECE = 256
ATT_M_INIT = -1e30


def _attention_kernel(q_ref, kv_ref, qn_ref, kn_ref, o_ref, qf_ref, kf_ref, vf_ref, acc_ref, m_ref, l_ref):
    seq = q_ref.shape[1]
    grp = pl.program_id(1)
    hd = HEAD_DIM
    blk = ATT_BLOCK
    slopes = _alibi_slopes()

    qg = qn_ref[0] * (hd ** -0.5)
    kg = kn_ref[0]

    def prep(pi, carry):
        r0 = pl.multiple_of(pi * ATT_PIECE, ATT_PIECE)
        rows = pl.ds(r0, ATT_PIECE)
        for j in range(Q_PER_GROUP):
            cols = slice(j * hd, (j + 1) * hd)
            qf_ref[j, rows, :] = _rms(q_ref[0, rows, cols].astype(F32), 1.0) * qg
        for j in range(KV_PER_GROUP):
            cols = slice(j * hd, (j + 1) * hd)
            kf_ref[j, rows, :] = _rms(kv_ref[0, rows, cols].astype(F32), 1.0) * kg
            vcols = slice((KV_PER_GROUP + j) * hd, (KV_PER_GROUP + j + 1) * hd)
            vf_ref[j, rows, :] = kv_ref[0, rows, vcols].astype(F32)
        return carry

    lax.fori_loop(0, seq // ATT_PIECE, prep, 0)

    @pl.when(grp == 0)
    def _():
        def init(pi, carry):
            r0 = pl.multiple_of(pi * ATT_PIECE, ATT_PIECE)
            rows = pl.ds(r0, ATT_PIECE)
            for j in range(Q_PER_GROUP):
                acc_ref[j, rows, :] = jnp.zeros((ATT_PIECE, hd), F32)
                l_ref[j, rows, :] = jnp.zeros((ATT_PIECE, hd), F32)
                m_ref[j, rows, :] = jnp.full((ATT_PIECE, hd), ATT_M_INIT, F32)
            return carry

        lax.fori_loop(0, seq // ATT_PIECE, init, 0)

    def rows_of(start, size, dil):
        return pl.ds(start, size) if dil == 1 else pl.ds(start, size, stride=dil)

    def attend(g, dil, q_start, k_start, nk):
        qrows = rows_of(q_start, blk, dil)
        krows = rows_of(k_start, nk, dil)
        qi = lax.broadcasted_iota(jnp.int32, (blk, nk), 0)
        kidx = lax.broadcasted_iota(jnp.int32, (blk, nk), 1)
        dist = (nk - blk) + qi - kidx
        valid = (dist >= 0) & (dist <= blk)
        distf = dist.astype(F32)
        kv_heads = range(KV_PER_GROUP)
        heads = [(kvh, rep) for kvh in kv_heads for rep in range(Q_REP)]
        vb = [vf_ref[kvh, krows, :].astype(BF16) for kvh in kv_heads]
        sc2 = [_dot_nt(jnp.concatenate([qf_ref[kvh * Q_REP + rep, qrows, :] for rep in range(Q_REP)], axis=0),
                       kf_ref[kvh, krows, :]) for kvh in kv_heads]
        sc = [jnp.where(valid, sc2[kvh][rep * blk:(rep + 1) * blk, :] - float(slopes[g, kvh, rep] * dil) * distf,
                        -jnp.inf) for kvh, rep in heads]
        m_old = [m_ref[kvh * Q_REP + rep, qrows, :] for kvh, rep in heads]
        m_new = [jnp.maximum(mo, jnp.max(s, axis=-1, keepdims=True)) for mo, s in zip(m_old, sc)]
        alpha = [jnp.exp(mo - mn) for mo, mn in zip(m_old, m_new)]
        p = [jnp.exp(s - mn[:, 0:1]) for s, mn in zip(sc, m_new)]
        pv2 = [jnp.dot(jnp.concatenate([p[kvh * Q_REP + rep].astype(BF16) for rep in range(Q_REP)], axis=0),
                       vb[kvh], preferred_element_type=F32) for kvh in kv_heads]
        for i, (kvh, rep) in enumerate(heads):
            j = kvh * Q_REP + rep
            l_ref[j, qrows, :] = alpha[i] * l_ref[j, qrows, :] + jnp.sum(p[i], axis=-1, keepdims=True)
            acc_ref[j, qrows, :] = alpha[i] * acc_ref[j, qrows, :] + pv2[kvh][rep * blk:(rep + 1) * blk, :]
            m_ref[j, qrows, :] = m_new[i]

    for g, (window, dil) in enumerate(DIL_CONFIGS):
        sub_len = seq // dil
        nblk = sub_len // blk

        @pl.when(grp == g)
        def _(g=g, dil=dil, nblk=nblk):
            def residue(res, carry):
                attend(g, dil, res, res, blk)
                if nblk > 1:
                    def later(n, c2):
                        attend(g, dil, res + n * blk * dil, res + (n - 1) * blk * dil, 2 * blk)
                        return c2
                    lax.fori_loop(1, nblk, later, 0)
                return carry

            lax.fori_loop(0, dil, residue, 0)

    @pl.when(grp == N_ATT_GROUPS - 1)
    def _():
        def finish(pi, carry):
            r0 = pl.multiple_of(pi * ATT_PIECE, ATT_PIECE)
            rows = pl.ds(r0, ATT_PIECE)
            for j in range(Q_PER_GROUP):
                cols = slice(j * hd, (j + 1) * hd)
                o_ref[0, rows, cols] = (acc_ref[j, rows, :] / l_ref[j, rows, :]).astype(o_ref.dtype)
            return carry

        lax.fori_loop(0, seq // ATT_PIECE, finish, 0)


def dilated_attention(q, kv, q_norm, k_norm):
    b, s, _ = q.shape
    gw = GROUP_WIDTH
    return pl.pallas_call(
        _attention_kernel,
        grid=(b, N_ATT_GROUPS),
        in_specs=[pl.BlockSpec((1, s, gw), lambda bi, gi: (bi, 0, gi)),
                  pl.BlockSpec((1, s, gw), lambda bi, gi: (bi, 0, gi)),
                  pl.BlockSpec((1, 1, HEAD_DIM), lambda bi, gi: (gi, 0, 0)),
                  pl.BlockSpec((1, 1, HEAD_DIM), lambda bi, gi: (gi, 0, 0))],
        out_specs=pl.BlockSpec((1, s, gw), lambda bi, gi: (bi, 0, 0)),
        out_shape=jax.ShapeDtypeStruct((b, s, gw), BF16),
        scratch_shapes=[pltpu.VMEM((Q_PER_GROUP, s, HEAD_DIM), F32),
                        pltpu.VMEM((KV_PER_GROUP, s, HEAD_DIM), F32),
                        pltpu.VMEM((KV_PER_GROUP, s, HEAD_DIM), F32),
                        pltpu.VMEM((Q_PER_GROUP, s, HEAD_DIM), F32),
                        pltpu.VMEM((Q_PER_GROUP, s, HEAD_DIM), F32),
                        pltpu.VMEM((Q_PER_GROUP, s, HEAD_DIM), F32)],
        compiler_params=pltpu.CompilerParams(dimension_semantics=("parallel", "arbitrary"),
                                             vmem_limit_bytes=VMEM_LIMIT),
        name="dilated_attention",
    )(q, kv, q_norm.reshape(N_ATT_GROUPS, 1, HEAD_DIM), k_norm.reshape(N_ATT_GROUPS, 1, HEAD_DIM))


ROUTER_TM = 512
SEG_ALIGN = 8
SORT_ROWS = 2 * ROUTER_TM + 256
assert SORT_ROWS >= 2 * ROUTER_TM + MOE_EXPERTS * (SEG_ALIGN - 1) and SORT_ROWS % LANES == 0
META_W0, META_W1, META_P0, META_P1 = 0, 1, 2, 3
TAB_CNT, TAB_OFF, TAB_SEG = 0, 1, 2


def _router_kernel(h_ref, g_ref, w_ref, b_ref, xn_ref, meta_ref, post_ref, tab_ref, cnt_ref, carry_ref):
    tm = h_ref.shape[0]

    @pl.when(pl.program_id(0) == 0)
    def _():
        carry_ref[...] = jnp.zeros_like(carry_ref)

    xn = _rms(h_ref[...], g_ref[...])
    xn_ref[...] = xn.astype(BF16)
    xh, xl = _split2(xn)
    wh, wl = _split2(w_ref[...])
    d = lambda a, bb: jnp.dot(a, bb, preferred_element_type=F32)
    logits = d(xh, wh) + d(xh, wl) + d(xl, wh) + b_ref[...]

    lane = lax.broadcasted_iota(jnp.int32, (tm, ROUTER_LANES), 1)
    big = jnp.int32(ROUTER_LANES)
    first_where = lambda cond: jnp.min(jnp.where(cond, lane, big), axis=-1, keepdims=True)

    gl = jnp.where(lane < MOE_GROUPS, logits, -jnp.inf)
    ge = jnp.exp(gl - jnp.max(gl, axis=-1, keepdims=True))
    gp = ge / jnp.sum(ge, axis=-1, keepdims=True)
    g_w = jnp.max(gp, axis=-1, keepdims=True)
    g_idx = first_where(gp == g_w)

    lo = EXP_LANE0 + g_idx * MOE_EPG
    in_group = (lane >= lo) & (lane < lo + MOE_EPG)
    el = jnp.where(in_group, logits, -jnp.inf)
    ee = jnp.exp(el - jnp.max(el, axis=-1, keepdims=True))
    ep = ee / jnp.sum(ee, axis=-1, keepdims=True)
    p0 = jnp.max(jnp.where(in_group, ep, -1.0), axis=-1, keepdims=True)
    i0 = first_where(in_group & (ep == p0))
    rest = in_group & (lane != i0)
    p1 = jnp.max(jnp.where(rest, ep, -1.0), axis=-1, keepdims=True)
    i1 = first_where(rest & (ep == p1))
    w0 = g_w * p0 / (p0 + p1)
    w1 = g_w * p1 / (p0 + p1)

    oh0 = jnp.where(lane == i0, 1.0, 0.0)
    oh1 = jnp.where(lane == i1, 1.0, 0.0)
    both = oh0 + oh1
    ti = lax.broadcasted_iota(jnp.int32, (tm, tm), 0)
    tj = lax.broadcasted_iota(jnp.int32, (tm, tm), 1)
    before = jnp.where(tj < ti, 1.0, 0.0).astype(BF16)
    within = jnp.dot(before, both.astype(BF16), preferred_element_type=F32)
    cnt = jnp.sum(both, axis=0, keepdims=True)
    cnt_pad = jnp.floor((cnt + (SEG_ALIGN - 1)) * (1.0 / SEG_ALIGN)) * SEG_ALIGN
    li = lax.broadcasted_iota(jnp.int32, (ROUTER_LANES, ROUTER_LANES), 0)
    lj = lax.broadcasted_iota(jnp.int32, (ROUTER_LANES, ROUTER_LANES), 1)
    earlier = jnp.where(li < lj, 1.0, 0.0).astype(BF16)
    tile_off = _dot_exact01(jnp.broadcast_to(cnt_pad, (8, ROUTER_LANES)), earlier)[0:1, :]
    row = tile_off + within
    pos0 = jnp.sum(row * oh0, axis=-1, keepdims=True)
    pos1 = jnp.sum(row * oh1, axis=-1, keepdims=True)
    seg_off = carry_ref[...]
    total = seg_off + cnt_pad
    carry_ref[...] = total
    cnt_ref[...] = jnp.broadcast_to(total, cnt_ref.shape)

    sub = lax.broadcasted_iota(jnp.int32, (8, ROUTER_LANES), 0)
    tab_ref[...] = jnp.where(sub == TAB_CNT, cnt_pad, jnp.where(sub == TAB_OFF, tile_off,
                                                                jnp.where(sub == TAB_SEG, seg_off, 0.0)))
    meta = jnp.zeros((tm, ROUTER_LANES), F32)
    for idx, val in ((META_W0, w0), (META_W1, w1), (META_P0, pos0), (META_P1, pos1)):
        meta = jnp.where(lane == idx, val, meta)
    meta_ref[...] = meta
    post_ref[0] = meta.T[0:8, :]


def moe_router(h, g, w_router, b_router):
    n, k = h.shape
    tm = ROUTER_TM
    nt = n // tm
    return pl.pallas_call(
        _router_kernel,
        grid=(nt,),
        in_specs=[pl.BlockSpec((tm, k), lambda i: (i, 0)),
                  pl.BlockSpec((1, k), lambda i: (0, 0)),
                  pl.BlockSpec((k, ROUTER_LANES), lambda i: (0, 0)),
                  pl.BlockSpec((1, ROUTER_LANES), lambda i: (0, 0))],
        out_specs=[pl.BlockSpec((tm, k), lambda i: (i, 0)),
                   pl.BlockSpec((tm, ROUTER_LANES), lambda i: (i, 0)),
                   pl.BlockSpec((1, 8, tm), lambda i: (i, 0, 0)),
                   pl.BlockSpec((8, ROUTER_LANES), lambda i: (i, 0)),
                   pl.BlockSpec((8, ROUTER_LANES), lambda i: (0, 0))],
        out_shape=[jax.ShapeDtypeStruct((n, k), BF16),
                   jax.ShapeDtypeStruct((n, ROUTER_LANES), F32),
                   jax.ShapeDtypeStruct((nt, 8, tm), F32),
                   jax.ShapeDtypeStruct((nt * 8, ROUTER_LANES), F32),
                   jax.ShapeDtypeStruct((8, ROUTER_LANES), F32)],
        scratch_shapes=[pltpu.VMEM((1, ROUTER_LANES), F32)],
        compiler_params=pltpu.CompilerParams(dimension_semantics=("arbitrary",),
                                             vmem_limit_bytes=VMEM_LIMIT),
        name="moe_router",
    )(h, g.reshape(1, k), w_router, b_router)


def _pack_halves(x):
    k = x.shape[1] // 2
    lo = pltpu.bitcast(x[:, :k].astype(BF16).astype(F32), jnp.uint32)
    hi = pltpu.bitcast(x[:, k:].astype(BF16).astype(F32), jnp.uint32)
    return (hi & jnp.uint32(0xFFFF0000)) | (lo >> 16)


def _unpack_halves(w):
    lo = pltpu.bitcast(w << 16, F32)
    hi = pltpu.bitcast(w & jnp.uint32(0xFFFF0000), F32)
    return lo.astype(BF16), hi.astype(BF16)


SEG_PIECE = 2 * SEG_ALIGN


def _segment_copies(rows_of, src_ref, dst_ref, sem, wait):
    def piece(s, d, rows):
        cp = pltpu.make_async_copy(src_ref.at[pl.ds(pl.multiple_of(s, SEG_ALIGN), rows), :],
                                   dst_ref.at[pl.ds(pl.multiple_of(d, SEG_ALIGN), rows), :], sem)
        cp.wait() if wait else cp.start()

    def per_expert(e, carry):
        cnt, s0, d0 = rows_of(e)
        n_full = cnt // SEG_PIECE

        def full_piece(j, c2):
            piece(s0 + j * SEG_PIECE, d0 + j * SEG_PIECE, SEG_PIECE)
            return c2

        lax.fori_loop(0, n_full, full_piece, 0)

        @pl.when(cnt % SEG_PIECE != 0)
        def _():
            piece(s0 + n_full * SEG_PIECE, d0 + n_full * SEG_PIECE, SEG_ALIGN)

        return carry

    lax.fori_loop(0, MOE_EXPERTS, per_expert, 0)


def _tile_segments(cnt_ref, tile, src_off_ref, dst_off_ref):
    def rows_of(e):
        k = tile * MOE_EXPERTS + e
        return cnt_ref[k], src_off_ref[k], dst_off_ref[k]
    return rows_of


def _dispatch_kernel(cnt_ref, off_ref, seg_ref, tail_cnt_ref, tail_dst_ref, nvalid_ref, post_ref, x_ref, out_hbm,
                     xs_ref, zero_ref, sem):
    tile = pl.program_id(0)
    tm = x_ref.shape[0]
    bm = zero_ref.shape[0]
    n_blocks = out_hbm.shape[0] // bm

    @pl.when(tile == 0)
    def _():
        zero_ref[...] = jnp.zeros_like(zero_ref)
        tails = lambda e: (tail_cnt_ref[e], 0, tail_dst_ref[e])

        def unused_block(wait):
            def body(blk, carry):
                cp = pltpu.make_async_copy(zero_ref, out_hbm.at[pl.ds(pl.multiple_of(blk * bm, bm), bm), :], sem)
                cp.wait() if wait else cp.start()
                return carry
            return body

        _segment_copies(tails, zero_ref, out_hbm, sem, wait=False)
        lax.fori_loop(nvalid_ref[0], n_blocks, unused_block(False), 0)
        _segment_copies(tails, zero_ref, out_hbm, sem, wait=True)
        lax.fori_loop(nvalid_ref[0], n_blocks, unused_block(True), 0)

    post = post_ref[0]
    p0 = post[META_P0:META_P0 + 1, :].astype(jnp.int32)
    p1 = post[META_P1:META_P1 + 1, :].astype(jnp.int32)
    r = lax.broadcasted_iota(jnp.int32, (SORT_ROWS, tm), 0)
    sel = jnp.where(r == p0, 1.0, jnp.where(r == p1, 1.0, 0.0)).astype(BF16)
    xs_ref[...] = _pack_halves(jnp.dot(sel, x_ref[...], preferred_element_type=F32))
    segments = _tile_segments(cnt_ref, tile, off_ref, seg_ref)
    _segment_copies(segments, xs_ref, out_hbm, sem, wait=False)
    _segment_copies(segments, xs_ref, out_hbm, sem, wait=True)


def moe_dispatch(tab_cnt, tab_off, tab_seg, tail_cnt, tail_dst, n_valid, post, xn, rows):
    n, d = xn.shape
    tm = ROUTER_TM
    grid_spec = pltpu.PrefetchScalarGridSpec(
        num_scalar_prefetch=6,
        grid=(n // tm,),
        in_specs=[pl.BlockSpec((1, 8, tm), lambda i, *_: (i, 0, 0)),
                  pl.BlockSpec((tm, d), lambda i, *_: (i, 0))],
        out_specs=pl.BlockSpec(memory_space=pl.ANY),
        scratch_shapes=[pltpu.VMEM((SORT_ROWS, d // 2), jnp.uint32), pltpu.VMEM((MOE_BM, d // 2), jnp.uint32),
                        pltpu.SemaphoreType.DMA(())],
    )
    return pl.pallas_call(
        _dispatch_kernel,
        grid_spec=grid_spec,
        out_shape=jax.ShapeDtypeStruct((rows, d // 2), jnp.uint32),
        compiler_params=pltpu.CompilerParams(dimension_semantics=("arbitrary",),
                                             vmem_limit_bytes=VMEM_LIMIT),
        name="moe_dispatch",
    )(tab_cnt, tab_off, tab_seg, tail_cnt, tail_dst, n_valid, post, xn)


def _expert_kernel(be_ref, nvalid_ref, x_ref, w1_ref, w3_ref, w2_ref, y_ref, w1b_ref, w3b_ref, w2b_ref):
    i = pl.program_id(0)

    @pl.when(i < nvalid_ref[0])
    def _():
        @pl.when(jnp.logical_or(i == 0, be_ref[i] != be_ref[jnp.maximum(i - 1, 0)]))
        def _():
            w1b_ref[...] = w1_ref[0, 0].astype(BF16)
            w3b_ref[...] = w3_ref[0, 0].astype(BF16)
            w2b_ref[...] = w2_ref[0, 0].astype(BF16)

        xlo, xhi = _unpack_halves(x_ref[...])
        half = xlo.shape[1]
        up = lambda w_ref: (jnp.dot(xlo, w_ref[:half, :], preferred_element_type=F32)
                            + jnp.dot(xhi, w_ref[half:, :], preferred_element_type=F32))
        hdn = (_silu(up(w1b_ref)) * up(w3b_ref)).astype(BF16)
        y_ref[...] = _pack_halves(jnp.dot(hdn, w2b_ref[...], preferred_element_type=F32))

    @pl.when(i >= nvalid_ref[0])
    def _():
        y_ref[...] = jnp.zeros_like(y_ref)


def moe_experts(x_sorted, block_e, n_valid, w1, w3, w2, layer):
    rows, half = x_sorted.shape
    n_blocks = block_e.shape[0]
    bm = MOE_BM
    d, hid = w1.shape[2], w1.shape[3]
    grid_spec = pltpu.PrefetchScalarGridSpec(
        num_scalar_prefetch=2,
        grid=(n_blocks,),
        in_specs=[pl.BlockSpec((bm, half), lambda i, be, nv: (jnp.minimum(i, nv[0] - 1), 0)),
                  pl.BlockSpec((1, 1, d, hid), lambda i, be, nv: (layer, be[i], 0, 0)),
                  pl.BlockSpec((1, 1, d, hid), lambda i, be, nv: (layer, be[i], 0, 0)),
                  pl.BlockSpec((1, 1, hid, d), lambda i, be, nv: (layer, be[i], 0, 0))],
        out_specs=pl.BlockSpec((bm, half), lambda i, be, nv: (i, 0)),
        scratch_shapes=[pltpu.VMEM((d, hid), BF16), pltpu.VMEM((d, hid), BF16), pltpu.VMEM((hid, d), BF16)],
    )
    return pl.pallas_call(
        _expert_kernel,
        grid_spec=grid_spec,
        out_shape=jax.ShapeDtypeStruct((rows, half), jnp.uint32),
        compiler_params=pltpu.CompilerParams(dimension_semantics=("arbitrary",),
                                             vmem_limit_bytes=VMEM_LIMIT),
        name="moe_experts",
    )(block_e, n_valid, x_sorted, w1, w3, w2)


def _combine_kernel(cnt_ref, off_ref, seg_ref, y_hbm, h_ref, meta_ref, o_ref, ys_ref, sem):
    tile = pl.program_id(0)
    tm, d = h_ref.shape

    @pl.when(tile == 0)
    def _():
        ys_ref[...] = jnp.zeros_like(ys_ref)

    segments = _tile_segments(cnt_ref, tile, seg_ref, off_ref)
    _segment_copies(segments, y_hbm, ys_ref, sem, wait=False)
    _segment_copies(segments, y_hbm, ys_ref, sem, wait=True)
    meta = meta_ref[...]
    w0 = meta[:, META_W0:META_W0 + 1]
    w1 = meta[:, META_W1:META_W1 + 1]
    p0 = meta[:, META_P0:META_P0 + 1].astype(jnp.int32)
    p1 = meta[:, META_P1:META_P1 + 1].astype(jnp.int32)
    r = lax.broadcasted_iota(jnp.int32, (tm, SORT_ROWS), 1)
    wh, wl = _split2(jnp.where(r == p0, w0, jnp.where(r == p1, w1, 0.0)))
    ylo, yhi = _unpack_halves(ys_ref[...])
    mix = lambda y: jnp.dot(wh, y, preferred_element_type=F32) + jnp.dot(wl, y, preferred_element_type=F32)
    half = d // 2
    o_ref[:, :half] = h_ref[:, :half] + mix(ylo)
    o_ref[:, half:] = h_ref[:, half:] + mix(yhi)


def moe_combine(tab_cnt, tab_off, tab_seg, y, h, meta):
    n, d = h.shape
    tm = ROUTER_TM
    grid_spec = pltpu.PrefetchScalarGridSpec(
        num_scalar_prefetch=3,
        grid=(n // tm,),
        in_specs=[pl.BlockSpec(memory_space=pl.ANY),
                  pl.BlockSpec((tm, d), lambda i, *_: (i, 0)),
                  pl.BlockSpec((tm, ROUTER_LANES), lambda i, *_: (i, 0))],
        out_specs=pl.BlockSpec((tm, d), lambda i, *_: (i, 0)),
        scratch_shapes=[pltpu.VMEM((SORT_ROWS, d // 2), jnp.uint32), pltpu.SemaphoreType.DMA(())],
    )
    return pl.pallas_call(
        _combine_kernel,
        grid_spec=grid_spec,
        out_shape=jax.ShapeDtypeStruct((n, d), F32),
        compiler_params=pltpu.CompilerParams(dimension_semantics=("arbitrary",),
                                             vmem_limit_bytes=VMEM_LIMIT),
        name="moe_combine",
    )(tab_cnt, tab_off, tab_seg, y, h, meta)


def hierarchical_moe(h, ffn_norm, w_rg, b_rg, w_re, b_re, w1, w3, w2, layer):
    n, d = h.shape
    pad = ROUTER_LANES - MOE_GROUPS - MOE_EXPERTS
    w_router = jnp.concatenate([w_rg, w_re, jnp.zeros((d, pad), F32)], axis=1)
    b_router = jnp.concatenate([b_rg, b_re, jnp.zeros((pad,), F32)]).reshape(1, ROUTER_LANES)
    xn, meta, post, tabs, cnt = moe_router(h, ffn_norm, w_router, b_router)

    bm = MOE_BM
    nt = n // ROUTER_TM
    lanes = slice(EXP_LANE0, EXP_LANE0 + MOE_EXPERTS)
    totals = cnt[0, lanes].astype(jnp.int32)
    region = (totals + bm - 1) // bm * bm
    region_end = jnp.cumsum(region)
    region_start = region_end - region
    n_blocks = -(-(2 * n + nt * MOE_EXPERTS * (SEG_ALIGN - 1)) // bm) + MOE_EXPERTS
    block_row0 = jnp.arange(n_blocks, dtype=jnp.int32) * bm
    block_e = jnp.minimum(jnp.sum((block_row0[:, None] >= region_end[None, :]).astype(jnp.int32), axis=1),
                          MOE_EXPERTS - 1).astype(jnp.int32)
    n_valid = (region_end[-1:] // bm).astype(jnp.int32)
    tabs = tabs.reshape(nt, 8, ROUTER_LANES)[:, :, lanes].astype(jnp.int32)
    tab_cnt = tabs[:, TAB_CNT].reshape(-1)
    tab_off = tabs[:, TAB_OFF].reshape(-1)
    tab_seg = (tabs[:, TAB_SEG] + region_start[None, :]).reshape(-1)

    x_sorted = moe_dispatch(tab_cnt, tab_off, tab_seg, region - totals, region_start + totals, n_valid, post, xn,
                            n_blocks * bm)
    y = moe_experts(x_sorted, block_e, n_valid, w1, w3, w2, layer)
    return moe_combine(tab_cnt, tab_off, tab_seg, y, h, meta)


def _ple_kernel(h_ref, p_ref, g_ref, wg_ref, wp_ref, o_ref):
    h = h_ref[...]
    gate = _sigmoid(jnp.dot(_rms(h, g_ref[...]).astype(BF16), wg_ref[...], preferred_element_type=F32))
    proj = jnp.dot(p_ref[...].astype(BF16), wp_ref[...], preferred_element_type=F32)
    o_ref[...] = h + gate * proj


def per_layer_embedding(h, p, layer, g, w_gate, w_proj):
    n, d = h.shape
    return _row_tiled_call(_ple_kernel, "per_layer_embedding", n, [h, (p, layer)],
                           [g.reshape(1, d), w_gate, w_proj], [d], [F32])[0]


def kernel(x, p, a_norm, a_w_in, a_conv, a_A_log, a_dt_bias, a_o_norm, a_w_out, kv_norm, w_kv, k_norm, b_norm, b_w_q, b_q_norm, b_w_out, ffn_norm, w_router_group, b_router_group, w_router_expert, b_router_expert, w1, w3, w2, ple_norm, w_ple_gate, w_ple_proj):
    b, s, d = x.shape
    n = b * s
    depth = p.shape[0]
    n_a = a_norm.shape[0]
    h = x.reshape(n, d)
    p_rows = p.reshape(depth, n, -1)
    kv = None
    for i in range(depth):
        if i < n_a:
            w_in = a_w_in[i]
            proj, gates_t = dn_inproj(h, a_norm[i], w_in[:, :DN_MAIN].astype(BF16), w_in[:, DN_MAIN:].T)
            o = deltanet(proj.reshape(b, s, DN_MAIN), gates_t.reshape(2 * DN_HEADS, b, s // DN_CHUNK, DN_CHUNK),
                         a_conv[i], a_A_log[i], a_dt_bias[i], a_o_norm[i])
            h = matmul_residual(o.reshape(n, DN_V), a_w_out[i].astype(BF16), h)
        else:
            bl = i - n_a
            q = norm_matmul(h, b_norm[bl], b_w_q[bl].astype(BF16), BF16)
            o = dilated_attention(q.reshape(b, s, -1), kv.reshape(b, s, -1), b_q_norm[bl], k_norm)
            h = matmul_residual(o.reshape(n, GROUP_WIDTH), b_w_out[bl].astype(BF16), h)
        h = hierarchical_moe(h, ffn_norm[i], w_router_group[i], b_router_group[i], w_router_expert[i],
                             b_router_expert[i], w1, w3, w2, i)
        h = per_layer_embedding(h, p_rows, i, ple_norm[i], w_ple_gate[i].astype(BF16), w_ple_proj[i].astype(BF16))
        if i == n_a - 1:
            kv = norm_matmul(h, kv_norm, w_kv.astype(BF16), BF16)
    return h.reshape(b, s, d)
```

```python
import numpy as np
import jax
import jax.numpy as jnp
from jax import lax
from jax.experimental import pallas as pl
from jax.experimental.pallas import tpu as pltpu

F32 = jnp.float32
BF16 = jnp.bfloat16

NORM_EPS = 1e-6

DN_HEADS = 8
DN_DK = 128
DN_DV = 128
DN_CONV = 4
DN_CHUNK = 128
DN_SQUARINGS = DN_CHUNK.bit_length() - 2
DN_GROUP = 16
assert DN_CHUNK == DN_DK == DN_DV
DN_QK = DN_HEADS * DN_DK
DN_V = DN_HEADS * DN_DV
DN_MAIN = 2 * DN_QK + 2 * DN_V

DIL_CONFIGS = ((128, 1), (512, 4), (2048, 16))
N_ATT_GROUPS = len(DIL_CONFIGS)
HEAD_DIM = 128
Q_PER_GROUP = 4
KV_PER_GROUP = 2
Q_REP = Q_PER_GROUP // KV_PER_GROUP
ATT_BLOCK = 128
ALIBI_MAX = 8.0
GROUP_WIDTH = Q_PER_GROUP * HEAD_DIM

MOE_GROUPS = 4
MOE_EPG = 8
MOE_EXPERTS = MOE_GROUPS * MOE_EPG
MOE_HIDDEN = 512
MOE_BM = 256
ROUTER_LANES = 128
EXP_LANE0 = MOE_GROUPS

LANES = 128
VMEM_LIMIT = 48 * 1024 * 1024
VMEM_LIMIT_DELTANET = 56 * 1024 * 1024


def _alibi_slopes():
    n = N_ATT_GROUPS * Q_PER_GROUP
    s = 2.0 ** (-ALIBI_MAX * np.arange(1, n + 1) / n)
    return s.reshape(N_ATT_GROUPS, KV_PER_GROUP, Q_REP)


def _rms(x, g):
    ms = jnp.mean(x * x, axis=-1, keepdims=True)
    return x * lax.rsqrt(ms + NORM_EPS) * g


def _dot(a, b):
    return jnp.dot(a.astype(BF16), b.astype(BF16), preferred_element_type=F32)


def _dot_nt(a, b):
    return lax.dot_general(a.astype(BF16), b.astype(BF16), (((1,), (1,)), ((), ())),
                           preferred_element_type=F32)


def _dot_tn(a, b):
    return lax.dot_general(a.astype(BF16), b.astype(BF16), (((0,), (0,)), ((), ())),
                           preferred_element_type=F32)


def _split2(x):
    hi = x.astype(BF16)
    lo = (x - hi.astype(F32)).astype(BF16)
    return hi, lo


def _split3(x):
    hi = x.astype(BF16)
    r = x - hi.astype(F32)
    mid = r.astype(BF16)
    lo = (r - mid.astype(F32)).astype(BF16)
    return hi, mid, lo


def _dot_exact01(x, sel):
    hi, mid, lo = _split3(x)
    d = lambda p: jnp.dot(p, sel, preferred_element_type=F32)
    return d(hi) + d(mid) + d(lo)


def _aligned(i, m):
    return i if isinstance(i, int) else pl.multiple_of(i, m)


def _sigmoid(x):
    return 1.0 / (1.0 + jnp.exp(-x))


def _silu(x):
    return x * _sigmoid(x)


ROW_TILE = 512


def _row_tiled_call(body, name, n, row_inputs, resident_inputs, out_widths, out_dtypes, extra_out_specs=(),
                    extra_out_shapes=()):
    tm = ROW_TILE
    row_spec = lambda width: pl.BlockSpec((tm, width), lambda i: (i, 0))
    whole = lambda a: pl.BlockSpec(a.shape, lambda i: (0,) * a.ndim)

    def in_row_spec(a):
        if isinstance(a, tuple):
            arr, layer = a
            return pl.BlockSpec((None, tm, arr.shape[2]), lambda i: (layer, i, 0))
        return row_spec(a.shape[1])

    row_specs = [in_row_spec(a) for a in row_inputs]
    row_inputs = [a[0] if isinstance(a, tuple) else a for a in row_inputs]
    return pl.pallas_call(
        body,
        grid=(n // tm,),
        in_specs=row_specs + [whole(a) for a in resident_inputs],
        out_specs=[row_spec(w) for w in out_widths] + list(extra_out_specs),
        out_shape=[jax.ShapeDtypeStruct((n, w), dt) for w, dt in zip(out_widths, out_dtypes)]
        + list(extra_out_shapes),
        compiler_params=pltpu.CompilerParams(dimension_semantics=("parallel",), vmem_limit_bytes=VMEM_LIMIT),
        name=name,
    )(*row_inputs, *resident_inputs)


def _nm_kernel(x_ref, g_ref, w_ref, o_ref):
    xn = _rms(x_ref[...], g_ref[...]).astype(BF16)
    o_ref[...] = jnp.dot(xn, w_ref[...], preferred_element_type=F32).astype(o_ref.dtype)


def norm_matmul(x, g, w, out_dtype):
    n, k = x.shape
    return _row_tiled_call(_nm_kernel, "norm_matmul", n, [x], [g.reshape(1, k), w], [w.shape[1]], [out_dtype])[0]


def _dn_inproj_kernel(x_ref, g_ref, w_ref, wgt_ref, o_ref, gt_ref):
    xn = _rms(x_ref[...], g_ref[...])
    xh, xl = _split2(xn)
    wh, wl = _split2(wgt_ref[...])
    gt_ref[...] = _dot_nt(wh, xh) + _dot_nt(wh, xl) + _dot_nt(wl, xh)
    o_ref[...] = jnp.dot(xh, w_ref[...], preferred_element_type=F32).astype(o_ref.dtype)


def dn_inproj(x, g, w_main, w_gates_t):
    n, k = x.shape
    ng = w_gates_t.shape[0]
    return _row_tiled_call(_dn_inproj_kernel, "dn_inproj", n, [x], [g.reshape(1, k), w_main, w_gates_t],
                           [w_main.shape[1]], [BF16],
                           extra_out_specs=[pl.BlockSpec((ng, ROW_TILE), lambda i: (0, i))],
                           extra_out_shapes=[jax.ShapeDtypeStruct((ng, n), F32)])


def _mm_res_kernel(a_ref, r_ref, w_ref, o_ref):
    o_ref[...] = r_ref[...] + jnp.dot(a_ref[...], w_ref[...], preferred_element_type=F32)


def matmul_residual(a, w, res):
    return _row_tiled_call(_mm_res_kernel, "matmul_residual", a.shape[0], [a, res], [w], [w.shape[1]], [F32])[0]


DN_PIECE = 256
DN_HALO = 8
DN_HB = 4
assert DN_HEADS % DN_HB == 0


def _deltanet_kernel(alog_ref, dtb_ref, q_ref, k_ref, v_ref, z_ref, cq_ref, ck_ref, cv_ref,
                     bpre_ref, apre_ref, onorm_ref, o_ref,
                     xf_ref, qs_ref, ks_ref, vs_ref, gcum_ref, betac_ref, gc_ref,
                     pm_ref, rq_ref, qq_ref, o0_ref, elast_ref):
    seq = q_ref.shape[1]
    c = DN_CHUNK
    n_chunks = seq // c
    assert 2 * n_chunks <= c
    head0 = pl.program_id(1) * DN_HB
    ki = lax.broadcasted_iota(jnp.int32, (c, c), 0)
    ji = lax.broadcasted_iota(jnp.int32, (c, c), 1)
    upper = jnp.where(ki <= ji, 1.0, 0.0).astype(BF16)
    causal = ki >= ji
    strict = ki > ji
    onorm = onorm_ref[...]

    xf_ref[0:DN_HALO, :] = jnp.zeros((DN_HALO, DN_DK), F32)

    def conv_silu(x_ref, w_ref, hb, finish, out_ref):
        cols = slice(hb * DN_DK, (hb + 1) * DN_DK)
        w = w_ref[:, cols]
        pieces = [slice(p * DN_PIECE, (p + 1) * DN_PIECE) for p in range(seq // DN_PIECE)]
        for rows in pieces:
            xf_ref[DN_HALO + rows.start:DN_HALO + rows.stop, :] = x_ref[0, rows, cols].astype(F32)
        for rows in pieces:
            acc = xf_ref[DN_HALO + rows.start:DN_HALO + rows.stop, :] * w[DN_CONV - 1:DN_CONV, :]
            for j in range(1, DN_CONV):
                acc = acc + xf_ref[DN_HALO + rows.start - j:DN_HALO + rows.stop - j, :] * w[DN_CONV - 1 - j:DN_CONV - j, :]
            out_ref[rows, :] = finish(_silu(acc))

    def l2n(scale):
        return lambda x: x * (lax.rsqrt(jnp.sum(x * x, axis=-1, keepdims=True) + NORM_EPS) * scale)

    def prologue(hb):
        conv_silu(q_ref, cq_ref, hb, l2n(DN_DK ** -0.5), qs_ref)
        conv_silu(k_ref, ck_ref, hb, l2n(1.0), ks_ref)
        conv_silu(v_ref, cv_ref, hb, lambda x: x, vs_ref)
        beta = _sigmoid(bpre_ref[hb, 0])
        a = apre_ref[hb, 0] + dtb_ref[head0 + hb]
        softplus = jnp.maximum(a, 0.0) + jnp.log(1.0 + jnp.exp(-jnp.abs(a)))
        g_log = -jnp.exp(jnp.full(a.shape, alog_ref[head0 + hb], F32)) * softplus
        gcum = _dot_exact01(g_log, upper)
        gcum_ref[...] = gcum
        t = jnp.concatenate([beta, gcum, jnp.zeros((c - 2 * n_chunks, c), F32)], axis=0).T
        for ci in range(n_chunks):
            betac_ref[ci * c:(ci + 1) * c, :] = jnp.broadcast_to(t[:, ci:ci + 1], (c, DN_DV))
            gc_ref[ci * c:(ci + 1) * c, :] = jnp.broadcast_to(t[:, n_chunks + ci:n_chunks + ci + 1], (c, DN_DV))

    def prepare(hb, cis):
        each = lambda f, *ls: [f(*xs) for xs in zip(*ls)]
        rows = [pl.ds(_aligned(ci * c, c), c) for ci in cis]
        qc = [qs_ref[r, :] for r in rows]
        kc = [ks_ref[r, :] for r in rows]
        vc = [vs_ref[r, :] for r in rows]
        beta_c = [betac_ref[r, :] for r in rows]
        g_c = [gc_ref[r, :] for r in rows]
        g_j = [jnp.broadcast_to(gcum_ref[pl.ds(ci, 1), :], (c, c)) for ci in cis]
        decay = each(lambda gi, gj: jnp.exp(jnp.where(causal, gi - gj, -jnp.inf)), g_c, g_j)
        kq = each(lambda k, q: _dot_nt(jnp.concatenate([k, q], axis=0), k), kc, qc)
        m = each(lambda b, x, d: jnp.where(strict, -(b * x[:c, :] * d), 0.0), beta_c, kq, decay)
        pw = each(lambda x: _dot(x, x), m)
        r = m
        for _ in range(DN_SQUARINGS - 1):
            xs = each(lambda p_, r_: _dot(p_, jnp.concatenate([p_, r_], axis=1)), pw, r)
            r = each(lambda r_, p_, x: r_ + p_ + x[:, c:], r, pw, xs)
            pw = [x[:, :c] for x in xs]
        xs = each(_dot, pw, r)
        r = each(lambda r_, p_, x: r_ + p_ + x, r, pw, xs)
        e_g = [jnp.exp(g) for g in g_c]
        rhs = each(lambda b, v, e, k: jnp.concatenate([b * v, b * e * k], axis=1), beta_c, vc, e_g, kc)
        sol = each(lambda rh, r_: rh + _dot(r_, rh), rhs, r)
        attn = each(lambda x, d: jnp.where(causal, x[c:, :] * d, 0.0), kq, decay)
        k_d = each(lambda k, g: k * jnp.exp(jnp.broadcast_to(g[c - 1:c, :], (c, DN_DV)) - g), kc, g_c)
        kt = each(_dot_tn, k_d, sol)
        at = each(_dot, attn, sol)
        for i, (ci, r_) in enumerate(zip(cis, rows)):
            qq_ref[hb, r_, :] = kt[i][:, :DN_DV]
            pm_ref[hb, r_, :] = kt[i][:, DN_DV:].astype(BF16)
            o0_ref[hb, r_, :] = at[i][:, :DN_DV]
            rq_ref[hb, r_, :] = (qc[i] * e_g[i] - at[i][:, DN_DV:]).astype(BF16)
            elast_ref[hb, pl.ds(_aligned(ci * 8, 8), 8), :] = jnp.exp(
                jnp.broadcast_to(g_c[i][c - 1:c, :], (8, DN_DV)))

    group = min(DN_GROUP, n_chunks)
    assert n_chunks % group == 0
    for hb in range(DN_HB):
        prologue(hb)
        if group == n_chunks:
            prepare(hb, list(range(n_chunks)))
        else:
            def prepare_group(gi, carry, hb=hb):
                prepare(hb, [gi * group + k for k in range(group)])
                return carry

            lax.fori_loop(0, n_chunks // group, prepare_group, 0)

    def chunk_step(ci, states):
        rows = pl.ds(pl.multiple_of(ci * c, c), c)
        xs = [_dot(jnp.concatenate([pm_ref[hb, rows, :], rq_ref[hb, rows, :]], axis=0), states[hb])
              for hb in range(DN_HB)]
        new_states = []
        for hb in range(DN_HB):
            cols = slice(hb * DN_DV, (hb + 1) * DN_DV)
            e_last = jnp.broadcast_to(elast_ref[hb, pl.ds(pl.multiple_of(ci * 8, 8), 1), :], (DN_DK, DN_DV))
            new_states.append(e_last * states[hb] - xs[hb][:c, :] + qq_ref[hb, rows, :])
            o = xs[hb][c:, :] + o0_ref[hb, rows, :]
            zc = z_ref[0, rows, cols].astype(F32)
            o_ref[0, rows, cols] = (_rms(o, onorm) * _silu(zc)).astype(o_ref.dtype)
        return tuple(new_states)

    lax.fori_loop(0, n_chunks, chunk_step, tuple(jnp.zeros((DN_DK, DN_DV), F32) for _ in range(DN_HB)))


def deltanet(proj, gates_t, conv_w, a_log, dt_bias, o_norm):
    b, s, _ = proj.shape
    ng = DN_HEADS // DN_HB
    nc = s // DN_CHUNK
    wide = DN_HB * DN_DK
    col = lambda off: pl.BlockSpec((1, s, wide), lambda bi, hi: (bi, 0, off + hi))
    cw = lambda off: pl.BlockSpec((DN_CONV, wide), lambda bi, hi: (0, off + hi))
    gate = lambda off: pl.BlockSpec((DN_HB, 1, nc, DN_CHUNK), lambda bi, hi: (off + hi, bi, 0, 0))
    smem = pl.BlockSpec(memory_space=pltpu.SMEM)
    per_head = lambda dt: pltpu.VMEM((DN_HB, s, DN_DV), dt)
    return pl.pallas_call(
        _deltanet_kernel,
        grid=(b, ng),
        in_specs=[smem, smem, col(0), col(ng), col(2 * ng), col(3 * ng), cw(0), cw(ng), cw(2 * ng),
                  gate(0), gate(ng), pl.BlockSpec((1, DN_DV), lambda bi, hi: (0, 0))],
        out_specs=pl.BlockSpec((1, s, wide), lambda bi, hi: (bi, 0, hi)),
        out_shape=jax.ShapeDtypeStruct((b, s, DN_V), BF16),
        scratch_shapes=[pltpu.VMEM((DN_HALO + s, DN_DK), F32),
                        pltpu.VMEM((s, DN_DK), F32), pltpu.VMEM((s, DN_DK), F32), pltpu.VMEM((s, DN_DV), F32),
                        pltpu.VMEM((nc, DN_CHUNK), F32), pltpu.VMEM((s, DN_DV), F32), pltpu.VMEM((s, DN_DV), F32),
                        per_head(BF16), per_head(BF16), per_head(F32), per_head(F32),
                        pltpu.VMEM((DN_HB, nc * 8, DN_DV), F32)],
        compiler_params=pltpu.CompilerParams(dimension_semantics=("parallel", "parallel"),
                                             vmem_limit_bytes=VMEM_LIMIT_DELTANET),
        name="deltanet",
    )(a_log, dt_bias, proj, proj, proj, proj, conv_w, conv_w, conv_w, gates_t, gates_t, o_norm.reshape(1, DN_DV))


ATT_PIECE = 256
ATT_M_INIT = -1e30


def _attention_kernel(q_ref, kv_ref, qn_ref, kn_ref, o_ref, qf_ref, kf_ref, vf_ref, acc_ref, m_ref, l_ref):
    seq = q_ref.shape[1]
    grp = pl.program_id(1)
    hd = HEAD_DIM
    blk = ATT_BLOCK
    slopes = _alibi_slopes()

    qg = qn_ref[0] * (hd ** -0.5)
    kg = kn_ref[0]

    def prep(pi, carry):
        r0 = pl.multiple_of(pi * ATT_PIECE, ATT_PIECE)
        rows = pl.ds(r0, ATT_PIECE)
        for j in range(Q_PER_GROUP):
            cols = slice(j * hd, (j + 1) * hd)
            qf_ref[j, rows, :] = _rms(q_ref[0, rows, cols].astype(F32), 1.0) * qg
        for j in range(KV_PER_GROUP):
            cols = slice(j * hd, (j + 1) * hd)
            kf_ref[j, rows, :] = _rms(kv_ref[0, rows, cols].astype(F32), 1.0) * kg
            vcols = slice((KV_PER_GROUP + j) * hd, (KV_PER_GROUP + j + 1) * hd)
            vf_ref[j, rows, :] = kv_ref[0, rows, vcols].astype(F32)
        return carry

    lax.fori_loop(0, seq // ATT_PIECE, prep, 0)

    @pl.when(grp == 0)
    def _():
        def init(pi, carry):
            r0 = pl.multiple_of(pi * ATT_PIECE, ATT_PIECE)
            rows = pl.ds(r0, ATT_PIECE)
            for j in range(Q_PER_GROUP):
                acc_ref[j, rows, :] = jnp.zeros((ATT_PIECE, hd), F32)
                l_ref[j, rows, :] = jnp.zeros((ATT_PIECE, hd), F32)
                m_ref[j, rows, :] = jnp.full((ATT_PIECE, hd), ATT_M_INIT, F32)
            return carry

        lax.fori_loop(0, seq // ATT_PIECE, init, 0)

    def rows_of(start, size, dil):
        return pl.ds(start, size) if dil == 1 else pl.ds(start, size, stride=dil)

    def attend(g, dil, q_start, k_start, nk):
        qrows = rows_of(q_start, blk, dil)
        krows = rows_of(k_start, nk, dil)
        qi = lax.broadcasted_iota(jnp.int32, (blk, nk), 0)
        kidx = lax.broadcasted_iota(jnp.int32, (blk, nk), 1)
        dist = (nk - blk) + qi - kidx
        valid = (dist >= 0) & (dist <= blk)
        distf = dist.astype(F32)
        kv_heads = range(KV_PER_GROUP)
        heads = [(kvh, rep) for kvh in kv_heads for rep in range(Q_REP)]
        vb = [vf_ref[kvh, krows, :].astype(BF16) for kvh in kv_heads]
        sc2 = [_dot_nt(jnp.concatenate([qf_ref[kvh * Q_REP + rep, qrows, :] for rep in range(Q_REP)], axis=0),
                       kf_ref[kvh, krows, :]) for kvh in kv_heads]
        sc = [jnp.where(valid, sc2[kvh][rep * blk:(rep + 1) * blk, :] - float(slopes[g, kvh, rep] * dil) * distf,
                        -jnp.inf) for kvh, rep in heads]
        m_old = [m_ref[kvh * Q_REP + rep, qrows, :] for kvh, rep in heads]
        m_new = [jnp.maximum(mo, jnp.max(s, axis=-1, keepdims=True)) for mo, s in zip(m_old, sc)]
        alpha = [jnp.exp(mo - mn) for mo, mn in zip(m_old, m_new)]
        p = [jnp.exp(s - mn[:, 0:1]) for s, mn in zip(sc, m_new)]
        pv2 = [jnp.dot(jnp.concatenate([p[kvh * Q_REP + rep].astype(BF16) for rep in range(Q_REP)], axis=0),
                       vb[kvh], preferred_element_type=F32) for kvh in kv_heads]
        for i, (kvh, rep) in enumerate(heads):
            j = kvh * Q_REP + rep
            l_ref[j, qrows, :] = alpha[i] * l_ref[j, qrows, :] + jnp.sum(p[i], axis=-1, keepdims=True)
            acc_ref[j, qrows, :] = alpha[i] * acc_ref[j, qrows, :] + pv2[kvh][rep * blk:(rep + 1) * blk, :]
            m_ref[j, qrows, :] = m_new[i]

    for g, (window, dil) in enumerate(DIL_CONFIGS):
        sub_len = seq // dil
        nblk = sub_len // blk

        @pl.when(grp == g)
        def _(g=g, dil=dil, nblk=nblk):
            def residue(res, carry):
                attend(g, dil, res, res, blk)
                if nblk > 1:
                    def later(n, c2):
                        attend(g, dil, res + n * blk * dil, res + (n - 1) * blk * dil, 2 * blk)
                        return c2
                    lax.fori_loop(1, nblk, later, 0)
                return carry

            lax.fori_loop(0, dil, residue, 0)

    @pl.when(grp == N_ATT_GROUPS - 1)
    def _():
        def finish(pi, carry):
            r0 = pl.multiple_of(pi * ATT_PIECE, ATT_PIECE)
            rows = pl.ds(r0, ATT_PIECE)
            for j in range(Q_PER_GROUP):
                cols = slice(j * hd, (j + 1) * hd)
                o_ref[0, rows, cols] = (acc_ref[j, rows, :] / l_ref[j, rows, :]).astype(o_ref.dtype)
            return carry

        lax.fori_loop(0, seq // ATT_PIECE, finish, 0)


def dilated_attention(q, kv, q_norm, k_norm):
    b, s, _ = q.shape
    gw = GROUP_WIDTH
    return pl.pallas_call(
        _attention_kernel,
        grid=(b, N_ATT_GROUPS),
        in_specs=[pl.BlockSpec((1, s, gw), lambda bi, gi: (bi, 0, gi)),
                  pl.BlockSpec((1, s, gw), lambda bi, gi: (bi, 0, gi)),
                  pl.BlockSpec((1, 1, HEAD_DIM), lambda bi, gi: (gi, 0, 0)),
                  pl.BlockSpec((1, 1, HEAD_DIM), lambda bi, gi: (gi, 0, 0))],
        out_specs=pl.BlockSpec((1, s, gw), lambda bi, gi: (bi, 0, 0)),
        out_shape=jax.ShapeDtypeStruct((b, s, gw), BF16),
        scratch_shapes=[pltpu.VMEM((Q_PER_GROUP, s, HEAD_DIM), F32),
                        pltpu.VMEM((KV_PER_GROUP, s, HEAD_DIM), F32),
                        pltpu.VMEM((KV_PER_GROUP, s, HEAD_DIM), F32),
                        pltpu.VMEM((Q_PER_GROUP, s, HEAD_DIM), F32),
                        pltpu.VMEM((Q_PER_GROUP, s, HEAD_DIM), F32),
                        pltpu.VMEM((Q_PER_GROUP, s, HEAD_DIM), F32)],
        compiler_params=pltpu.CompilerParams(dimension_semantics=("parallel", "arbitrary"),
                                             vmem_limit_bytes=VMEM_LIMIT),
        name="dilated_attention",
    )(q, kv, q_norm.reshape(N_ATT_GROUPS, 1, HEAD_DIM), k_norm.reshape(N_ATT_GROUPS, 1, HEAD_DIM))


ROUTER_TM = 512
SEG_ALIGN = 8
SORT_ROWS = 2 * ROUTER_TM + 256
assert SORT_ROWS >= 2 * ROUTER_TM + MOE_EXPERTS * (SEG_ALIGN - 1) and SORT_ROWS % LANES == 0
META_W0, META_W1, META_P0, META_P1 = 0, 1, 2, 3
TAB_CNT, TAB_OFF, TAB_SEG = 0, 1, 2


def _router_kernel(h_ref, g_ref, w_ref, b_ref, xn_ref, meta_ref, post_ref, tab_ref, cnt_ref, carry_ref):
    tm = h_ref.shape[0]

    @pl.when(pl.program_id(0) == 0)
    def _():
        carry_ref[...] = jnp.zeros_like(carry_ref)

    xn = _rms(h_ref[...], g_ref[...])
    xn_ref[...] = xn.astype(BF16)
    xh, xl = _split2(xn)
    wh, wl = _split2(w_ref[...])
    d = lambda a, bb: jnp.dot(a, bb, preferred_element_type=F32)
    logits = d(xh, wh) + d(xh, wl) + d(xl, wh) + b_ref[...]

    lane = lax.broadcasted_iota(jnp.int32, (tm, ROUTER_LANES), 1)
    big = jnp.int32(ROUTER_LANES)
    first_where = lambda cond: jnp.min(jnp.where(cond, lane, big), axis=-1, keepdims=True)

    gl = jnp.where(lane < MOE_GROUPS, logits, -jnp.inf)
    ge = jnp.exp(gl - jnp.max(gl, axis=-1, keepdims=True))
    gp = ge / jnp.sum(ge, axis=-1, keepdims=True)
    g_w = jnp.max(gp, axis=-1, keepdims=True)
    g_idx = first_where(gp == g_w)

    lo = EXP_LANE0 + g_idx * MOE_EPG
    in_group = (lane >= lo) & (lane < lo + MOE_EPG)
    el = jnp.where(in_group, logits, -jnp.inf)
    ee = jnp.exp(el - jnp.max(el, axis=-1, keepdims=True))
    ep = ee / jnp.sum(ee, axis=-1, keepdims=True)
    p0 = jnp.max(jnp.where(in_group, ep, -1.0), axis=-1, keepdims=True)
    i0 = first_where(in_group & (ep == p0))
    rest = in_group & (lane != i0)
    p1 = jnp.max(jnp.where(rest, ep, -1.0), axis=-1, keepdims=True)
    i1 = first_where(rest & (ep == p1))
    w0 = g_w * p0 / (p0 + p1)
    w1 = g_w * p1 / (p0 + p1)

    oh0 = jnp.where(lane == i0, 1.0, 0.0)
    oh1 = jnp.where(lane == i1, 1.0, 0.0)
    both = oh0 + oh1
    ti = lax.broadcasted_iota(jnp.int32, (tm, tm), 0)
    tj = lax.broadcasted_iota(jnp.int32, (tm, tm), 1)
    before = jnp.where(tj < ti, 1.0, 0.0).astype(BF16)
    within = jnp.dot(before, both.astype(BF16), preferred_element_type=F32)
    cnt = jnp.sum(both, axis=0, keepdims=True)
    cnt_pad = jnp.floor((cnt + (SEG_ALIGN - 1)) * (1.0 / SEG_ALIGN)) * SEG_ALIGN
    li = lax.broadcasted_iota(jnp.int32, (ROUTER_LANES, ROUTER_LANES), 0)
    lj = lax.broadcasted_iota(jnp.int32, (ROUTER_LANES, ROUTER_LANES), 1)
    earlier = jnp.where(li < lj, 1.0, 0.0).astype(BF16)
    tile_off = _dot_exact01(jnp.broadcast_to(cnt_pad, (8, ROUTER_LANES)), earlier)[0:1, :]
    row = tile_off + within
    pos0 = jnp.sum(row * oh0, axis=-1, keepdims=True)
    pos1 = jnp.sum(row * oh1, axis=-1, keepdims=True)
    seg_off = carry_ref[...]
    total = seg_off + cnt_pad
    carry_ref[...] = total
    cnt_ref[...] = jnp.broadcast_to(total, cnt_ref.shape)

    sub = lax.broadcasted_iota(jnp.int32, (8, ROUTER_LANES), 0)
    tab_ref[...] = jnp.where(sub == TAB_CNT, cnt_pad, jnp.where(sub == TAB_OFF, tile_off,
                                                                jnp.where(sub == TAB_SEG, seg_off, 0.0)))
    meta = jnp.zeros((tm, ROUTER_LANES), F32)
    for idx, val in ((META_W0, w0), (META_W1, w1), (META_P0, pos0), (META_P1, pos1)):
        meta = jnp.where(lane == idx, val, meta)
    meta_ref[...] = meta
    post_ref[0] = meta.T[0:8, :]


def moe_router(h, g, w_router, b_router):
    n, k = h.shape
    tm = ROUTER_TM
    nt = n // tm
    return pl.pallas_call(
        _router_kernel,
        grid=(nt,),
        in_specs=[pl.BlockSpec((tm, k), lambda i: (i, 0)),
                  pl.BlockSpec((1, k), lambda i: (0, 0)),
                  pl.BlockSpec((k, ROUTER_LANES), lambda i: (0, 0)),
                  pl.BlockSpec((1, ROUTER_LANES), lambda i: (0, 0))],
        out_specs=[pl.BlockSpec((tm, k), lambda i: (i, 0)),
                   pl.BlockSpec((tm, ROUTER_LANES), lambda i: (i, 0)),
                   pl.BlockSpec((1, 8, tm), lambda i: (i, 0, 0)),
                   pl.BlockSpec((8, ROUTER_LANES), lambda i: (i, 0)),
                   pl.BlockSpec((8, ROUTER_LANES), lambda i: (0, 0))],
        out_shape=[jax.ShapeDtypeStruct((n, k), BF16),
                   jax.ShapeDtypeStruct((n, ROUTER_LANES), F32),
                   jax.ShapeDtypeStruct((nt, 8, tm), F32),
                   jax.ShapeDtypeStruct((nt * 8, ROUTER_LANES), F32),
                   jax.ShapeDtypeStruct((8, ROUTER_LANES), F32)],
        scratch_shapes=[pltpu.VMEM((1, ROUTER_LANES), F32)],
        compiler_params=pltpu.CompilerParams(dimension_semantics=("arbitrary",),
                                             vmem_limit_bytes=VMEM_LIMIT),
        name="moe_router",
    )(h, g.reshape(1, k), w_router, b_router)


def _pack_halves(x):
    k = x.shape[1] // 2
    lo = pltpu.bitcast(x[:, :k].astype(BF16).astype(F32), jnp.uint32)
    hi = pltpu.bitcast(x[:, k:].astype(BF16).astype(F32), jnp.uint32)
    return (hi & jnp.uint32(0xFFFF0000)) | (lo >> 16)


def _unpack_halves(w):
    lo = pltpu.bitcast(w << 16, F32)
    hi = pltpu.bitcast(w & jnp.uint32(0xFFFF0000), F32)
    return lo.astype(BF16), hi.astype(BF16)


SEG_PIECE = 2 * SEG_ALIGN


def _segment_copies(rows_of, src_ref, dst_ref, sem, wait):
    def piece(s, d, rows):
        cp = pltpu.make_async_copy(src_ref.at[pl.ds(pl.multiple_of(s, SEG_ALIGN), rows), :],
                                   dst_ref.at[pl.ds(pl.multiple_of(d, SEG_ALIGN), rows), :], sem)
        cp.wait() if wait else cp.start()

    def per_expert(e, carry):
        cnt, s0, d0 = rows_of(e)
        n_full = cnt // SEG_PIECE

        def full_piece(j, c2):
            piece(s0 + j * SEG_PIECE, d0 + j * SEG_PIECE, SEG_PIECE)
            return c2

        lax.fori_loop(0, n_full, full_piece, 0)

        @pl.when(cnt % SEG_PIECE != 0)
        def _():
            piece(s0 + n_full * SEG_PIECE, d0 + n_full * SEG_PIECE, SEG_ALIGN)

        return carry

    lax.fori_loop(0, MOE_EXPERTS, per_expert, 0)


def _tile_segments(cnt_ref, tile, src_off_ref, dst_off_ref):
    def rows_of(e):
        k = tile * MOE_EXPERTS + e
        return cnt_ref[k], src_off_ref[k], dst_off_ref[k]
    return rows_of


def _dispatch_kernel(cnt_ref, off_ref, seg_ref, tail_cnt_ref, tail_dst_ref, nvalid_ref, post_ref, x_ref, out_hbm,
                     xs_ref, zero_ref, sem):
    tile = pl.program_id(0)
    tm = x_ref.shape[0]
    bm = zero_ref.shape[0]
    n_blocks = out_hbm.shape[0] // bm

    @pl.when(tile == 0)
    def _():
        zero_ref[...] = jnp.zeros_like(zero_ref)
        tails = lambda e: (tail_cnt_ref[e], 0, tail_dst_ref[e])

        def unused_block(wait):
            def body(blk, carry):
                cp = pltpu.make_async_copy(zero_ref, out_hbm.at[pl.ds(pl.multiple_of(blk * bm, bm), bm), :], sem)
                cp.wait() if wait else cp.start()
                return carry
            return body

        _segment_copies(tails, zero_ref, out_hbm, sem, wait=False)
        lax.fori_loop(nvalid_ref[0], n_blocks, unused_block(False), 0)
        _segment_copies(tails, zero_ref, out_hbm, sem, wait=True)
        lax.fori_loop(nvalid_ref[0], n_blocks, unused_block(True), 0)

    post = post_ref[0]
    p0 = post[META_P0:META_P0 + 1, :].astype(jnp.int32)
    p1 = post[META_P1:META_P1 + 1, :].astype(jnp.int32)
    r = lax.broadcasted_iota(jnp.int32, (SORT_ROWS, tm), 0)
    sel = jnp.where(r == p0, 1.0, jnp.where(r == p1, 1.0, 0.0)).astype(BF16)
    xs_ref[...] = _pack_halves(jnp.dot(sel, x_ref[...], preferred_element_type=F32))
    segments = _tile_segments(cnt_ref, tile, off_ref, seg_ref)
    _segment_copies(segments, xs_ref, out_hbm, sem, wait=False)
    _segment_copies(segments, xs_ref, out_hbm, sem, wait=True)


def moe_dispatch(tab_cnt, tab_off, tab_seg, tail_cnt, tail_dst, n_valid, post, xn, rows):
    n, d = xn.shape
    tm = ROUTER_TM
    grid_spec = pltpu.PrefetchScalarGridSpec(
        num_scalar_prefetch=6,
        grid=(n // tm,),
        in_specs=[pl.BlockSpec((1, 8, tm), lambda i, *_: (i, 0, 0)),
                  pl.BlockSpec((tm, d), lambda i, *_: (i, 0))],
        out_specs=pl.BlockSpec(memory_space=pl.ANY),
        scratch_shapes=[pltpu.VMEM((SORT_ROWS, d // 2), jnp.uint32), pltpu.VMEM((MOE_BM, d // 2), jnp.uint32),
                        pltpu.SemaphoreType.DMA(())],
    )
    return pl.pallas_call(
        _dispatch_kernel,
        grid_spec=grid_spec,
        out_shape=jax.ShapeDtypeStruct((rows, d // 2), jnp.uint32),
        compiler_params=pltpu.CompilerParams(dimension_semantics=("arbitrary",),
                                             vmem_limit_bytes=VMEM_LIMIT),
        name="moe_dispatch",
    )(tab_cnt, tab_off, tab_seg, tail_cnt, tail_dst, n_valid, post, xn)


def _expert_kernel(be_ref, nvalid_ref, x_ref, w1_ref, w3_ref, w2_ref, y_ref, w1b_ref, w3b_ref, w2b_ref):
    i = pl.program_id(0)

    @pl.when(i < nvalid_ref[0])
    def _():
        @pl.when(jnp.logical_or(i == 0, be_ref[i] != be_ref[jnp.maximum(i - 1, 0)]))
        def _():
            w1b_ref[...] = w1_ref[0, 0].astype(BF16)
            w3b_ref[...] = w3_ref[0, 0].astype(BF16)
            w2b_ref[...] = w2_ref[0, 0].astype(BF16)

        xlo, xhi = _unpack_halves(x_ref[...])
        half = xlo.shape[1]
        up = lambda w_ref: (jnp.dot(xlo, w_ref[:half, :], preferred_element_type=F32)
                            + jnp.dot(xhi, w_ref[half:, :], preferred_element_type=F32))
        hdn = (_silu(up(w1b_ref)) * up(w3b_ref)).astype(BF16)
        y_ref[...] = _pack_halves(jnp.dot(hdn, w2b_ref[...], preferred_element_type=F32))

    @pl.when(i >= nvalid_ref[0])
    def _():
        y_ref[...] = jnp.zeros_like(y_ref)


def moe_experts(x_sorted, block_e, n_valid, w1, w3, w2, layer):
    rows, half = x_sorted.shape
    n_blocks = block_e.shape[0]
    bm = MOE_BM
    d, hid = w1.shape[2], w1.shape[3]
    grid_spec = pltpu.PrefetchScalarGridSpec(
        num_scalar_prefetch=2,
        grid=(n_blocks,),
        in_specs=[pl.BlockSpec((bm, half), lambda i, be, nv: (jnp.minimum(i, nv[0] - 1), 0)),
                  pl.BlockSpec((1, 1, d, hid), lambda i, be, nv: (layer, be[i], 0, 0)),
                  pl.BlockSpec((1, 1, d, hid), lambda i, be, nv: (layer, be[i], 0, 0)),
                  pl.BlockSpec((1, 1, hid, d), lambda i, be, nv: (layer, be[i], 0, 0))],
        out_specs=pl.BlockSpec((bm, half), lambda i, be, nv: (i, 0)),
        scratch_shapes=[pltpu.VMEM((d, hid), BF16), pltpu.VMEM((d, hid), BF16), pltpu.VMEM((hid, d), BF16)],
    )
    return pl.pallas_call(
        _expert_kernel,
        grid_spec=grid_spec,
        out_shape=jax.ShapeDtypeStruct((rows, half), jnp.uint32),
        compiler_params=pltpu.CompilerParams(dimension_semantics=("arbitrary",),
                                             vmem_limit_bytes=VMEM_LIMIT),
        name="moe_experts",
    )(block_e, n_valid, x_sorted, w1, w3, w2)


def _combine_kernel(cnt_ref, off_ref, seg_ref, y_hbm, h_ref, meta_ref, o_ref, ys_ref, sem):
    tile = pl.program_id(0)
    tm, d = h_ref.shape

    @pl.when(tile == 0)
    def _():
        ys_ref[...] = jnp.zeros_like(ys_ref)

    segments = _tile_segments(cnt_ref, tile, seg_ref, off_ref)
    _segment_copies(segments, y_hbm, ys_ref, sem, wait=False)
    _segment_copies(segments, y_hbm, ys_ref, sem, wait=True)
    meta = meta_ref[...]
    w0 = meta[:, META_W0:META_W0 + 1]
    w1 = meta[:, META_W1:META_W1 + 1]
    p0 = meta[:, META_P0:META_P0 + 1].astype(jnp.int32)
    p1 = meta[:, META_P1:META_P1 + 1].astype(jnp.int32)
    r = lax.broadcasted_iota(jnp.int32, (tm, SORT_ROWS), 1)
    wmat = jnp.where(r == p0, w0, jnp.where(r == p1, w1, 0.0)).astype(BF16)
    ylo, yhi = _unpack_halves(ys_ref[...])
    mix = lambda y: jnp.dot(wmat, y, preferred_element_type=F32)
    half = d // 2
    o_ref[:, :half] = h_ref[:, :half] + mix(ylo)
    o_ref[:, half:] = h_ref[:, half:] + mix(yhi)


def moe_combine(tab_cnt, tab_off, tab_seg, y, h, meta):
    n, d = h.shape
    tm = ROUTER_TM
    grid_spec = pltpu.PrefetchScalarGridSpec(
        num_scalar_prefetch=3,
        grid=(n // tm,),
        in_specs=[pl.BlockSpec(memory_space=pl.ANY),
                  pl.BlockSpec((tm, d), lambda i, *_: (i, 0)),
                  pl.BlockSpec((tm, ROUTER_LANES), lambda i, *_: (i, 0))],
        out_specs=pl.BlockSpec((tm, d), lambda i, *_: (i, 0)),
        scratch_shapes=[pltpu.VMEM((SORT_ROWS, d // 2), jnp.uint32), pltpu.SemaphoreType.DMA(())],
    )
    return pl.pallas_call(
        _combine_kernel,
        grid_spec=grid_spec,
        out_shape=jax.ShapeDtypeStruct((n, d), F32),
        compiler_params=pltpu.CompilerParams(dimension_semantics=("arbitrary",),
                                             vmem_limit_bytes=VMEM_LIMIT),
        name="moe_combine",
    )(tab_cnt, tab_off, tab_seg, y, h, meta)


def hierarchical_moe(h, ffn_norm, w_rg, b_rg, w_re, b_re, w1, w3, w2, layer):
    n, d = h.shape
    pad = ROUTER_LANES - MOE_GROUPS - MOE_EXPERTS
    w_router = jnp.concatenate([w_rg, w_re, jnp.zeros((d, pad), F32)], axis=1)
    b_router = jnp.concatenate([b_rg, b_re, jnp.zeros((pad,), F32)]).reshape(1, ROUTER_LANES)
    xn, meta, post, tabs, cnt = moe_router(h, ffn_norm, w_router, b_router)

    bm = MOE_BM
    nt = n // ROUTER_TM
    lanes = slice(EXP_LANE0, EXP_LANE0 + MOE_EXPERTS)
    totals = cnt[0, lanes].astype(jnp.int32)
    region = (totals + bm - 1) // bm * bm
    region_end = jnp.cumsum(region)
    region_start = region_end - region
    n_blocks = -(-(2 * n + nt * MOE_EXPERTS * (SEG_ALIGN - 1)) // bm) + MOE_EXPERTS
    block_row0 = jnp.arange(n_blocks, dtype=jnp.int32) * bm
    block_e = jnp.minimum(jnp.sum((block_row0[:, None] >= region_end[None, :]).astype(jnp.int32), axis=1),
                          MOE_EXPERTS - 1).astype(jnp.int32)
    n_valid = (region_end[-1:] // bm).astype(jnp.int32)
    tabs = tabs.reshape(nt, 8, ROUTER_LANES)[:, :, lanes].astype(jnp.int32)
    tab_cnt = tabs[:, TAB_CNT].reshape(-1)
    tab_off = tabs[:, TAB_OFF].reshape(-1)
    tab_seg = (tabs[:, TAB_SEG] + region_start[None, :]).reshape(-1)

    x_sorted = moe_dispatch(tab_cnt, tab_off, tab_seg, region - totals, region_start + totals, n_valid, post, xn,
                            n_blocks * bm)
    y = moe_experts(x_sorted, block_e, n_valid, w1, w3, w2, layer)
    return moe_combine(tab_cnt, tab_off, tab_seg, y, h, meta)


def _ple_kernel(h_ref, p_ref, g_ref, wg_ref, wp_ref, o_ref):
    h = h_ref[...]
    gate = _sigmoid(jnp.dot(_rms(h, g_ref[...]).astype(BF16), wg_ref[...], preferred_element_type=F32))
    proj = jnp.dot(p_ref[...].astype(BF16), wp_ref[...], preferred_element_type=F32)
    o_ref[...] = h + gate * proj


def per_layer_embedding(h, p, layer, g, w_gate, w_proj):
    n, d = h.shape
    return _row_tiled_call(_ple_kernel, "per_layer_embedding", n, [h, (p, layer)],
                           [g.reshape(1, d), w_gate, w_proj], [d], [F32])[0]


def kernel(x, p, a_norm, a_w_in, a_conv, a_A_log, a_dt_bias, a_o_norm, a_w_out, kv_norm, w_kv, k_norm, b_norm, b_w_q, b_q_norm, b_w_out, ffn_norm, w_router_group, b_router_group, w_router_expert, b_router_expert, w1, w3, w2, ple_norm, w_ple_gate, w_ple_proj):
    b, s, d = x.shape
    n = b * s
    depth = p.shape[0]
    n_a = a_norm.shape[0]
    h = x.reshape(n, d)
    p_rows = p.reshape(depth, n, -1)
    kv = None
    for i in range(depth):
        if i < n_a:
            w_in = a_w_in[i]
            proj, gates_t = dn_inproj(h, a_norm[i], w_in[:, :DN_MAIN].astype(BF16), w_in[:, DN_MAIN:].T)
            o = deltanet(proj.reshape(b, s, DN_MAIN), gates_t.reshape(2 * DN_HEADS, b, s // DN_CHUNK, DN_CHUNK),
                         a_conv[i], a_A_log[i], a_dt_bias[i], a_o_norm[i])
            h = matmul_residual(o.reshape(n, DN_V), a_w_out[i].astype(BF16), h)
        else:
            bl = i - n_a
            q = norm_matmul(h, b_norm[bl], b_w_q[bl].astype(BF16), BF16)
            o = dilated_attention(q.reshape(b, s, -1), kv.reshape(b, s, -1), b_q_norm[bl], k_norm)
            h = matmul_residual(o.reshape(n, GROUP_WIDTH), b_w_out[bl].astype(BF16), h)
        h = hierarchical_moe(h, ffn_norm[i], w_router_group[i], b_router_group[i], w_router_expert[i],
                             b_router_expert[i], w1, w3, w2, i)
        h = per_layer_embedding(h, p_rows, i, ple_norm[i], w_ple_gate[i].astype(BF16), w_ple_proj[i].astype(BF16))
        if i == n_a - 1:
            kv = norm_matmul(h, kv_norm, w_kv.astype(BF16), BF16)
    return h.reshape(b, s, d)
```

```python
import numpy as np
import jax
import jax.numpy as jnp
from jax import lax
from jax.experimental import pallas as pl
from jax.experimental.pallas import tpu as pltpu

F32 = jnp.float32
BF16 = jnp.bfloat16

NORM_EPS = 1e-6

DN_HEADS = 8
DN_DK = 128
DN_DV = 128
DN_CONV = 4
DN_CHUNK = 128
DN_SQUARINGS = DN_CHUNK.bit_length() - 2
DN_GROUP = 16
assert DN_CHUNK == DN_DK == DN_DV
DN_QK = DN_HEADS * DN_DK
DN_V = DN_HEADS * DN_DV
DN_MAIN = 2 * DN_QK + 2 * DN_V

DIL_CONFIGS = ((128, 1), (512, 4), (2048, 16))
N_ATT_GROUPS = len(DIL_CONFIGS)
HEAD_DIM = 128
Q_PER_GROUP = 4
KV_PER_GROUP = 2
Q_REP = Q_PER_GROUP // KV_PER_GROUP
ATT_BLOCK = 128
ALIBI_MAX = 8.0
GROUP_WIDTH = Q_PER_GROUP * HEAD_DIM

MOE_GROUPS = 4
MOE_EPG = 8
MOE_EXPERTS = MOE_GROUPS * MOE_EPG
MOE_HIDDEN = 512
MOE_BM = 256
ROUTER_LANES = 128
EXP_LANE0 = MOE_GROUPS

LANES = 128
VMEM_LIMIT = 48 * 1024 * 1024
VMEM_LIMIT_DELTANET = 56 * 1024 * 1024


def _alibi_slopes():
    n = N_ATT_GROUPS * Q_PER_GROUP
    s = 2.0 ** (-ALIBI_MAX * np.arange(1, n + 1) / n)
    return s.reshape(N_ATT_GROUPS, KV_PER_GROUP, Q_REP)


def _rms(x, g):
    ms = jnp.mean(x * x, axis=-1, keepdims=True)
    return x * lax.rsqrt(ms + NORM_EPS) * g


def _dot(a, b):
    return jnp.dot(a.astype(BF16), b.astype(BF16), preferred_element_type=F32)


def _dot_nt(a, b):
    return lax.dot_general(a.astype(BF16), b.astype(BF16), (((1,), (1,)), ((), ())),
                           preferred_element_type=F32)


def _dot_tn(a, b):
    return lax.dot_general(a.astype(BF16), b.astype(BF16), (((0,), (0,)), ((), ())),
                           preferred_element_type=F32)


def _split2(x):
    hi = x.astype(BF16)
    lo = (x - hi.astype(F32)).astype(BF16)
    return hi, lo


def _split3(x):
    hi = x.astype(BF16)
    r = x - hi.astype(F32)
    mid = r.astype(BF16)
    lo = (r - mid.astype(F32)).astype(BF16)
    return hi, mid, lo


def _dot_exact01(x, sel):
    hi, mid, lo = _split3(x)
    d = lambda p: jnp.dot(p, sel, preferred_element_type=F32)
    return d(hi) + d(mid) + d(lo)


def _aligned(i, m):
    return i if isinstance(i, int) else pl.multiple_of(i, m)


def _sigmoid(x):
    return 1.0 / (1.0 + jnp.exp(-x))


def _silu(x):
    return x * _sigmoid(x)


ROW_TILE = 512


def _row_tiled_call(body, name, n, row_inputs, resident_inputs, out_widths, out_dtypes, extra_out_specs=(),
                    extra_out_shapes=()):
    tm = ROW_TILE
    row_spec = lambda width: pl.BlockSpec((tm, width), lambda i: (i, 0))
    whole = lambda a: pl.BlockSpec(a.shape, lambda i: (0,) * a.ndim)

    def in_row_spec(a):
        if isinstance(a, tuple):
            arr, layer = a
            return pl.BlockSpec((None, tm, arr.shape[2]), lambda i: (layer, i, 0))
        return row_spec(a.shape[1])

    row_specs = [in_row_spec(a) for a in row_inputs]
    row_inputs = [a[0] if isinstance(a, tuple) else a for a in row_inputs]
    return pl.pallas_call(
        body,
        grid=(n // tm,),
        in_specs=row_specs + [whole(a) for a in resident_inputs],
        out_specs=[row_spec(w) for w in out_widths] + list(extra_out_specs),
        out_shape=[jax.ShapeDtypeStruct((n, w), dt) for w, dt in zip(out_widths, out_dtypes)]
        + list(extra_out_shapes),
        compiler_params=pltpu.CompilerParams(dimension_semantics=("parallel",), vmem_limit_bytes=VMEM_LIMIT),
        name=name,
    )(*row_inputs, *resident_inputs)


def _nm_kernel(x_ref, g_ref, w_ref, o_ref):
    xn = _rms(x_ref[...], g_ref[...]).astype(BF16)
    o_ref[...] = jnp.dot(xn, w_ref[...], preferred_element_type=F32).astype(o_ref.dtype)


def norm_matmul(x, g, w, out_dtype):
    n, k = x.shape
    return _row_tiled_call(_nm_kernel, "norm_matmul", n, [x], [g.reshape(1, k), w], [w.shape[1]], [out_dtype])[0]


def _dn_inproj_kernel(x_ref, g_ref, w_ref, wgt_ref, o_ref, gt_ref):
    xn = _rms(x_ref[...], g_ref[...])
    xh, xl = _split2(xn)
    wh, wl = _split2(wgt_ref[...])
    gt_ref[...] = _dot_nt(wh, xh) + _dot_nt(wh, xl) + _dot_nt(wl, xh)
    o_ref[...] = jnp.dot(xh, w_ref[...], preferred_element_type=F32).astype(o_ref.dtype)


def dn_inproj(x, g, w_main, w_gates_t):
    n, k = x.shape
    ng = w_gates_t.shape[0]
    return _row_tiled_call(_dn_inproj_kernel, "dn_inproj", n, [x], [g.reshape(1, k), w_main, w_gates_t],
                           [w_main.shape[1]], [BF16],
                           extra_out_specs=[pl.BlockSpec((ng, ROW_TILE), lambda i: (0, i))],
                           extra_out_shapes=[jax.ShapeDtypeStruct((ng, n), F32)])


def _mm_res_kernel(a_ref, r_ref, w_ref, o_ref):
    o_ref[...] = r_ref[...] + jnp.dot(a_ref[...], w_ref[...], preferred_element_type=F32)


def matmul_residual(a, w, res):
    return _row_tiled_call(_mm_res_kernel, "matmul_residual", a.shape[0], [a, res], [w], [w.shape[1]], [F32])[0]


DN_PIECE = 256
DN_HALO = 8
DN_HB = 4
assert DN_HEADS % DN_HB == 0


def _deltanet_kernel(alog_ref, dtb_ref, q_ref, k_ref, v_ref, z_ref, cq_ref, ck_ref, cv_ref,
                     bpre_ref, apre_ref, onorm_ref, o_ref,
                     xf_ref, qs_ref, ks_ref, vs_ref, gcum_ref, betac_ref, gc_ref,
                     pm_ref, rq_ref, qq_ref, o0_ref, elast_ref):
    seq = q_ref.shape[1]
    c = DN_CHUNK
    n_chunks = seq // c
    assert 2 * n_chunks <= c
    head0 = pl.program_id(1) * DN_HB
    ki = lax.broadcasted_iota(jnp.int32, (c, c), 0)
    ji = lax.broadcasted_iota(jnp.int32, (c, c), 1)
    upper = jnp.where(ki <= ji, 1.0, 0.0).astype(BF16)
    causal = ki >= ji
    strict = ki > ji
    onorm = onorm_ref[...]

    xf_ref[0:DN_HALO, :] = jnp.zeros((DN_HALO, DN_DK), F32)

    def conv_silu(x_ref, w_ref, hb, finish, out_ref):
        cols = slice(hb * DN_DK, (hb + 1) * DN_DK)
        w = w_ref[:, cols]
        pieces = [slice(p * DN_PIECE, (p + 1) * DN_PIECE) for p in range(seq // DN_PIECE)]
        for rows in pieces:
            xf_ref[DN_HALO + rows.start:DN_HALO + rows.stop, :] = x_ref[0, rows, cols].astype(F32)
        for rows in pieces:
            acc = xf_ref[DN_HALO + rows.start:DN_HALO + rows.stop, :] * w[DN_CONV - 1:DN_CONV, :]
            for j in range(1, DN_CONV):
                acc = acc + xf_ref[DN_HALO + rows.start - j:DN_HALO + rows.stop - j, :] * w[DN_CONV - 1 - j:DN_CONV - j, :]
            out_ref[rows, :] = finish(_silu(acc))

    def l2n(scale):
        return lambda x: x * (lax.rsqrt(jnp.sum(x * x, axis=-1, keepdims=True) + NORM_EPS) * scale)

    def prologue(hb):
        conv_silu(q_ref, cq_ref, hb, l2n(DN_DK ** -0.5), qs_ref)
        conv_silu(k_ref, ck_ref, hb, l2n(1.0), ks_ref)
        conv_silu(v_ref, cv_ref, hb, lambda x: x, vs_ref)
        beta = _sigmoid(bpre_ref[hb, 0])
        a = apre_ref[hb, 0] + dtb_ref[head0 + hb]
        softplus = jnp.maximum(a, 0.0) + jnp.log(1.0 + jnp.exp(-jnp.abs(a)))
        g_log = -jnp.exp(jnp.full(a.shape, alog_ref[head0 + hb], F32)) * softplus
        gcum = _dot_exact01(g_log, upper)
        gcum_ref[...] = gcum
        t = jnp.concatenate([beta, gcum, jnp.zeros((c - 2 * n_chunks, c), F32)], axis=0).T
        for ci in range(n_chunks):
            betac_ref[ci * c:(ci + 1) * c, :] = jnp.broadcast_to(t[:, ci:ci + 1], (c, DN_DV))
            gc_ref[ci * c:(ci + 1) * c, :] = jnp.broadcast_to(t[:, n_chunks + ci:n_chunks + ci + 1], (c, DN_DV))

    def prepare(hb, cis):
        each = lambda f, *ls: [f(*xs) for xs in zip(*ls)]
        rows = [pl.ds(_aligned(ci * c, c), c) for ci in cis]
        qc = [qs_ref[r, :] for r in rows]
        kc = [ks_ref[r, :] for r in rows]
        vc = [vs_ref[r, :] for r in rows]
        beta_c = [betac_ref[r, :] for r in rows]
        g_c = [gc_ref[r, :] for r in rows]
        g_j = [jnp.broadcast_to(gcum_ref[pl.ds(ci, 1), :], (c, c)) for ci in cis]
        decay = each(lambda gi, gj: jnp.exp(jnp.where(causal, gi - gj, -jnp.inf)), g_c, g_j)
        kq = each(lambda k, q: _dot_nt(jnp.concatenate([k, q], axis=0), k), kc, qc)
        m = each(lambda b, x, d: jnp.where(strict, -(b * x[:c, :] * d), 0.0), beta_c, kq, decay)
        pw = each(lambda x: _dot(x, x), m)
        r = m
        for _ in range(DN_SQUARINGS - 1):
            xs = each(lambda p_, r_: _dot(p_, jnp.concatenate([p_, r_], axis=1)), pw, r)
            r = each(lambda r_, p_, x: r_ + p_ + x[:, c:], r, pw, xs)
            pw = [x[:, :c] for x in xs]
        xs = each(_dot, pw, r)
        r = each(lambda r_, p_, x: r_ + p_ + x, r, pw, xs)
        e_g = [jnp.exp(g) for g in g_c]
        rhs = each(lambda b, v, e, k: jnp.concatenate([b * v, b * e * k], axis=1), beta_c, vc, e_g, kc)
        sol = each(lambda rh, r_: rh + _dot(r_, rh), rhs, r)
        attn = each(lambda x, d: jnp.where(causal, x[c:, :] * d, 0.0), kq, decay)
        k_d = each(lambda k, g: k * jnp.exp(jnp.broadcast_to(g[c - 1:c, :], (c, DN_DV)) - g), kc, g_c)
        kt = each(_dot_tn, k_d, sol)
        at = each(_dot, attn, sol)
        for i, (ci, r_) in enumerate(zip(cis, rows)):
            qq_ref[hb, r_, :] = kt[i][:, :DN_DV]
            pm_ref[hb, r_, :] = kt[i][:, DN_DV:].astype(BF16)
            o0_ref[hb, r_, :] = at[i][:, :DN_DV]
            rq_ref[hb, r_, :] = (qc[i] * e_g[i] - at[i][:, DN_DV:]).astype(BF16)
            elast_ref[hb, pl.ds(_aligned(ci * 8, 8), 8), :] = jnp.exp(
                jnp.broadcast_to(g_c[i][c - 1:c, :], (8, DN_DV)))

    group = min(DN_GROUP, n_chunks)
    assert n_chunks % group == 0
    for hb in range(DN_HB):
        prologue(hb)
        if group == n_chunks:
            prepare(hb, list(range(n_chunks)))
        else:
            def prepare_group(gi, carry, hb=hb):
                prepare(hb, [gi * group + k for k in range(group)])
                return carry

            lax.fori_loop(0, n_chunks // group, prepare_group, 0)

    def chunk_step(ci, states):
        rows = pl.ds(pl.multiple_of(ci * c, c), c)
        xs = [_dot(jnp.concatenate([pm_ref[hb, rows, :], rq_ref[hb, rows, :]], axis=0), states[hb])
              for hb in range(DN_HB)]
        new_states = []
        for hb in range(DN_HB):
            cols = slice(hb * DN_DV, (hb + 1) * DN_DV)
            e_last = jnp.broadcast_to(elast_ref[hb, pl.ds(pl.multiple_of(ci * 8, 8), 1), :], (DN_DK, DN_DV))
            new_states.append(e_last * states[hb] - xs[hb][:c, :] + qq_ref[hb, rows, :])
            o = xs[hb][c:, :] + o0_ref[hb, rows, :]
            zc = z_ref[0, rows, cols].astype(F32)
            o_ref[0, rows, cols] = (_rms(o, onorm) * _silu(zc)).astype(o_ref.dtype)
        return tuple(new_states)

    lax.fori_loop(0, n_chunks, chunk_step, tuple(jnp.zeros((DN_DK, DN_DV), F32) for _ in range(DN_HB)))


def deltanet(proj, gates_t, conv_w, a_log, dt_bias, o_norm):
    b, s, _ = proj.shape
    ng = DN_HEADS // DN_HB
    nc = s // DN_CHUNK
    wide = DN_HB * DN_DK
    col = lambda off: pl.BlockSpec((1, s, wide), lambda bi, hi: (bi, 0, off + hi))
    cw = lambda off: pl.BlockSpec((DN_CONV, wide), lambda bi, hi: (0, off + hi))
    gate = lambda off: pl.BlockSpec((DN_HB, 1, nc, DN_CHUNK), lambda bi, hi: (off + hi, bi, 0, 0))
    smem = pl.BlockSpec(memory_space=pltpu.SMEM)
    per_head = lambda dt: pltpu.VMEM((DN_HB, s, DN_DV), dt)
    return pl.pallas_call(
        _deltanet_kernel,
        grid=(b, ng),
        in_specs=[smem, smem, col(0), col(ng), col(2 * ng), col(3 * ng), cw(0), cw(ng), cw(2 * ng),
                  gate(0), gate(ng), pl.BlockSpec((1, DN_DV), lambda bi, hi: (0, 0))],
        out_specs=pl.BlockSpec((1, s, wide), lambda bi, hi: (bi, 0, hi)),
        out_shape=jax.ShapeDtypeStruct((b, s, DN_V), BF16),
        scratch_shapes=[pltpu.VMEM((DN_HALO + s, DN_DK), F32),
                        pltpu.VMEM((s, DN_DK), F32), pltpu.VMEM((s, DN_DK), F32), pltpu.VMEM((s, DN_DV), F32),
                        pltpu.VMEM((nc, DN_CHUNK), F32), pltpu.VMEM((s, DN_DV), F32), pltpu.VMEM((s, DN_DV), F32),
                        per_head(BF16), per_head(BF16), per_head(F32), per_head(F32),
                        pltpu.VMEM((DN_HB, nc * 8, DN_DV), F32)],
        compiler_params=pltpu.CompilerParams(dimension_semantics=("parallel", "parallel"),
                                             vmem_limit_bytes=VMEM_LIMIT_DELTANET),
        name="deltanet",
    )(a_log, dt_bias, proj, proj, proj, proj, conv_w, conv_w, conv_w, gates_t, gates_t, o_norm.reshape(1, DN_DV))


ATT_PIECE = 256
ATT_M_INIT = -1e30


def _attention_kernel(q_ref, kv_ref, qn_ref, kn_ref, o_ref, qf_ref, kf_ref, vf_ref, acc_ref, m_ref, l_ref):
    seq = q_ref.shape[1]
    grp = pl.program_id(1)
    hd = HEAD_DIM
    blk = ATT_BLOCK
    slopes = _alibi_slopes()

    qg = qn_ref[0] * (hd ** -0.5)
    kg = kn_ref[0]

    def prep(pi, carry):
        r0 = pl.multiple_of(pi * ATT_PIECE, ATT_PIECE)
        rows = pl.ds(r0, ATT_PIECE)
        for j in range(Q_PER_GROUP):
            cols = slice(j * hd, (j + 1) * hd)
            qf_ref[j, rows, :] = _rms(q_ref[0, rows, cols].astype(F32), 1.0) * qg
        for j in range(KV_PER_GROUP):
            cols = slice(j * hd, (j + 1) * hd)
            kf_ref[j, rows, :] = _rms(kv_ref[0, rows, cols].astype(F32), 1.0) * kg
            vcols = slice((KV_PER_GROUP + j) * hd, (KV_PER_GROUP + j + 1) * hd)
            vf_ref[j, rows, :] = kv_ref[0, rows, vcols].astype(F32)
        return carry

    lax.fori_loop(0, seq // ATT_PIECE, prep, 0)

    @pl.when(grp == 0)
    def _():
        def init(pi, carry):
            r0 = pl.multiple_of(pi * ATT_PIECE, ATT_PIECE)
            rows = pl.ds(r0, ATT_PIECE)
            for j in range(Q_PER_GROUP):
                acc_ref[j, rows, :] = jnp.zeros((ATT_PIECE, hd), F32)
                l_ref[j, rows, :] = jnp.zeros((ATT_PIECE, hd), F32)
                m_ref[j, rows, :] = jnp.full((ATT_PIECE, hd), ATT_M_INIT, F32)
            return carry

        lax.fori_loop(0, seq // ATT_PIECE, init, 0)

    def rows_of(start, size, dil):
        return pl.ds(start, size) if dil == 1 else pl.ds(start, size, stride=dil)

    def attend(g, dil, q_start, k_start, nk):
        qrows = rows_of(q_start, blk, dil)
        krows = rows_of(k_start, nk, dil)
        qi = lax.broadcasted_iota(jnp.int32, (blk, nk), 0)
        kidx = lax.broadcasted_iota(jnp.int32, (blk, nk), 1)
        dist = (nk - blk) + qi - kidx
        valid = (dist >= 0) & (dist <= blk)
        distf = dist.astype(F32)
        kv_heads = range(KV_PER_GROUP)
        heads = [(kvh, rep) for kvh in kv_heads for rep in range(Q_REP)]
        vb = [vf_ref[kvh, krows, :].astype(BF16) for kvh in kv_heads]
        sc2 = [_dot_nt(jnp.concatenate([qf_ref[kvh * Q_REP + rep, qrows, :] for rep in range(Q_REP)], axis=0),
                       kf_ref[kvh, krows, :]) for kvh in kv_heads]
        sc = [jnp.where(valid, sc2[kvh][rep * blk:(rep + 1) * blk, :] - float(slopes[g, kvh, rep] * dil) * distf,
                        -jnp.inf) for kvh, rep in heads]
        m_old = [m_ref[kvh * Q_REP + rep, qrows, :] for kvh, rep in heads]
        m_new = [jnp.maximum(mo, jnp.max(s, axis=-1, keepdims=True)) for mo, s in zip(m_old, sc)]
        alpha = [jnp.exp(mo - mn) for mo, mn in zip(m_old, m_new)]
        p = [jnp.exp(s - mn[:, 0:1]) for s, mn in zip(sc, m_new)]
        pv2 = [jnp.dot(jnp.concatenate([p[kvh * Q_REP + rep].astype(BF16) for rep in range(Q_REP)], axis=0),
                       vb[kvh], preferred_element_type=F32) for kvh in kv_heads]
        for i, (kvh, rep) in enumerate(heads):
            j = kvh * Q_REP + rep
            l_ref[j, qrows, :] = alpha[i] * l_ref[j, qrows, :] + jnp.sum(p[i], axis=-1, keepdims=True)
            acc_ref[j, qrows, :] = alpha[i] * acc_ref[j, qrows, :] + pv2[kvh][rep * blk:(rep + 1) * blk, :]
            m_ref[j, qrows, :] = m_new[i]

    for g, (window, dil) in enumerate(DIL_CONFIGS):
        sub_len = seq // dil
        nblk = sub_len // blk

        @pl.when(grp == g)
        def _(g=g, dil=dil, nblk=nblk):
            def residue(res, carry):
                attend(g, dil, res, res, blk)
                if nblk > 1:
                    def later(n, c2):
                        attend(g, dil, res + n * blk * dil, res + (n - 1) * blk * dil, 2 * blk)
                        return c2
                    lax.fori_loop(1, nblk, later, 0)
                return carry

            lax.fori_loop(0, dil, residue, 0)

    @pl.when(grp == N_ATT_GROUPS - 1)
    def _():
        def finish(pi, carry):
            r0 = pl.multiple_of(pi * ATT_PIECE, ATT_PIECE)
            rows = pl.ds(r0, ATT_PIECE)
            for j in range(Q_PER_GROUP):
                cols = slice(j * hd, (j + 1) * hd)
                o_ref[0, rows, cols] = (acc_ref[j, rows, :] / l_ref[j, rows, :]).astype(o_ref.dtype)
            return carry

        lax.fori_loop(0, seq // ATT_PIECE, finish, 0)


def dilated_attention(q, kv, q_norm, k_norm):
    b, s, _ = q.shape
    gw = GROUP_WIDTH
    return pl.pallas_call(
        _attention_kernel,
        grid=(b, N_ATT_GROUPS),
        in_specs=[pl.BlockSpec((1, s, gw), lambda bi, gi: (bi, 0, gi)),
                  pl.BlockSpec((1, s, gw), lambda bi, gi: (bi, 0, gi)),
                  pl.BlockSpec((1, 1, HEAD_DIM), lambda bi, gi: (gi, 0, 0)),
                  pl.BlockSpec((1, 1, HEAD_DIM), lambda bi, gi: (gi, 0, 0))],
        out_specs=pl.BlockSpec((1, s, gw), lambda bi, gi: (bi, 0, 0)),
        out_shape=jax.ShapeDtypeStruct((b, s, gw), BF16),
        scratch_shapes=[pltpu.VMEM((Q_PER_GROUP, s, HEAD_DIM), F32),
                        pltpu.VMEM((KV_PER_GROUP, s, HEAD_DIM), F32),
                        pltpu.VMEM((KV_PER_GROUP, s, HEAD_DIM), F32),
                        pltpu.VMEM((Q_PER_GROUP, s, HEAD_DIM), F32),
                        pltpu.VMEM((Q_PER_GROUP, s, HEAD_DIM), F32),
                        pltpu.VMEM((Q_PER_GROUP, s, HEAD_DIM), F32)],
        compiler_params=pltpu.CompilerParams(dimension_semantics=("parallel", "arbitrary"),
                                             vmem_limit_bytes=VMEM_LIMIT),
        name="dilated_attention",
    )(q, kv, q_norm.reshape(N_ATT_GROUPS, 1, HEAD_DIM), k_norm.reshape(N_ATT_GROUPS, 1, HEAD_DIM))


ROUTER_TM = 512
SEG_ALIGN = 8
SORT_ROWS = 2 * ROUTER_TM + 256
assert SORT_ROWS >= 2 * ROUTER_TM + MOE_EXPERTS * (SEG_ALIGN - 1) and SORT_ROWS % LANES == 0
META_W0, META_W1, META_P0, META_P1 = 0, 1, 2, 3
TAB_CNT, TAB_OFF, TAB_SEG = 0, 1, 2


def _router_kernel(h_ref, g_ref, w_ref, b_ref, xn_ref, meta_ref, post_ref, tab_ref, cnt_ref, carry_ref):
    tm = h_ref.shape[0]

    @pl.when(pl.program_id(0) == 0)
    def _():
        carry_ref[...] = jnp.zeros_like(carry_ref)

    xn = _rms(h_ref[...], g_ref[...])
    xn_ref[...] = xn.astype(BF16)
    xh, xl = _split2(xn)
    wh, wl = _split2(w_ref[...])
    d = lambda a, bb: jnp.dot(a, bb, preferred_element_type=F32)
    logits = d(xh, wh) + d(xh, wl) + d(xl, wh) + b_ref[...]

    lane = lax.broadcasted_iota(jnp.int32, (tm, ROUTER_LANES), 1)
    big = jnp.int32(ROUTER_LANES)
    first_where = lambda cond: jnp.min(jnp.where(cond, lane, big), axis=-1, keepdims=True)

    gl = jnp.where(lane < MOE_GROUPS, logits, -jnp.inf)
    ge = jnp.exp(gl - jnp.max(gl, axis=-1, keepdims=True))
    gp = ge / jnp.sum(ge, axis=-1, keepdims=True)
    g_w = jnp.max(gp, axis=-1, keepdims=True)
    g_idx = first_where(gp == g_w)

    lo = EXP_LANE0 + g_idx * MOE_EPG
    in_group = (lane >= lo) & (lane < lo + MOE_EPG)
    el = jnp.where(in_group, logits, -jnp.inf)
    ee = jnp.exp(el - jnp.max(el, axis=-1, keepdims=True))
    ep = ee / jnp.sum(ee, axis=-1, keepdims=True)
    p0 = jnp.max(jnp.where(in_group, ep, -1.0), axis=-1, keepdims=True)
    i0 = first_where(in_group & (ep == p0))
    rest = in_group & (lane != i0)
    p1 = jnp.max(jnp.where(rest, ep, -1.0), axis=-1, keepdims=True)
    i1 = first_where(rest & (ep == p1))
    w0 = g_w * p0 / (p0 + p1)
    w1 = g_w * p1 / (p0 + p1)

    oh0 = jnp.where(lane == i0, 1.0, 0.0)
    oh1 = jnp.where(lane == i1, 1.0, 0.0)
    both = oh0 + oh1
    ti = lax.broadcasted_iota(jnp.int32, (tm, tm), 0)
    tj = lax.broadcasted_iota(jnp.int32, (tm, tm), 1)
    before = jnp.where(tj < ti, 1.0, 0.0).astype(BF16)
    within = jnp.dot(before, both.astype(BF16), preferred_element_type=F32)
    cnt = jnp.sum(both, axis=0, keepdims=True)
    cnt_pad = jnp.floor((cnt + (SEG_ALIGN - 1)) * (1.0 / SEG_ALIGN)) * SEG_ALIGN
    li = lax.broadcasted_iota(jnp.int32, (ROUTER_LANES, ROUTER_LANES), 0)
    lj = lax.broadcasted_iota(jnp.int32, (ROUTER_LANES, ROUTER_LANES), 1)
    earlier = jnp.where(li < lj, 1.0, 0.0).astype(BF16)
    tile_off = _dot_exact01(jnp.broadcast_to(cnt_pad, (8, ROUTER_LANES)), earlier)[0:1, :]
    row = tile_off + within
    pos0 = jnp.sum(row * oh0, axis=-1, keepdims=True)
    pos1 = jnp.sum(row * oh1, axis=-1, keepdims=True)
    seg_off = carry_ref[...]
    total = seg_off + cnt_pad
    carry_ref[...] = total
    cnt_ref[...] = jnp.broadcast_to(total, cnt_ref.shape)

    sub = lax.broadcasted_iota(jnp.int32, (8, ROUTER_LANES), 0)
    tab_ref[...] = jnp.where(sub == TAB_CNT, cnt_pad, jnp.where(sub == TAB_OFF, tile_off,
                                                                jnp.where(sub == TAB_SEG, seg_off, 0.0)))
    meta = jnp.zeros((tm, ROUTER_LANES), F32)
    for idx, val in ((META_W0, w0), (META_W1, w1), (META_P0, pos0), (META_P1, pos1)):
        meta = jnp.where(lane == idx, val, meta)
    meta_ref[...] = meta
    post_ref[0] = meta.T[0:8, :]


def moe_router(h, g, w_router, b_router):
    n, k = h.shape
    tm = ROUTER_TM
    nt = n // tm
    return pl.pallas_call(
        _router_kernel,
        grid=(nt,),
        in_specs=[pl.BlockSpec((tm, k), lambda i: (i, 0)),
                  pl.BlockSpec((1, k), lambda i: (0, 0)),
                  pl.BlockSpec((k, ROUTER_LANES), lambda i: (0, 0)),
                  pl.BlockSpec((1, ROUTER_LANES), lambda i: (0, 0))],
        out_specs=[pl.BlockSpec((tm, k), lambda i: (i, 0)),
                   pl.BlockSpec((tm, ROUTER_LANES), lambda i: (i, 0)),
                   pl.BlockSpec((1, 8, tm), lambda i: (i, 0, 0)),
                   pl.BlockSpec((8, ROUTER_LANES), lambda i: (i, 0)),
                   pl.BlockSpec((8, ROUTER_LANES), lambda i: (0, 0))],
        out_shape=[jax.ShapeDtypeStruct((n, k), BF16),
                   jax.ShapeDtypeStruct((n, ROUTER_LANES), F32),
                   jax.ShapeDtypeStruct((nt, 8, tm), F32),
                   jax.ShapeDtypeStruct((nt * 8, ROUTER_LANES), F32),
                   jax.ShapeDtypeStruct((8, ROUTER_LANES), F32)],
        scratch_shapes=[pltpu.VMEM((1, ROUTER_LANES), F32)],
        compiler_params=pltpu.CompilerParams(dimension_semantics=("arbitrary",),
                                             vmem_limit_bytes=VMEM_LIMIT),
        name="moe_router",
    )(h, g.reshape(1, k), w_router, b_router)


def _pack_halves(x):
    k = x.shape[1] // 2
    lo = pltpu.bitcast(x[:, :k].astype(BF16).astype(F32), jnp.uint32)
    hi = pltpu.bitcast(x[:, k:].astype(BF16).astype(F32), jnp.uint32)
    return (hi & jnp.uint32(0xFFFF0000)) | (lo >> 16)


def _unpack_halves(w):
    lo = pltpu.bitcast(w << 16, F32)
    hi = pltpu.bitcast(w & jnp.uint32(0xFFFF0000), F32)
    return lo.astype(BF16), hi.astype(BF16)


SEG_PIECE = 2 * SEG_ALIGN


def _segment_copies(rows_of, src_ref, dst_ref, sem, wait):
    def piece(s, d, rows):
        cp = pltpu.make_async_copy(src_ref.at[pl.ds(pl.multiple_of(s, SEG_ALIGN), rows), :],
                                   dst_ref.at[pl.ds(pl.multiple_of(d, SEG_ALIGN), rows), :], sem)
        cp.wait() if wait else cp.start()

    def per_expert(e, carry):
        cnt, s0, d0 = rows_of(e)
        n_full = cnt // SEG_PIECE

        def full_piece(j, c2):
            piece(s0 + j * SEG_PIECE, d0 + j * SEG_PIECE, SEG_PIECE)
            return c2

        lax.fori_loop(0, n_full, full_piece, 0)

        @pl.when(cnt % SEG_PIECE != 0)
        def _():
            piece(s0 + n_full * SEG_PIECE, d0 + n_full * SEG_PIECE, SEG_ALIGN)

        return carry

    lax.fori_loop(0, MOE_EXPERTS, per_expert, 0)


def _tile_segments(cnt_ref, tile, src_off_ref, dst_off_ref):
    def rows_of(e):
        k = tile * MOE_EXPERTS + e
        return cnt_ref[k], src_off_ref[k], dst_off_ref[k]
    return rows_of


def _dispatch_kernel(cnt_ref, off_ref, seg_ref, tail_cnt_ref, tail_dst_ref, nvalid_ref, post_ref, x_ref, out_hbm,
                     xs_ref, zero_ref, sems):
    tile = pl.program_id(0)
    tm = x_ref.shape[0]
    bm = zero_ref.shape[0]
    sem = sems.at[2]
    n_blocks = out_hbm.shape[0] // bm

    @pl.when(tile == 0)
    def _():
        zero_ref[...] = jnp.zeros_like(zero_ref)
        tails = lambda e: (tail_cnt_ref[e], 0, tail_dst_ref[e])

        def unused_block(wait):
            def body(blk, carry):
                cp = pltpu.make_async_copy(zero_ref, out_hbm.at[pl.ds(pl.multiple_of(blk * bm, bm), bm), :], sem)
                cp.wait() if wait else cp.start()
                return carry
            return body

        _segment_copies(tails, zero_ref, out_hbm, sem, wait=False)
        lax.fori_loop(nvalid_ref[0], n_blocks, unused_block(False), 0)
        _segment_copies(tails, zero_ref, out_hbm, sem, wait=True)
        lax.fori_loop(nvalid_ref[0], n_blocks, unused_block(True), 0)

    post = post_ref[0]
    p0 = post[META_P0:META_P0 + 1, :].astype(jnp.int32)
    p1 = post[META_P1:META_P1 + 1, :].astype(jnp.int32)
    r = lax.broadcasted_iota(jnp.int32, (SORT_ROWS, tm), 0)
    sel = jnp.where(r == p0, 1.0, jnp.where(r == p1, 1.0, 0.0)).astype(BF16)
    n_tiles = pl.num_programs(0)
    slot = tile % 2

    def copies(t, s, wait):
        _segment_copies(_tile_segments(cnt_ref, t, off_ref, seg_ref), xs_ref.at[s], out_hbm, sems.at[s], wait)

    @pl.when(tile >= 2)
    def _():
        copies(tile - 2, slot, True)

    xs_ref[slot] = _pack_halves(jnp.dot(sel, x_ref[...], preferred_element_type=F32))
    copies(tile, slot, False)

    @pl.when(tile == n_tiles - 1)
    def _():
        @pl.when(tile >= 1)
        def _():
            copies(tile - 1, 1 - slot, True)

        copies(tile, slot, True)


def moe_dispatch(tab_cnt, tab_off, tab_seg, tail_cnt, tail_dst, n_valid, post, xn, rows):
    n, d = xn.shape
    tm = ROUTER_TM
    grid_spec = pltpu.PrefetchScalarGridSpec(
        num_scalar_prefetch=6,
        grid=(n // tm,),
        in_specs=[pl.BlockSpec((1, 8, tm), lambda i, *_: (i, 0, 0)),
                  pl.BlockSpec((tm, d), lambda i, *_: (i, 0))],
        out_specs=pl.BlockSpec(memory_space=pl.ANY),
        scratch_shapes=[pltpu.VMEM((2, SORT_ROWS, d // 2), jnp.uint32), pltpu.VMEM((MOE_BM, d // 2), jnp.uint32),
                        pltpu.SemaphoreType.DMA((3,))],
    )
    return pl.pallas_call(
        _dispatch_kernel,
        grid_spec=grid_spec,
        out_shape=jax.ShapeDtypeStruct((rows, d // 2), jnp.uint32),
        compiler_params=pltpu.CompilerParams(dimension_semantics=("arbitrary",),
                                             vmem_limit_bytes=VMEM_LIMIT),
        name="moe_dispatch",
    )(tab_cnt, tab_off, tab_seg, tail_cnt, tail_dst, n_valid, post, xn)


def _expert_kernel(be_ref, nvalid_ref, x_ref, w1_ref, w3_ref, w2_ref, y_ref, w1b_ref, w3b_ref, w2b_ref):
    i = pl.program_id(0)

    @pl.when(i < nvalid_ref[0])
    def _():
        @pl.when(jnp.logical_or(i == 0, be_ref[i] != be_ref[jnp.maximum(i - 1, 0)]))
        def _():
            w1b_ref[...] = w1_ref[0, 0].astype(BF16)
            w3b_ref[...] = w3_ref[0, 0].astype(BF16)
            w2b_ref[...] = w2_ref[0, 0].astype(BF16)

        xlo, xhi = _unpack_halves(x_ref[...])
        half = xlo.shape[1]
        up = lambda w_ref: (jnp.dot(xlo, w_ref[:half, :], preferred_element_type=F32)
                            + jnp.dot(xhi, w_ref[half:, :], preferred_element_type=F32))
        hdn = (_silu(up(w1b_ref)) * up(w3b_ref)).astype(BF16)
        y_ref[...] = _pack_halves(jnp.dot(hdn, w2b_ref[...], preferred_element_type=F32))

    @pl.when(i >= nvalid_ref[0])
    def _():
        y_ref[...] = jnp.zeros_like(y_ref)


def moe_experts(x_sorted, block_e, n_valid, w1, w3, w2, layer):
    rows, half = x_sorted.shape
    n_blocks = block_e.shape[0]
    bm = MOE_BM
    d, hid = w1.shape[2], w1.shape[3]
    grid_spec = pltpu.PrefetchScalarGridSpec(
        num_scalar_prefetch=2,
        grid=(n_blocks,),
        in_specs=[pl.BlockSpec((bm, half), lambda i, be, nv: (jnp.minimum(i, nv[0] - 1), 0)),
                  pl.BlockSpec((1, 1, d, hid), lambda i, be, nv: (layer, be[i], 0, 0)),
                  pl.BlockSpec((1, 1, d, hid), lambda i, be, nv: (layer, be[i], 0, 0)),
                  pl.BlockSpec((1, 1, hid, d), lambda i, be, nv: (layer, be[i], 0, 0))],
        out_specs=pl.BlockSpec((bm, half), lambda i, be, nv: (i, 0)),
        scratch_shapes=[pltpu.VMEM((d, hid), BF16), pltpu.VMEM((d, hid), BF16), pltpu.VMEM((hid, d), BF16)],
    )
    return pl.pallas_call(
        _expert_kernel,
        grid_spec=grid_spec,
        out_shape=jax.ShapeDtypeStruct((rows, half), jnp.uint32),
        compiler_params=pltpu.CompilerParams(dimension_semantics=("arbitrary",),
                                             vmem_limit_bytes=VMEM_LIMIT),
        name="moe_experts",
    )(block_e, n_valid, x_sorted, w1, w3, w2)


def _combine_ple_kernel(cnt_ref, off_ref, seg_ref, y_hbm, h_ref, meta_ref, p_ref, g_ref, wg_ref, wp_ref, o_ref,
                        ys_ref, sems):
    tile = pl.program_id(0)
    n_tiles = pl.num_programs(0)
    tm, d = h_ref.shape
    slot = tile % 2

    def copies(t, s, wait):
        _segment_copies(_tile_segments(cnt_ref, t, seg_ref, off_ref), y_hbm, ys_ref.at[s], sems.at[s], wait)

    @pl.when(tile == 0)
    def _():
        ys_ref[...] = jnp.zeros_like(ys_ref)
        copies(0, 0, False)

    @pl.when(tile + 1 < n_tiles)
    def _():
        copies(tile + 1, 1 - slot, False)

    copies(tile, slot, True)
    meta = meta_ref[...]
    w0 = meta[:, META_W0:META_W0 + 1]
    w1 = meta[:, META_W1:META_W1 + 1]
    p0 = meta[:, META_P0:META_P0 + 1].astype(jnp.int32)
    p1 = meta[:, META_P1:META_P1 + 1].astype(jnp.int32)
    r = lax.broadcasted_iota(jnp.int32, (tm, SORT_ROWS), 1)
    wmat = jnp.where(r == p0, w0, jnp.where(r == p1, w1, 0.0)).astype(BF16)
    ylo, yhi = _unpack_halves(ys_ref[slot])
    mix = lambda y: jnp.dot(wmat, y, preferred_element_type=F32)
    half = d // 2
    h = jnp.concatenate([h_ref[:, :half] + mix(ylo), h_ref[:, half:] + mix(yhi)], axis=1)
    gate = _sigmoid(jnp.dot(_rms(h, g_ref[...]).astype(BF16), wg_ref[...], preferred_element_type=F32))
    proj = jnp.dot(p_ref[...].astype(BF16), wp_ref[...], preferred_element_type=F32)
    o_ref[...] = h + gate * proj


def moe_combine_ple(tab_cnt, tab_off, tab_seg, y, h, meta, p, layer, g, w_gate, w_proj):
    n, d = h.shape
    pd = p.shape[2]
    tm = ROUTER_TM
    grid_spec = pltpu.PrefetchScalarGridSpec(
        num_scalar_prefetch=3,
        grid=(n // tm,),
        in_specs=[pl.BlockSpec(memory_space=pl.ANY),
                  pl.BlockSpec((tm, d), lambda i, *_: (i, 0)),
                  pl.BlockSpec((tm, ROUTER_LANES), lambda i, *_: (i, 0)),
                  pl.BlockSpec((None, tm, pd), lambda i, *_: (layer, i, 0)),
                  pl.BlockSpec((1, d), lambda i, *_: (0, 0)),
                  pl.BlockSpec((d, d), lambda i, *_: (0, 0)),
                  pl.BlockSpec((pd, d), lambda i, *_: (0, 0))],
        out_specs=pl.BlockSpec((tm, d), lambda i, *_: (i, 0)),
        scratch_shapes=[pltpu.VMEM((2, SORT_ROWS, d // 2), jnp.uint32), pltpu.SemaphoreType.DMA((2,))],
    )
    return pl.pallas_call(
        _combine_ple_kernel,
        grid_spec=grid_spec,
        out_shape=jax.ShapeDtypeStruct((n, d), F32),
        compiler_params=pltpu.CompilerParams(dimension_semantics=("arbitrary",),
                                             vmem_limit_bytes=VMEM_LIMIT),
        name="moe_combine_ple",
    )(tab_cnt, tab_off, tab_seg, y, h, meta, p, g.reshape(1, d), w_gate, w_proj)


def moe_and_embedding(h, ffn_norm, w_rg, b_rg, w_re, b_re, w1, w3, w2, p, ple_norm, w_ple_gate, w_ple_proj, layer):
    n, d = h.shape
    pad = ROUTER_LANES - MOE_GROUPS - MOE_EXPERTS
    w_router = jnp.concatenate([w_rg, w_re, jnp.zeros((d, pad), F32)], axis=1)
    b_router = jnp.concatenate([b_rg, b_re, jnp.zeros((pad,), F32)]).reshape(1, ROUTER_LANES)
    xn, meta, post, tabs, cnt = moe_router(h, ffn_norm, w_router, b_router)

    bm = MOE_BM
    nt = n // ROUTER_TM
    lanes = slice(EXP_LANE0, EXP_LANE0 + MOE_EXPERTS)
    totals = cnt[0, lanes].astype(jnp.int32)
    region = (totals + bm - 1) // bm * bm
    region_end = jnp.cumsum(region)
    region_start = region_end - region
    n_blocks = -(-(2 * n + nt * MOE_EXPERTS * (SEG_ALIGN - 1)) // bm) + MOE_EXPERTS
    block_row0 = jnp.arange(n_blocks, dtype=jnp.int32) * bm
    block_e = jnp.minimum(jnp.sum((block_row0[:, None] >= region_end[None, :]).astype(jnp.int32), axis=1),
                          MOE_EXPERTS - 1).astype(jnp.int32)
    n_valid = (region_end[-1:] // bm).astype(jnp.int32)
    tabs = tabs.reshape(nt, 8, ROUTER_LANES)[:, :, lanes].astype(jnp.int32)
    tab_cnt = tabs[:, TAB_CNT].reshape(-1)
    tab_off = tabs[:, TAB_OFF].reshape(-1)
    tab_seg = (tabs[:, TAB_SEG] + region_start[None, :]).reshape(-1)

    x_sorted = moe_dispatch(tab_cnt, tab_off, tab_seg, region - totals, region_start + totals, n_valid, post, xn,
                            n_blocks * bm)
    y = moe_experts(x_sorted, block_e, n_valid, w1, w3, w2, layer)
    return moe_combine_ple(tab_cnt, tab_off, tab_seg, y, h, meta, p, layer, ple_norm, w_ple_gate, w_ple_proj)


def kernel(x, p, a_norm, a_w_in, a_conv, a_A_log, a_dt_bias, a_o_norm, a_w_out, kv_norm, w_kv, k_norm, b_norm, b_w_q, b_q_norm, b_w_out, ffn_norm, w_router_group, b_router_group, w_router_expert, b_router_expert, w1, w3, w2, ple_norm, w_ple_gate, w_ple_proj):
    b, s, d = x.shape
    n = b * s
    depth = p.shape[0]
    n_a = a_norm.shape[0]
    h = x.reshape(n, d)
    p_rows = p.reshape(depth, n, -1)
    kv = None
    for i in range(depth):
        if i < n_a:
            w_in = a_w_in[i]
            proj, gates_t = dn_inproj(h, a_norm[i], w_in[:, :DN_MAIN].astype(BF16), w_in[:, DN_MAIN:].T)
            o = deltanet(proj.reshape(b, s, DN_MAIN), gates_t.reshape(2 * DN_HEADS, b, s // DN_CHUNK, DN_CHUNK),
                         a_conv[i], a_A_log[i], a_dt_bias[i], a_o_norm[i])
            h = matmul_residual(o.reshape(n, DN_V), a_w_out[i].astype(BF16), h)
        else:
            bl = i - n_a
            q = norm_matmul(h, b_norm[bl], b_w_q[bl].astype(BF16), BF16)
            o = dilated_attention(q.reshape(b, s, -1), kv.reshape(b, s, -1), b_q_norm[bl], k_norm)
            h = matmul_residual(o.reshape(n, GROUP_WIDTH), b_w_out[bl].astype(BF16), h)
        h = moe_and_embedding(h, ffn_norm[i], w_router_group[i], b_router_group[i], w_router_expert[i],
                              b_router_expert[i], w1, w3, w2, p_rows, ple_norm[i], w_ple_gate[i].astype(BF16),
                              w_ple_proj[i].astype(BF16), i)
        if i == n_a - 1:
            kv = norm_matmul(h, kv_norm, w_kv.astype(BF16), BF16)
    return h.reshape(b, s, d)
```

```python
import numpy as np
import jax
import jax.numpy as jnp
from jax import lax
from jax.experimental import pallas as pl
from jax.experimental.pallas import tpu as pltpu

F32 = jnp.float32
BF16 = jnp.bfloat16

NORM_EPS = 1e-6

DN_HEADS = 8
DN_DK = 128
DN_DV = 128
DN_CONV = 4
DN_CHUNK = 128
DN_SQUARINGS = DN_CHUNK.bit_length() - 2
DN_GROUP = 16
assert DN_CHUNK == DN_DK == DN_DV
DN_QK = DN_HEADS * DN_DK
DN_V = DN_HEADS * DN_DV
DN_MAIN = 2 * DN_QK + 2 * DN_V

DIL_CONFIGS = ((128, 1), (512, 4), (2048, 16))
N_ATT_GROUPS = len(DIL_CONFIGS)
HEAD_DIM = 128
Q_PER_GROUP = 4
KV_PER_GROUP = 2
Q_REP = Q_PER_GROUP // KV_PER_GROUP
ATT_BLOCK = 128
ALIBI_MAX = 8.0
GROUP_WIDTH = Q_PER_GROUP * HEAD_DIM

MOE_GROUPS = 4
MOE_EPG = 8
MOE_EXPERTS = MOE_GROUPS * MOE_EPG
MOE_HIDDEN = 512
MOE_BM = 512
ROUTER_LANES = 128
EXP_LANE0 = MOE_GROUPS

LANES = 128
VMEM_LIMIT = 48 * 1024 * 1024
VMEM_LIMIT_DELTANET = 56 * 1024 * 1024


def _alibi_slopes():
    n = N_ATT_GROUPS * Q_PER_GROUP
    s = 2.0 ** (-ALIBI_MAX * np.arange(1, n + 1) / n)
    return s.reshape(N_ATT_GROUPS, KV_PER_GROUP, Q_REP)


def _rms(x, g):
    ms = jnp.mean(x * x, axis=-1, keepdims=True)
    return x * lax.rsqrt(ms + NORM_EPS) * g


def _dot(a, b):
    return jnp.dot(a.astype(BF16), b.astype(BF16), preferred_element_type=F32)


def _dot_nt(a, b):
    return lax.dot_general(a.astype(BF16), b.astype(BF16), (((1,), (1,)), ((), ())),
                           preferred_element_type=F32)


def _dot_tn(a, b):
    return lax.dot_general(a.astype(BF16), b.astype(BF16), (((0,), (0,)), ((), ())),
                           preferred_element_type=F32)


def _split2(x):
    hi = x.astype(BF16)
    lo = (x - hi.astype(F32)).astype(BF16)
    return hi, lo


def _split3(x):
    hi = x.astype(BF16)
    r = x - hi.astype(F32)
    mid = r.astype(BF16)
    lo = (r - mid.astype(F32)).astype(BF16)
    return hi, mid, lo


def _dot_exact01(x, sel):
    hi, mid, lo = _split3(x)
    d = lambda p: jnp.dot(p, sel, preferred_element_type=F32)
    return d(hi) + d(mid) + d(lo)


def _aligned(i, m):
    return i if isinstance(i, int) else pl.multiple_of(i, m)


def _sigmoid(x):
    return 1.0 / (1.0 + jnp.exp(-x))


def _silu(x):
    return x * _sigmoid(x)


ROW_TILE = 512


def _row_tiled_call(body, name, n, row_inputs, resident_inputs, out_widths, out_dtypes, extra_out_specs=(),
                    extra_out_shapes=()):
    tm = ROW_TILE
    row_spec = lambda width: pl.BlockSpec((tm, width), lambda i: (i, 0))
    whole = lambda a: pl.BlockSpec(a.shape, lambda i: (0,) * a.ndim)

    def in_row_spec(a):
        if isinstance(a, tuple):
            arr, layer = a
            return pl.BlockSpec((None, tm, arr.shape[2]), lambda i: (layer, i, 0))
        return row_spec(a.shape[1])

    row_specs = [in_row_spec(a) for a in row_inputs]
    row_inputs = [a[0] if isinstance(a, tuple) else a for a in row_inputs]
    return pl.pallas_call(
        body,
        grid=(n // tm,),
        in_specs=row_specs + [whole(a) for a in resident_inputs],
        out_specs=[row_spec(w) for w in out_widths] + list(extra_out_specs),
        out_shape=[jax.ShapeDtypeStruct((n, w), dt) for w, dt in zip(out_widths, out_dtypes)]
        + list(extra_out_shapes),
        compiler_params=pltpu.CompilerParams(dimension_semantics=("parallel",), vmem_limit_bytes=VMEM_LIMIT),
        name=name,
    )(*row_inputs, *resident_inputs)


def _nm_kernel(x_ref, g_ref, w_ref, o_ref):
    xn = _rms(x_ref[...], g_ref[...]).astype(BF16)
    o_ref[...] = jnp.dot(xn, w_ref[...], preferred_element_type=F32).astype(o_ref.dtype)


def norm_matmul(x, g, w, out_dtype):
    n, k = x.shape
    return _row_tiled_call(_nm_kernel, "norm_matmul", n, [x], [g.reshape(1, k), w], [w.shape[1]], [out_dtype])[0]


def _dn_inproj_kernel(x_ref, g_ref, w_ref, wgt_ref, o_ref, gt_ref):
    xn = _rms(x_ref[...], g_ref[...])
    xh, xl = _split2(xn)
    wh, wl = _split2(wgt_ref[...])
    gt_ref[...] = _dot_nt(wh, xh) + _dot_nt(wh, xl) + _dot_nt(wl, xh)
    o_ref[...] = jnp.dot(xh, w_ref[...], preferred_element_type=F32).astype(o_ref.dtype)


def dn_inproj(x, g, w_main, w_gates_t):
    n, k = x.shape
    ng = w_gates_t.shape[0]
    return _row_tiled_call(_dn_inproj_kernel, "dn_inproj", n, [x], [g.reshape(1, k), w_main, w_gates_t],
                           [w_main.shape[1]], [BF16],
                           extra_out_specs=[pl.BlockSpec((ng, ROW_TILE), lambda i: (0, i))],
                           extra_out_shapes=[jax.ShapeDtypeStruct((ng, n), F32)])


def _mm_res_kernel(a_ref, r_ref, w_ref, o_ref):
    o_ref[...] = r_ref[...] + jnp.dot(a_ref[...], w_ref[...], preferred_element_type=F32)


def matmul_residual(a, w, res):
    return _row_tiled_call(_mm_res_kernel, "matmul_residual", a.shape[0], [a, res], [w], [w.shape[1]], [F32])[0]


DN_PIECE = 256
DN_HALO = 8
DN_HB = 4
assert DN_HEADS % DN_HB == 0


def _deltanet_kernel(alog_ref, dtb_ref, q_ref, k_ref, v_ref, z_ref, cq_ref, ck_ref, cv_ref,
                     bpre_ref, apre_ref, onorm_ref, o_ref,
                     xf_ref, qs_ref, ks_ref, vs_ref, gcum_ref, betac_ref, gc_ref,
                     pm_ref, rq_ref, qq_ref, o0_ref, elast_ref):
    seq = q_ref.shape[1]
    c = DN_CHUNK
    n_chunks = seq // c
    assert 2 * n_chunks <= c
    head0 = pl.program_id(1) * DN_HB
    ki = lax.broadcasted_iota(jnp.int32, (c, c), 0)
    ji = lax.broadcasted_iota(jnp.int32, (c, c), 1)
    upper = jnp.where(ki <= ji, 1.0, 0.0).astype(BF16)
    causal = ki >= ji
    strict = ki > ji
    onorm = onorm_ref[...]

    xf_ref[0:DN_HALO, :] = jnp.zeros((DN_HALO, DN_DK), F32)

    def conv_silu(x_ref, w_ref, hb, finish, out_ref):
        cols = slice(hb * DN_DK, (hb + 1) * DN_DK)
        w = w_ref[:, cols]
        pieces = [slice(p * DN_PIECE, (p + 1) * DN_PIECE) for p in range(seq // DN_PIECE)]
        for rows in pieces:
            xf_ref[DN_HALO + rows.start:DN_HALO + rows.stop, :] = x_ref[0, rows, cols].astype(F32)
        for rows in pieces:
            acc = xf_ref[DN_HALO + rows.start:DN_HALO + rows.stop, :] * w[DN_CONV - 1:DN_CONV, :]
            for j in range(1, DN_CONV):
                acc = acc + xf_ref[DN_HALO + rows.start - j:DN_HALO + rows.stop - j, :] * w[DN_CONV - 1 - j:DN_CONV - j, :]
            out_ref[rows, :] = finish(_silu(acc))

    def l2n(scale):
        return lambda x: x * (lax.rsqrt(jnp.sum(x * x, axis=-1, keepdims=True) + NORM_EPS) * scale)

    def prologue(hb):
        conv_silu(q_ref, cq_ref, hb, l2n(DN_DK ** -0.5), qs_ref)
        conv_silu(k_ref, ck_ref, hb, l2n(1.0), ks_ref)
        conv_silu(v_ref, cv_ref, hb, lambda x: x, vs_ref)
        beta = _sigmoid(bpre_ref[hb, 0])
        a = apre_ref[hb, 0] + dtb_ref[head0 + hb]
        softplus = jnp.maximum(a, 0.0) + jnp.log(1.0 + jnp.exp(-jnp.abs(a)))
        g_log = -jnp.exp(jnp.full(a.shape, alog_ref[head0 + hb], F32)) * softplus
        gcum = _dot_exact01(g_log, upper)
        gcum_ref[...] = gcum
        t = jnp.concatenate([beta, gcum, jnp.zeros((c - 2 * n_chunks, c), F32)], axis=0).T
        for ci in range(n_chunks):
            betac_ref[ci * c:(ci + 1) * c, :] = jnp.broadcast_to(t[:, ci:ci + 1], (c, DN_DV))
            gc_ref[ci * c:(ci + 1) * c, :] = jnp.broadcast_to(t[:, n_chunks + ci:n_chunks + ci + 1], (c, DN_DV))

    def prepare(hb, cis):
        each = lambda f, *ls: [f(*xs) for xs in zip(*ls)]
        rows = [pl.ds(_aligned(ci * c, c), c) for ci in cis]
        qc = [qs_ref[r, :] for r in rows]
        kc = [ks_ref[r, :] for r in rows]
        vc = [vs_ref[r, :] for r in rows]
        beta_c = [betac_ref[r, :] for r in rows]
        g_c = [gc_ref[r, :] for r in rows]
        g_j = [jnp.broadcast_to(gcum_ref[pl.ds(ci, 1), :], (c, c)) for ci in cis]
        decay = each(lambda gi, gj: jnp.exp(jnp.where(causal, gi - gj, -jnp.inf)), g_c, g_j)
        kq = each(lambda k, q: _dot_nt(jnp.concatenate([k, q], axis=0), k), kc, qc)
        m = each(lambda b, x, d: jnp.where(strict, -(b * x[:c, :] * d), 0.0), beta_c, kq, decay)
        pw = each(lambda x: _dot(x, x), m)
        r = m
        for _ in range(DN_SQUARINGS - 1):
            xs = each(lambda p_, r_: _dot(p_, jnp.concatenate([p_, r_], axis=1)), pw, r)
            r = each(lambda r_, p_, x: r_ + p_ + x[:, c:], r, pw, xs)
            pw = [x[:, :c] for x in xs]
        xs = each(_dot, pw, r)
        r = each(lambda r_, p_, x: r_ + p_ + x, r, pw, xs)
        e_g = [jnp.exp(g) for g in g_c]
        rhs = each(lambda b, v, e, k: jnp.concatenate([b * v, b * e * k], axis=1), beta_c, vc, e_g, kc)
        sol = each(lambda rh, r_: rh + _dot(r_, rh), rhs, r)
        attn = each(lambda x, d: jnp.where(causal, x[c:, :] * d, 0.0), kq, decay)
        k_d = each(lambda k, g: k * jnp.exp(jnp.broadcast_to(g[c - 1:c, :], (c, DN_DV)) - g), kc, g_c)
        kt = each(_dot_tn, k_d, sol)
        at = each(_dot, attn, sol)
        for i, (ci, r_) in enumerate(zip(cis, rows)):
            qq_ref[hb, r_, :] = kt[i][:, :DN_DV]
            pm_ref[hb, r_, :] = kt[i][:, DN_DV:].astype(BF16)
            o0_ref[hb, r_, :] = at[i][:, :DN_DV]
            rq_ref[hb, r_, :] = (qc[i] * e_g[i] - at[i][:, DN_DV:]).astype(BF16)
            elast_ref[hb, pl.ds(_aligned(ci * 8, 8), 8), :] = jnp.exp(
                jnp.broadcast_to(g_c[i][c - 1:c, :], (8, DN_DV)))

    group = min(DN_GROUP, n_chunks)
    assert n_chunks % group == 0
    for hb in range(DN_HB):
        prologue(hb)
        if group == n_chunks:
            prepare(hb, list(range(n_chunks)))
        else:
            def prepare_group(gi, carry, hb=hb):
                prepare(hb, [gi * group + k for k in range(group)])
                return carry

            lax.fori_loop(0, n_chunks // group, prepare_group, 0)

    def chunk_step(ci, states):
        rows = pl.ds(pl.multiple_of(ci * c, c), c)
        xs = [_dot(jnp.concatenate([pm_ref[hb, rows, :], rq_ref[hb, rows, :]], axis=0), states[hb])
              for hb in range(DN_HB)]
        new_states = []
        for hb in range(DN_HB):
            cols = slice(hb * DN_DV, (hb + 1) * DN_DV)
            e_last = jnp.broadcast_to(elast_ref[hb, pl.ds(pl.multiple_of(ci * 8, 8), 1), :], (DN_DK, DN_DV))
            new_states.append(e_last * states[hb] - xs[hb][:c, :] + qq_ref[hb, rows, :])
            o = xs[hb][c:, :] + o0_ref[hb, rows, :]
            zc = z_ref[0, rows, cols].astype(F32)
            o_ref[0, rows, cols] = (_rms(o, onorm) * _silu(zc)).astype(o_ref.dtype)
        return tuple(new_states)

    lax.fori_loop(0, n_chunks, chunk_step, tuple(jnp.zeros((DN_DK, DN_DV), F32) for _ in range(DN_HB)))


def deltanet(proj, gates_t, conv_w, a_log, dt_bias, o_norm):
    b, s, _ = proj.shape
    ng = DN_HEADS // DN_HB
    nc = s // DN_CHUNK
    wide = DN_HB * DN_DK
    col = lambda off: pl.BlockSpec((1, s, wide), lambda bi, hi: (bi, 0, off + hi))
    cw = lambda off: pl.BlockSpec((DN_CONV, wide), lambda bi, hi: (0, off + hi))
    gate = lambda off: pl.BlockSpec((DN_HB, 1, nc, DN_CHUNK), lambda bi, hi: (off + hi, bi, 0, 0))
    smem = pl.BlockSpec(memory_space=pltpu.SMEM)
    per_head = lambda dt: pltpu.VMEM((DN_HB, s, DN_DV), dt)
    return pl.pallas_call(
        _deltanet_kernel,
        grid=(b, ng),
        in_specs=[smem, smem, col(0), col(ng), col(2 * ng), col(3 * ng), cw(0), cw(ng), cw(2 * ng),
                  gate(0), gate(ng), pl.BlockSpec((1, DN_DV), lambda bi, hi: (0, 0))],
        out_specs=pl.BlockSpec((1, s, wide), lambda bi, hi: (bi, 0, hi)),
        out_shape=jax.ShapeDtypeStruct((b, s, DN_V), BF16),
        scratch_shapes=[pltpu.VMEM((DN_HALO + s, DN_DK), F32),
                        pltpu.VMEM((s, DN_DK), F32), pltpu.VMEM((s, DN_DK), F32), pltpu.VMEM((s, DN_DV), F32),
                        pltpu.VMEM((nc, DN_CHUNK), F32), pltpu.VMEM((s, DN_DV), F32), pltpu.VMEM((s, DN_DV), F32),
                        per_head(BF16), per_head(BF16), per_head(F32), per_head(F32),
                        pltpu.VMEM((DN_HB, nc * 8, DN_DV), F32)],
        compiler_params=pltpu.CompilerParams(dimension_semantics=("parallel", "parallel"),
                                             vmem_limit_bytes=VMEM_LIMIT_DELTANET),
        name="deltanet",
    )(a_log, dt_bias, proj, proj, proj, proj, conv_w, conv_w, conv_w, gates_t, gates_t, o_norm.reshape(1, DN_DV))


ATT_PIECE = 256
ATT_M_INIT = -1e30


def _attention_kernel(q_ref, kv_ref, qn_ref, kn_ref, o_ref, qf_ref, kf_ref, vf_ref, acc_ref, m_ref, l_ref):
    seq = q_ref.shape[1]
    grp = pl.program_id(1)
    hd = HEAD_DIM
    blk = ATT_BLOCK
    slopes = _alibi_slopes()

    qg = qn_ref[0] * (hd ** -0.5)
    kg = kn_ref[0]

    def prep(pi, carry):
        r0 = pl.multiple_of(pi * ATT_PIECE, ATT_PIECE)
        rows = pl.ds(r0, ATT_PIECE)
        for j in range(Q_PER_GROUP):
            cols = slice(j * hd, (j + 1) * hd)
            qf_ref[j, rows, :] = _rms(q_ref[0, rows, cols].astype(F32), 1.0) * qg
        for j in range(KV_PER_GROUP):
            cols = slice(j * hd, (j + 1) * hd)
            kf_ref[j, rows, :] = _rms(kv_ref[0, rows, cols].astype(F32), 1.0) * kg
            vcols = slice((KV_PER_GROUP + j) * hd, (KV_PER_GROUP + j + 1) * hd)
            vf_ref[j, rows, :] = kv_ref[0, rows, vcols].astype(F32)
        return carry

    lax.fori_loop(0, seq // ATT_PIECE, prep, 0)

    @pl.when(grp == 0)
    def _():
        def init(pi, carry):
            r0 = pl.multiple_of(pi * ATT_PIECE, ATT_PIECE)
            rows = pl.ds(r0, ATT_PIECE)
            for j in range(Q_PER_GROUP):
                acc_ref[j, rows, :] = jnp.zeros((ATT_PIECE, hd), F32)
                l_ref[j, rows, :] = jnp.zeros((ATT_PIECE, hd), F32)
                m_ref[j, rows, :] = jnp.full((ATT_PIECE, hd), ATT_M_INIT, F32)
            return carry

        lax.fori_loop(0, seq // ATT_PIECE, init, 0)

    def rows_of(start, size, dil):
        return pl.ds(start, size) if dil == 1 else pl.ds(start, size, stride=dil)

    def attend(g, dil, q_start, k_start, nk):
        qrows = rows_of(q_start, blk, dil)
        krows = rows_of(k_start, nk, dil)
        qi = lax.broadcasted_iota(jnp.int32, (blk, nk), 0)
        kidx = lax.broadcasted_iota(jnp.int32, (blk, nk), 1)
        dist = (nk - blk) + qi - kidx
        valid = (dist >= 0) & (dist <= blk)
        distf = dist.astype(F32)
        kv_heads = range(KV_PER_GROUP)
        heads = [(kvh, rep) for kvh in kv_heads for rep in range(Q_REP)]
        vb = [vf_ref[kvh, krows, :].astype(BF16) for kvh in kv_heads]
        sc2 = [_dot_nt(jnp.concatenate([qf_ref[kvh * Q_REP + rep, qrows, :] for rep in range(Q_REP)], axis=0),
                       kf_ref[kvh, krows, :]) for kvh in kv_heads]
        sc = [jnp.where(valid, sc2[kvh][rep * blk:(rep + 1) * blk, :] - float(slopes[g, kvh, rep] * dil) * distf,
                        -jnp.inf) for kvh, rep in heads]
        m_old = [m_ref[kvh * Q_REP + rep, qrows, :] for kvh, rep in heads]
        m_new = [jnp.maximum(mo, jnp.max(s, axis=-1, keepdims=True)) for mo, s in zip(m_old, sc)]
        alpha = [jnp.exp(mo - mn) for mo, mn in zip(m_old, m_new)]
        p = [jnp.exp(s - mn[:, 0:1]) for s, mn in zip(sc, m_new)]
        pv2 = [jnp.dot(jnp.concatenate([p[kvh * Q_REP + rep].astype(BF16) for rep in range(Q_REP)], axis=0),
                       vb[kvh], preferred_element_type=F32) for kvh in kv_heads]
        for i, (kvh, rep) in enumerate(heads):
            j = kvh * Q_REP + rep
            l_ref[j, qrows, :] = alpha[i] * l_ref[j, qrows, :] + jnp.sum(p[i], axis=-1, keepdims=True)
            acc_ref[j, qrows, :] = alpha[i] * acc_ref[j, qrows, :] + pv2[kvh][rep * blk:(rep + 1) * blk, :]
            m_ref[j, qrows, :] = m_new[i]

    for g, (window, dil) in enumerate(DIL_CONFIGS):
        sub_len = seq // dil
        nblk = sub_len // blk

        @pl.when(grp == g)
        def _(g=g, dil=dil, nblk=nblk):
            def residue(res, carry):
                attend(g, dil, res, res, blk)
                if nblk > 1:
                    def later(n, c2):
                        attend(g, dil, res + n * blk * dil, res + (n - 1) * blk * dil, 2 * blk)
                        return c2
                    lax.fori_loop(1, nblk, later, 0)
                return carry

            lax.fori_loop(0, dil, residue, 0)

    @pl.when(grp == N_ATT_GROUPS - 1)
    def _():
        def finish(pi, carry):
            r0 = pl.multiple_of(pi * ATT_PIECE, ATT_PIECE)
            rows = pl.ds(r0, ATT_PIECE)
            for j in range(Q_PER_GROUP):
                cols = slice(j * hd, (j + 1) * hd)
                o_ref[0, rows, cols] = (acc_ref[j, rows, :] / l_ref[j, rows, :]).astype(o_ref.dtype)
            return carry

        lax.fori_loop(0, seq // ATT_PIECE, finish, 0)


def dilated_attention(q, kv, q_norm, k_norm):
    b, s, _ = q.shape
    gw = GROUP_WIDTH
    return pl.pallas_call(
        _attention_kernel,
        grid=(b, N_ATT_GROUPS),
        in_specs=[pl.BlockSpec((1, s, gw), lambda bi, gi: (bi, 0, gi)),
                  pl.BlockSpec((1, s, gw), lambda bi, gi: (bi, 0, gi)),
                  pl.BlockSpec((1, 1, HEAD_DIM), lambda bi, gi: (gi, 0, 0)),
                  pl.BlockSpec((1, 1, HEAD_DIM), lambda bi, gi: (gi, 0, 0))],
        out_specs=pl.BlockSpec((1, s, gw), lambda bi, gi: (bi, 0, 0)),
        out_shape=jax.ShapeDtypeStruct((b, s, gw), BF16),
        scratch_shapes=[pltpu.VMEM((Q_PER_GROUP, s, HEAD_DIM), F32),
                        pltpu.VMEM((KV_PER_GROUP, s, HEAD_DIM), F32),
                        pltpu.VMEM((KV_PER_GROUP, s, HEAD_DIM), F32),
                        pltpu.VMEM((Q_PER_GROUP, s, HEAD_DIM), F32),
                        pltpu.VMEM((Q_PER_GROUP, s, HEAD_DIM), F32),
                        pltpu.VMEM((Q_PER_GROUP, s, HEAD_DIM), F32)],
        compiler_params=pltpu.CompilerParams(dimension_semantics=("parallel", "arbitrary"),
                                             vmem_limit_bytes=VMEM_LIMIT),
        name="dilated_attention",
    )(q, kv, q_norm.reshape(N_ATT_GROUPS, 1, HEAD_DIM), k_norm.reshape(N_ATT_GROUPS, 1, HEAD_DIM))


ROUTER_TM = 512
SEG_ALIGN = 8
SORT_ROWS = 2 * ROUTER_TM + 256
assert SORT_ROWS >= 2 * ROUTER_TM + MOE_EXPERTS * (SEG_ALIGN - 1) and SORT_ROWS % LANES == 0
META_W0, META_W1, META_P0, META_P1 = 0, 1, 2, 3
TAB_CNT, TAB_OFF, TAB_SEG = 0, 1, 2


def _router_kernel(h_ref, g_ref, w_ref, b_ref, xn_ref, meta_ref, post_ref, tab_ref, cnt_ref, carry_ref):
    tm = h_ref.shape[0]

    @pl.when(pl.program_id(0) == 0)
    def _():
        carry_ref[...] = jnp.zeros_like(carry_ref)

    xn = _rms(h_ref[...], g_ref[...])
    xn_ref[...] = xn.astype(BF16)
    xh, xl = _split2(xn)
    wh, wl = _split2(w_ref[...])
    d = lambda a, bb: jnp.dot(a, bb, preferred_element_type=F32)
    logits = d(xh, wh) + d(xh, wl) + d(xl, wh) + b_ref[...]

    lane = lax.broadcasted_iota(jnp.int32, (tm, ROUTER_LANES), 1)
    big = jnp.int32(ROUTER_LANES)
    first_where = lambda cond: jnp.min(jnp.where(cond, lane, big), axis=-1, keepdims=True)

    gl = jnp.where(lane < MOE_GROUPS, logits, -jnp.inf)
    ge = jnp.exp(gl - jnp.max(gl, axis=-1, keepdims=True))
    gp = ge / jnp.sum(ge, axis=-1, keepdims=True)
    g_w = jnp.max(gp, axis=-1, keepdims=True)
    g_idx = first_where(gp == g_w)

    lo = EXP_LANE0 + g_idx * MOE_EPG
    in_group = (lane >= lo) & (lane < lo + MOE_EPG)
    el = jnp.where(in_group, logits, -jnp.inf)
    ee = jnp.exp(el - jnp.max(el, axis=-1, keepdims=True))
    ep = ee / jnp.sum(ee, axis=-1, keepdims=True)
    p0 = jnp.max(jnp.where(in_group, ep, -1.0), axis=-1, keepdims=True)
    i0 = first_where(in_group & (ep == p0))
    rest = in_group & (lane != i0)
    p1 = jnp.max(jnp.where(rest, ep, -1.0), axis=-1, keepdims=True)
    i1 = first_where(rest & (ep == p1))
    w0 = g_w * p0 / (p0 + p1)
    w1 = g_w * p1 / (p0 + p1)

    oh0 = jnp.where(lane == i0, 1.0, 0.0)
    oh1 = jnp.where(lane == i1, 1.0, 0.0)
    both = oh0 + oh1
    ti = lax.broadcasted_iota(jnp.int32, (tm, tm), 0)
    tj = lax.broadcasted_iota(jnp.int32, (tm, tm), 1)
    before = jnp.where(tj < ti, 1.0, 0.0).astype(BF16)
    within = jnp.dot(before, both.astype(BF16), preferred_element_type=F32)
    cnt = jnp.sum(both, axis=0, keepdims=True)
    cnt_pad = jnp.floor((cnt + (SEG_ALIGN - 1)) * (1.0 / SEG_ALIGN)) * SEG_ALIGN
    li = lax.broadcasted_iota(jnp.int32, (ROUTER_LANES, ROUTER_LANES), 0)
    lj = lax.broadcasted_iota(jnp.int32, (ROUTER_LANES, ROUTER_LANES), 1)
    earlier = jnp.where(li < lj, 1.0, 0.0).astype(BF16)
    tile_off = _dot_exact01(jnp.broadcast_to(cnt_pad, (8, ROUTER_LANES)), earlier)[0:1, :]
    row = tile_off + within
    pos0 = jnp.sum(row * oh0, axis=-1, keepdims=True)
    pos1 = jnp.sum(row * oh1, axis=-1, keepdims=True)
    seg_off = carry_ref[...]
    total = seg_off + cnt_pad
    carry_ref[...] = total
    cnt_ref[...] = jnp.broadcast_to(total, cnt_ref.shape)

    sub = lax.broadcasted_iota(jnp.int32, (8, ROUTER_LANES), 0)
    tab_ref[...] = jnp.where(sub == TAB_CNT, cnt_pad, jnp.where(sub == TAB_OFF, tile_off,
                                                                jnp.where(sub == TAB_SEG, seg_off, 0.0)))
    meta = jnp.zeros((tm, ROUTER_LANES), F32)
    for idx, val in ((META_W0, w0), (META_W1, w1), (META_P0, pos0), (META_P1, pos1)):
        meta = jnp.where(lane == idx, val, meta)
    meta_ref[...] = meta
    post_ref[0] = meta.T[0:8, :]


def moe_router(h, g, w_router, b_router):
    n, k = h.shape
    tm = ROUTER_TM
    nt = n // tm
    return pl.pallas_call(
        _router_kernel,
        grid=(nt,),
        in_specs=[pl.BlockSpec((tm, k), lambda i: (i, 0)),
                  pl.BlockSpec((1, k), lambda i: (0, 0)),
                  pl.BlockSpec((k, ROUTER_LANES), lambda i: (0, 0)),
                  pl.BlockSpec((1, ROUTER_LANES), lambda i: (0, 0))],
        out_specs=[pl.BlockSpec((tm, k), lambda i: (i, 0)),
                   pl.BlockSpec((tm, ROUTER_LANES), lambda i: (i, 0)),
                   pl.BlockSpec((1, 8, tm), lambda i: (i, 0, 0)),
                   pl.BlockSpec((8, ROUTER_LANES), lambda i: (i, 0)),
                   pl.BlockSpec((8, ROUTER_LANES), lambda i: (0, 0))],
        out_shape=[jax.ShapeDtypeStruct((n, k), BF16),
                   jax.ShapeDtypeStruct((n, ROUTER_LANES), F32),
                   jax.ShapeDtypeStruct((nt, 8, tm), F32),
                   jax.ShapeDtypeStruct((nt * 8, ROUTER_LANES), F32),
                   jax.ShapeDtypeStruct((8, ROUTER_LANES), F32)],
        scratch_shapes=[pltpu.VMEM((1, ROUTER_LANES), F32)],
        compiler_params=pltpu.CompilerParams(dimension_semantics=("arbitrary",),
                                             vmem_limit_bytes=VMEM_LIMIT),
        name="moe_router",
    )(h, g.reshape(1, k), w_router, b_router)


def _pack_halves(x):
    k = x.shape[1] // 2
    lo = pltpu.bitcast(x[:, :k].astype(BF16).astype(F32), jnp.uint32)
    hi = pltpu.bitcast(x[:, k:].astype(BF16).astype(F32), jnp.uint32)
    return (hi & jnp.uint32(0xFFFF0000)) | (lo >> 16)


def _unpack_halves(w):
    lo = pltpu.bitcast(w << 16, F32)
    hi = pltpu.bitcast(w & jnp.uint32(0xFFFF0000), F32)
    return lo.astype(BF16), hi.astype(BF16)


SEG_PIECE = 2 * SEG_ALIGN


def _segment_copies(rows_of, src_ref, dst_ref, sem, wait):
    def piece(s, d, rows):
        cp = pltpu.make_async_copy(src_ref.at[pl.ds(pl.multiple_of(s, SEG_ALIGN), rows), :],
                                   dst_ref.at[pl.ds(pl.multiple_of(d, SEG_ALIGN), rows), :], sem)
        cp.wait() if wait else cp.start()

    def per_expert(e, carry):
        cnt, s0, d0 = rows_of(e)
        n_full = cnt // SEG_PIECE

        def full_piece(j, c2):
            piece(s0 + j * SEG_PIECE, d0 + j * SEG_PIECE, SEG_PIECE)
            return c2

        lax.fori_loop(0, n_full, full_piece, 0)

        @pl.when(cnt % SEG_PIECE != 0)
        def _():
            piece(s0 + n_full * SEG_PIECE, d0 + n_full * SEG_PIECE, SEG_ALIGN)

        return carry

    lax.fori_loop(0, MOE_EXPERTS, per_expert, 0)


def _tile_segments(cnt_ref, tile, src_off_ref, dst_off_ref):
    def rows_of(e):
        k = tile * MOE_EXPERTS + e
        return cnt_ref[k], src_off_ref[k], dst_off_ref[k]
    return rows_of


def _dispatch_kernel(cnt_ref, off_ref, seg_ref, tail_cnt_ref, tail_dst_ref, nvalid_ref, post_ref, x_ref, out_hbm,
                     xs_ref, zero_ref, sems):
    tile = pl.program_id(0)
    tm = x_ref.shape[0]
    bm = zero_ref.shape[0]
    sem = sems.at[2]
    n_blocks = out_hbm.shape[0] // bm

    @pl.when(tile == 0)
    def _():
        zero_ref[...] = jnp.zeros_like(zero_ref)
        tails = lambda e: (tail_cnt_ref[e], 0, tail_dst_ref[e])

        def unused_block(wait):
            def body(blk, carry):
                cp = pltpu.make_async_copy(zero_ref, out_hbm.at[pl.ds(pl.multiple_of(blk * bm, bm), bm), :], sem)
                cp.wait() if wait else cp.start()
                return carry
            return body

        _segment_copies(tails, zero_ref, out_hbm, sem, wait=False)
        lax.fori_loop(nvalid_ref[0], n_blocks, unused_block(False), 0)
        _segment_copies(tails, zero_ref, out_hbm, sem, wait=True)
        lax.fori_loop(nvalid_ref[0], n_blocks, unused_block(True), 0)

    n_tiles = pl.num_programs(0)
    slot = tile % 2

    def copies(t, s, wait):
        _segment_copies(_tile_segments(cnt_ref, t, off_ref, seg_ref), xs_ref.at[s], out_hbm, sems.at[s], wait)

    @pl.when(tile >= 2)
    def _():
        copies(tile - 2, slot, True)

    post = post_ref[0]
    p0 = post[META_P0:META_P0 + 1, :].astype(jnp.int32)
    p1 = post[META_P1:META_P1 + 1, :].astype(jnp.int32)
    r = lax.broadcasted_iota(jnp.int32, (SORT_ROWS, tm), 0)
    sel = jnp.where(r == p0, 1.0, jnp.where(r == p1, 1.0, 0.0)).astype(BF16)
    xs_ref[slot] = _pack_halves(jnp.dot(sel, x_ref[...], preferred_element_type=F32))
    copies(tile, slot, False)

    @pl.when(tile == n_tiles - 1)
    def _():
        @pl.when(tile >= 1)
        def _():
            copies(tile - 1, 1 - slot, True)

        copies(tile, slot, True)


def moe_dispatch(tab_cnt, tab_off, tab_seg, tail_cnt, tail_dst, n_valid, post, xn, rows):
    n, d = xn.shape
    tm = ROUTER_TM
    grid_spec = pltpu.PrefetchScalarGridSpec(
        num_scalar_prefetch=6,
        grid=(n // tm,),
        in_specs=[pl.BlockSpec((1, 8, tm), lambda i, *_: (i, 0, 0)),
                  pl.BlockSpec((tm, d), lambda i, *_: (i, 0))],
        out_specs=pl.BlockSpec(memory_space=pl.ANY),
        scratch_shapes=[pltpu.VMEM((2, SORT_ROWS, d // 2), jnp.uint32), pltpu.VMEM((MOE_BM, d // 2), jnp.uint32),
                        pltpu.SemaphoreType.DMA((3,))],
    )
    return pl.pallas_call(
        _dispatch_kernel,
        grid_spec=grid_spec,
        out_shape=jax.ShapeDtypeStruct((rows, d // 2), jnp.uint32),
        compiler_params=pltpu.CompilerParams(dimension_semantics=("arbitrary",),
                                             vmem_limit_bytes=VMEM_LIMIT),
        name="moe_dispatch",
    )(tab_cnt, tab_off, tab_seg, tail_cnt, tail_dst, n_valid, post, xn)


def _expert_kernel(be_ref, nvalid_ref, x_ref, w1_ref, w3_ref, w2_ref, y_ref, w1b_ref, w3b_ref, w2b_ref):
    i = pl.program_id(0)

    @pl.when(i < nvalid_ref[0])
    def _():
        @pl.when(jnp.logical_or(i == 0, be_ref[i] != be_ref[jnp.maximum(i - 1, 0)]))
        def _():
            w1b_ref[...] = w1_ref[0, 0].astype(BF16)
            w3b_ref[...] = w3_ref[0, 0].astype(BF16)
            w2b_ref[...] = w2_ref[0, 0].astype(BF16)

        xlo, xhi = _unpack_halves(x_ref[...])
        half = xlo.shape[1]
        up = lambda w_ref: (jnp.dot(xlo, w_ref[:half, :], preferred_element_type=F32)
                            + jnp.dot(xhi, w_ref[half:, :], preferred_element_type=F32))
        hdn = (_silu(up(w1b_ref)) * up(w3b_ref)).astype(BF16)
        y_ref[...] = _pack_halves(jnp.dot(hdn, w2b_ref[...], preferred_element_type=F32))

    @pl.when(i >= nvalid_ref[0])
    def _():
        y_ref[...] = jnp.zeros_like(y_ref)


def moe_experts(x_sorted, block_e, n_valid, w1, w3, w2, layer):
    rows, half = x_sorted.shape
    n_blocks = block_e.shape[0]
    bm = MOE_BM
    d, hid = w1.shape[2], w1.shape[3]
    grid_spec = pltpu.PrefetchScalarGridSpec(
        num_scalar_prefetch=2,
        grid=(n_blocks,),
        in_specs=[pl.BlockSpec((bm, half), lambda i, be, nv: (jnp.minimum(i, nv[0] - 1), 0)),
                  pl.BlockSpec((1, 1, d, hid), lambda i, be, nv: (layer, be[i], 0, 0)),
                  pl.BlockSpec((1, 1, d, hid), lambda i, be, nv: (layer, be[i], 0, 0)),
                  pl.BlockSpec((1, 1, hid, d), lambda i, be, nv: (layer, be[i], 0, 0))],
        out_specs=pl.BlockSpec((bm, half), lambda i, be, nv: (i, 0)),
        scratch_shapes=[pltpu.VMEM((d, hid), BF16), pltpu.VMEM((d, hid), BF16), pltpu.VMEM((hid, d), BF16)],
    )
    return pl.pallas_call(
        _expert_kernel,
        grid_spec=grid_spec,
        out_shape=jax.ShapeDtypeStruct((rows, half), jnp.uint32),
        compiler_params=pltpu.CompilerParams(dimension_semantics=("arbitrary",),
                                             vmem_limit_bytes=VMEM_LIMIT),
        name="moe_experts",
    )(block_e, n_valid, x_sorted, w1, w3, w2)


def _combine_ple_kernel(cnt_ref, off_ref, seg_ref, y_hbm, h_ref, meta_ref, p_ref, g_ref, wg_ref, wp_ref, o_ref,
                        ys_ref, sems):
    tile = pl.program_id(0)
    n_tiles = pl.num_programs(0)
    tm, d = h_ref.shape
    slot = tile % 2

    def copies(t, s, wait):
        _segment_copies(_tile_segments(cnt_ref, t, seg_ref, off_ref), y_hbm, ys_ref.at[s], sems.at[s], wait)

    @pl.when(tile == 0)
    def _():
        ys_ref[...] = jnp.zeros_like(ys_ref)
        copies(0, 0, False)

    @pl.when(tile + 1 < n_tiles)
    def _():
        copies(tile + 1, 1 - slot, False)

    copies(tile, slot, True)
    meta = meta_ref[...]
    w0 = meta[:, META_W0:META_W0 + 1]
    w1 = meta[:, META_W1:META_W1 + 1]
    p0 = meta[:, META_P0:META_P0 + 1].astype(jnp.int32)
    p1 = meta[:, META_P1:META_P1 + 1].astype(jnp.int32)
    r = lax.broadcasted_iota(jnp.int32, (tm, SORT_ROWS), 1)
    wmat = jnp.where(r == p0, w0, jnp.where(r == p1, w1, 0.0)).astype(BF16)
    ylo, yhi = _unpack_halves(ys_ref[slot])
    mix = lambda y: jnp.dot(wmat, y, preferred_element_type=F32)
    half = d // 2
    h = jnp.concatenate([h_ref[:, :half] + mix(ylo), h_ref[:, half:] + mix(yhi)], axis=1)
    gate = _sigmoid(jnp.dot(_rms(h, g_ref[...]).astype(BF16), wg_ref[...], preferred_element_type=F32))
    proj = jnp.dot(p_ref[...].astype(BF16), wp_ref[...], preferred_element_type=F32)
    o_ref[...] = h + gate * proj


def moe_combine_ple(tab_cnt, tab_off, tab_seg, y, h, meta, p, layer, g, w_gate, w_proj):
    n, d = h.shape
    pd = p.shape[2]
    tm = ROUTER_TM
    grid_spec = pltpu.PrefetchScalarGridSpec(
        num_scalar_prefetch=3,
        grid=(n // tm,),
        in_specs=[pl.BlockSpec(memory_space=pl.ANY),
                  pl.BlockSpec((tm, d), lambda i, *_: (i, 0)),
                  pl.BlockSpec((tm, ROUTER_LANES), lambda i, *_: (i, 0)),
                  pl.BlockSpec((None, tm, pd), lambda i, *_: (layer, i, 0)),
                  pl.BlockSpec((1, d), lambda i, *_: (0, 0)),
                  pl.BlockSpec((d, d), lambda i, *_: (0, 0)),
                  pl.BlockSpec((pd, d), lambda i, *_: (0, 0))],
        out_specs=pl.BlockSpec((tm, d), lambda i, *_: (i, 0)),
        scratch_shapes=[pltpu.VMEM((2, SORT_ROWS, d // 2), jnp.uint32), pltpu.SemaphoreType.DMA((2,))],
    )
    return pl.pallas_call(
        _combine_ple_kernel,
        grid_spec=grid_spec,
        out_shape=jax.ShapeDtypeStruct((n, d), F32),
        compiler_params=pltpu.CompilerParams(dimension_semantics=("arbitrary",),
                                             vmem_limit_bytes=VMEM_LIMIT),
        name="moe_combine_ple",
    )(tab_cnt, tab_off, tab_seg, y, h, meta, p, g.reshape(1, d), w_gate, w_proj)


def moe_and_embedding(h, ffn_norm, w_rg, b_rg, w_re, b_re, w1, w3, w2, p, ple_norm, w_ple_gate, w_ple_proj, layer):
    n, d = h.shape
    pad = ROUTER_LANES - MOE_GROUPS - MOE_EXPERTS
    w_router = jnp.concatenate([w_rg, w_re, jnp.zeros((d, pad), F32)], axis=1)
    b_router = jnp.concatenate([b_rg, b_re, jnp.zeros((pad,), F32)]).reshape(1, ROUTER_LANES)
    xn, meta, post, tabs, cnt = moe_router(h, ffn_norm, w_router, b_router)

    bm = MOE_BM
    nt = n // ROUTER_TM
    lanes = slice(EXP_LANE0, EXP_LANE0 + MOE_EXPERTS)
    totals = cnt[0, lanes].astype(jnp.int32)
    region = (totals + bm - 1) // bm * bm
    region_end = jnp.cumsum(region)
    region_start = region_end - region
    n_blocks = -(-(2 * n + nt * MOE_EXPERTS * (SEG_ALIGN - 1)) // bm) + MOE_EXPERTS
    block_row0 = jnp.arange(n_blocks, dtype=jnp.int32) * bm
    block_e = jnp.minimum(jnp.sum((block_row0[:, None] >= region_end[None, :]).astype(jnp.int32), axis=1),
                          MOE_EXPERTS - 1).astype(jnp.int32)
    n_valid = (region_end[-1:] // bm).astype(jnp.int32)
    tabs = tabs.reshape(nt, 8, ROUTER_LANES)[:, :, lanes].astype(jnp.int32)
    tab_cnt = tabs[:, TAB_CNT].reshape(-1)
    tab_off = tabs[:, TAB_OFF].reshape(-1)
    tab_seg = (tabs[:, TAB_SEG] + region_start[None, :]).reshape(-1)

    x_sorted = moe_dispatch(tab_cnt, tab_off, tab_seg, region - totals, region_start + totals, n_valid, post, xn,
                            n_blocks * bm)
    y = moe_experts(x_sorted, block_e, n_valid, w1, w3, w2, layer)
    return moe_combine_ple(tab_cnt, tab_off, tab_seg, y, h, meta, p, layer, ple_norm, w_ple_gate, w_ple_proj)


def kernel(x, p, a_norm, a_w_in, a_conv, a_A_log, a_dt_bias, a_o_norm, a_w_out, kv_norm, w_kv, k_norm, b_norm, b_w_q, b_q_norm, b_w_out, ffn_norm, w_router_group, b_router_group, w_router_expert, b_router_expert, w1, w3, w2, ple_norm, w_ple_gate, w_ple_proj):
    b, s, d = x.shape
    n = b * s
    depth = p.shape[0]
    n_a = a_norm.shape[0]
    h = x.reshape(n, d)
    p_rows = p.reshape(depth, n, -1)
    kv = None
    for i in range(depth):
        if i < n_a:
            w_in = a_w_in[i]
            proj, gates_t = dn_inproj(h, a_norm[i], w_in[:, :DN_MAIN].astype(BF16), w_in[:, DN_MAIN:].T)
            o = deltanet(proj.reshape(b, s, DN_MAIN), gates_t.reshape(2 * DN_HEADS, b, s // DN_CHUNK, DN_CHUNK),
                         a_conv[i], a_A_log[i], a_dt_bias[i], a_o_norm[i])
            h = matmul_residual(o.reshape(n, DN_V), a_w_out[i].astype(BF16), h)
        else:
            bl = i - n_a
            q = norm_matmul(h, b_norm[bl], b_w_q[bl].astype(BF16), BF16)
            o = dilated_attention(q.reshape(b, s, -1), kv.reshape(b, s, -1), b_q_norm[bl], k_norm)
            h = matmul_residual(o.reshape(n, GROUP_WIDTH), b_w_out[bl].astype(BF16), h)
        h = moe_and_embedding(h, ffn_norm[i], w_router_group[i], b_router_group[i], w_router_expert[i],
                              b_router_expert[i], w1, w3, w2, p_rows, ple_norm[i], w_ple_gate[i].astype(BF16),
                              w_ple_proj[i].astype(BF16), i)
        if i == n_a - 1:
            kv = norm_matmul(h, kv_norm, w_kv.astype(BF16), BF16)
    return h.reshape(b, s, d)
```

```python
import numpy as np
import jax
import jax.numpy as jnp
from jax import lax
from jax.experimental import pallas as pl
from jax.experimental.pallas import tpu as pltpu

F32 = jnp.float32
BF16 = jnp.bfloat16

NORM_EPS = 1e-6

DN_HEADS = 8
DN_DK = 128
DN_DV = 128
DN_CONV = 4
DN_CHUNK = 128
DN_SQUARINGS = DN_CHUNK.bit_length() - 2
DN_GROUP = 16
assert DN_CHUNK == DN_DK == DN_DV
DN_QK = DN_HEADS * DN_DK
DN_V = DN_HEADS * DN_DV
DN_MAIN = 2 * DN_QK + 2 * DN_V

DIL_CONFIGS = ((128, 1), (512, 4), (2048, 16))
N_ATT_GROUPS = len(DIL_CONFIGS)
HEAD_DIM = 128
Q_PER_GROUP = 4
KV_PER_GROUP = 2
Q_REP = Q_PER_GROUP // KV_PER_GROUP
ATT_BLOCK = 128
ALIBI_MAX = 8.0
GROUP_WIDTH = Q_PER_GROUP * HEAD_DIM

MOE_GROUPS = 4
MOE_EPG = 8
MOE_EXPERTS = MOE_GROUPS * MOE_EPG
MOE_HIDDEN = 512
MOE_BM = 512
ROUTER_LANES = 128
EXP_LANE0 = MOE_GROUPS

LANES = 128
VMEM_LIMIT = 48 * 1024 * 1024
VMEM_LIMIT_DELTANET = 56 * 1024 * 1024


def _alibi_slopes():
    n = N_ATT_GROUPS * Q_PER_GROUP
    s = 2.0 ** (-ALIBI_MAX * np.arange(1, n + 1) / n)
    return s.reshape(N_ATT_GROUPS, KV_PER_GROUP, Q_REP)


def _rms(x, g):
    ms = jnp.mean(x * x, axis=-1, keepdims=True)
    return x * lax.rsqrt(ms + NORM_EPS) * g


def _dot(a, b):
    return jnp.dot(a.astype(BF16), b.astype(BF16), preferred_element_type=F32)


def _dot_nt(a, b):
    return lax.dot_general(a.astype(BF16), b.astype(BF16), (((1,), (1,)), ((), ())),
                           preferred_element_type=F32)


def _dot_tn(a, b):
    return lax.dot_general(a.astype(BF16), b.astype(BF16), (((0,), (0,)), ((), ())),
                           preferred_element_type=F32)


def _split2(x):
    hi = x.astype(BF16)
    lo = (x - hi.astype(F32)).astype(BF16)
    return hi, lo


def _split3(x):
    hi = x.astype(BF16)
    r = x - hi.astype(F32)
    mid = r.astype(BF16)
    lo = (r - mid.astype(F32)).astype(BF16)
    return hi, mid, lo


def _dot_exact01(x, sel):
    hi, mid, lo = _split3(x)
    d = lambda p: jnp.dot(p, sel, preferred_element_type=F32)
    return d(hi) + d(mid) + d(lo)


def _aligned(i, m):
    return i if isinstance(i, int) else pl.multiple_of(i, m)


def _sigmoid(x):
    return 1.0 / (1.0 + jnp.exp(-x))


def _silu(x):
    return x * _sigmoid(x)


ROW_TILE = 512


def _row_tiled_call(body, name, n, row_inputs, resident_inputs, out_widths, out_dtypes, extra_out_specs=(),
                    extra_out_shapes=()):
    tm = ROW_TILE
    row_spec = lambda width: pl.BlockSpec((tm, width), lambda i: (i, 0))
    whole = lambda a: pl.BlockSpec(a.shape, lambda i: (0,) * a.ndim)

    def in_row_spec(a):
        if isinstance(a, tuple):
            arr, layer = a
            return pl.BlockSpec((None, tm, arr.shape[2]), lambda i: (layer, i, 0))
        return row_spec(a.shape[1])

    row_specs = [in_row_spec(a) for a in row_inputs]
    row_inputs = [a[0] if isinstance(a, tuple) else a for a in row_inputs]
    return pl.pallas_call(
        body,
        grid=(n // tm,),
        in_specs=row_specs + [whole(a) for a in resident_inputs],
        out_specs=[row_spec(w) for w in out_widths] + list(extra_out_specs),
        out_shape=[jax.ShapeDtypeStruct((n, w), dt) for w, dt in zip(out_widths, out_dtypes)]
        + list(extra_out_shapes),
        compiler_params=pltpu.CompilerParams(dimension_semantics=("parallel",), vmem_limit_bytes=VMEM_LIMIT),
        name=name,
    )(*row_inputs, *resident_inputs)


def _nm_kernel(x_ref, g_ref, w_ref, o_ref):
    xn = _rms(x_ref[...], g_ref[...]).astype(BF16)
    o_ref[...] = jnp.dot(xn, w_ref[...], preferred_element_type=F32).astype(o_ref.dtype)


def norm_matmul(x, g, w, out_dtype):
    n, k = x.shape
    return _row_tiled_call(_nm_kernel, "norm_matmul", n, [x], [g.reshape(1, k), w], [w.shape[1]], [out_dtype])[0]


def _dn_inproj_kernel(x_ref, g_ref, w_ref, wgt_ref, o_ref, gt_ref):
    xn = _rms(x_ref[...], g_ref[...])
    xh, xl = _split2(xn)
    wh, wl = _split2(wgt_ref[...])
    gt_ref[...] = _dot_nt(wh, xh) + _dot_nt(wh, xl) + _dot_nt(wl, xh)
    o_ref[...] = jnp.dot(xh, w_ref[...], preferred_element_type=F32).astype(o_ref.dtype)


def dn_inproj(x, g, w_main, w_gates_t):
    n, k = x.shape
    ng = w_gates_t.shape[0]
    return _row_tiled_call(_dn_inproj_kernel, "dn_inproj", n, [x], [g.reshape(1, k), w_main, w_gates_t],
                           [w_main.shape[1]], [BF16],
                           extra_out_specs=[pl.BlockSpec((ng, ROW_TILE), lambda i: (0, i))],
                           extra_out_shapes=[jax.ShapeDtypeStruct((ng, n), F32)])


DN_PIECE = 256
DN_HALO = 8
DN_HB = 4
assert DN_HEADS % DN_HB == 0


def _deltanet_kernel(alog_ref, dtb_ref, q_ref, k_ref, v_ref, z_ref, cq_ref, ck_ref, cv_ref,
                     bpre_ref, apre_ref, onorm_ref, o_ref,
                     xf_ref, qs_ref, ks_ref, vs_ref, gcum_ref, betac_ref, gc_ref,
                     pm_ref, rq_ref, qq_ref, o0_ref, elast_ref):
    seq = q_ref.shape[1]
    c = DN_CHUNK
    n_chunks = seq // c
    assert 2 * n_chunks <= c
    head0 = pl.program_id(1) * DN_HB
    ki = lax.broadcasted_iota(jnp.int32, (c, c), 0)
    ji = lax.broadcasted_iota(jnp.int32, (c, c), 1)
    upper = jnp.where(ki <= ji, 1.0, 0.0).astype(BF16)
    causal = ki >= ji
    strict = ki > ji
    onorm = onorm_ref[...]

    xf_ref[0:DN_HALO, :] = jnp.zeros((DN_HALO, DN_DK), F32)

    def conv_silu(x_ref, w_ref, hb, finish, out_ref):
        cols = slice(hb * DN_DK, (hb + 1) * DN_DK)
        w = w_ref[:, cols]
        pieces = [slice(p * DN_PIECE, (p + 1) * DN_PIECE) for p in range(seq // DN_PIECE)]
        for rows in pieces:
            xf_ref[DN_HALO + rows.start:DN_HALO + rows.stop, :] = x_ref[0, rows, cols].astype(F32)
        for rows in pieces:
            acc = xf_ref[DN_HALO + rows.start:DN_HALO + rows.stop, :] * w[DN_CONV - 1:DN_CONV, :]
            for j in range(1, DN_CONV):
                acc = acc + xf_ref[DN_HALO + rows.start - j:DN_HALO + rows.stop - j, :] * w[DN_CONV - 1 - j:DN_CONV - j, :]
            out_ref[rows, :] = finish(_silu(acc))

    def l2n(scale):
        return lambda x: x * (lax.rsqrt(jnp.sum(x * x, axis=-1, keepdims=True) + NORM_EPS) * scale)

    def prologue(hb):
        conv_silu(q_ref, cq_ref, hb, l2n(DN_DK ** -0.5), qs_ref)
        conv_silu(k_ref, ck_ref, hb, l2n(1.0), ks_ref)
        conv_silu(v_ref, cv_ref, hb, lambda x: x, vs_ref)
        beta = _sigmoid(bpre_ref[hb, 0])
        a = apre_ref[hb, 0] + dtb_ref[head0 + hb]
        softplus = jnp.maximum(a, 0.0) + jnp.log(1.0 + jnp.exp(-jnp.abs(a)))
        g_log = -jnp.exp(jnp.full(a.shape, alog_ref[head0 + hb], F32)) * softplus
        gcum = _dot_exact01(g_log, upper)
        gcum_ref[...] = gcum
        t = jnp.concatenate([beta, gcum, jnp.zeros((c - 2 * n_chunks, c), F32)], axis=0).T
        for ci in range(n_chunks):
            betac_ref[ci * c:(ci + 1) * c, :] = jnp.broadcast_to(t[:, ci:ci + 1], (c, DN_DV))
            gc_ref[ci * c:(ci + 1) * c, :] = jnp.broadcast_to(t[:, n_chunks + ci:n_chunks + ci + 1], (c, DN_DV))

    def prepare(hb, cis):
        each = lambda f, *ls: [f(*xs) for xs in zip(*ls)]
        rows = [pl.ds(_aligned(ci * c, c), c) for ci in cis]
        qc = [qs_ref[r, :] for r in rows]
        kc = [ks_ref[r, :] for r in rows]
        vc = [vs_ref[r, :] for r in rows]
        beta_c = [betac_ref[r, :] for r in rows]
        g_c = [gc_ref[r, :] for r in rows]
        g_j = [jnp.broadcast_to(gcum_ref[pl.ds(ci, 1), :], (c, c)) for ci in cis]
        decay = each(lambda gi, gj: jnp.exp(jnp.where(causal, gi - gj, -jnp.inf)), g_c, g_j)
        kq = each(lambda k, q: _dot_nt(jnp.concatenate([k, q], axis=0), k), kc, qc)
        m = each(lambda b, x, d: jnp.where(strict, -(b * x[:c, :] * d), 0.0), beta_c, kq, decay)
        pw = each(lambda x: _dot(x, x), m)
        r = m
        for _ in range(DN_SQUARINGS - 1):
            xs = each(lambda p_, r_: _dot(p_, jnp.concatenate([p_, r_], axis=1)), pw, r)
            r = each(lambda r_, p_, x: r_ + p_ + x[:, c:], r, pw, xs)
            pw = [x[:, :c] for x in xs]
        xs = each(_dot, pw, r)
        r = each(lambda r_, p_, x: r_ + p_ + x, r, pw, xs)
        e_g = [jnp.exp(g) for g in g_c]
        rhs = each(lambda b, v, e, k: jnp.concatenate([b * v, b * e * k], axis=1), beta_c, vc, e_g, kc)
        sol = each(lambda rh, r_: rh + _dot(r_, rh), rhs, r)
        attn = each(lambda x, d: jnp.where(causal, x[c:, :] * d, 0.0), kq, decay)
        k_d = each(lambda k, g: k * jnp.exp(jnp.broadcast_to(g[c - 1:c, :], (c, DN_DV)) - g), kc, g_c)
        kt = each(_dot_tn, k_d, sol)
        at = each(_dot, attn, sol)
        for i, (ci, r_) in enumerate(zip(cis, rows)):
            qq_ref[hb, r_, :] = kt[i][:, :DN_DV]
            pm_ref[hb, r_, :] = kt[i][:, DN_DV:].astype(BF16)
            o0_ref[hb, r_, :] = at[i][:, :DN_DV]
            rq_ref[hb, r_, :] = (qc[i] * e_g[i] - at[i][:, DN_DV:]).astype(BF16)
            elast_ref[hb, pl.ds(_aligned(ci * 8, 8), 8), :] = jnp.exp(
                jnp.broadcast_to(g_c[i][c - 1:c, :], (8, DN_DV)))

    group = min(DN_GROUP, n_chunks)
    assert n_chunks % group == 0
    for hb in range(DN_HB):
        prologue(hb)
        if group == n_chunks:
            prepare(hb, list(range(n_chunks)))
        else:
            def prepare_group(gi, carry, hb=hb):
                prepare(hb, [gi * group + k for k in range(group)])
                return carry

            lax.fori_loop(0, n_chunks // group, prepare_group, 0)

    def chunk_step(ci, states):
        rows = pl.ds(pl.multiple_of(ci * c, c), c)
        xs = [_dot(jnp.concatenate([pm_ref[hb, rows, :], rq_ref[hb, rows, :]], axis=0), states[hb])
              for hb in range(DN_HB)]
        new_states = []
        for hb in range(DN_HB):
            cols = slice(hb * DN_DV, (hb + 1) * DN_DV)
            e_last = jnp.broadcast_to(elast_ref[hb, pl.ds(pl.multiple_of(ci * 8, 8), 1), :], (DN_DK, DN_DV))
            new_states.append(e_last * states[hb] - xs[hb][:c, :] + qq_ref[hb, rows, :])
            o = xs[hb][c:, :] + o0_ref[hb, rows, :]
            zc = z_ref[0, rows, cols].astype(F32)
            o_ref[0, rows, cols] = (_rms(o, onorm) * _silu(zc)).astype(o_ref.dtype)
        return tuple(new_states)

    lax.fori_loop(0, n_chunks, chunk_step, tuple(jnp.zeros((DN_DK, DN_DV), F32) for _ in range(DN_HB)))


def deltanet(proj, gates_t, conv_w, a_log, dt_bias, o_norm):
    b, s, _ = proj.shape
    ng = DN_HEADS // DN_HB
    nc = s // DN_CHUNK
    wide = DN_HB * DN_DK
    col = lambda off: pl.BlockSpec((1, s, wide), lambda bi, hi: (bi, 0, off + hi))
    cw = lambda off: pl.BlockSpec((DN_CONV, wide), lambda bi, hi: (0, off + hi))
    gate = lambda off: pl.BlockSpec((DN_HB, 1, nc, DN_CHUNK), lambda bi, hi: (off + hi, bi, 0, 0))
    smem = pl.BlockSpec(memory_space=pltpu.SMEM)
    per_head = lambda dt: pltpu.VMEM((DN_HB, s, DN_DV), dt)
    return pl.pallas_call(
        _deltanet_kernel,
        grid=(b, ng),
        in_specs=[smem, smem, col(0), col(ng), col(2 * ng), col(3 * ng), cw(0), cw(ng), cw(2 * ng),
                  gate(0), gate(ng), pl.BlockSpec((1, DN_DV), lambda bi, hi: (0, 0))],
        out_specs=pl.BlockSpec((1, s, wide), lambda bi, hi: (bi, 0, hi)),
        out_shape=jax.ShapeDtypeStruct((b, s, DN_V), BF16),
        scratch_shapes=[pltpu.VMEM((DN_HALO + s, DN_DK), F32),
                        pltpu.VMEM((s, DN_DK), F32), pltpu.VMEM((s, DN_DK), F32), pltpu.VMEM((s, DN_DV), F32),
                        pltpu.VMEM((nc, DN_CHUNK), F32), pltpu.VMEM((s, DN_DV), F32), pltpu.VMEM((s, DN_DV), F32),
                        per_head(BF16), per_head(BF16), per_head(F32), per_head(F32),
                        pltpu.VMEM((DN_HB, nc * 8, DN_DV), F32)],
        compiler_params=pltpu.CompilerParams(dimension_semantics=("parallel", "parallel"),
                                             vmem_limit_bytes=VMEM_LIMIT_DELTANET),
        name="deltanet",
    )(a_log, dt_bias, proj, proj, proj, proj, conv_w, conv_w, conv_w, gates_t, gates_t, o_norm.reshape(1, DN_DV))


ATT_PIECE = 256
ATT_M_INIT = -1e30


def _attention_kernel(q_ref, kv_ref, qn_ref, kn_ref, o_ref, qf_ref, kf_ref, vf_ref, acc_ref, m_ref, l_ref):
    seq = q_ref.shape[1]
    grp = pl.program_id(1)
    hd = HEAD_DIM
    blk = ATT_BLOCK
    slopes = _alibi_slopes()

    qg = qn_ref[0] * (hd ** -0.5)
    kg = kn_ref[0]

    def prep(pi, carry):
        r0 = pl.multiple_of(pi * ATT_PIECE, ATT_PIECE)
        rows = pl.ds(r0, ATT_PIECE)
        for j in range(Q_PER_GROUP):
            cols = slice(j * hd, (j + 1) * hd)
            qf_ref[j, rows, :] = _rms(q_ref[0, rows, cols].astype(F32), 1.0) * qg
        for j in range(KV_PER_GROUP):
            cols = slice(j * hd, (j + 1) * hd)
            kf_ref[j, rows, :] = _rms(kv_ref[0, rows, cols].astype(F32), 1.0) * kg
            vcols = slice((KV_PER_GROUP + j) * hd, (KV_PER_GROUP + j + 1) * hd)
            vf_ref[j, rows, :] = kv_ref[0, rows, vcols].astype(F32)
        return carry

    lax.fori_loop(0, seq // ATT_PIECE, prep, 0)

    @pl.when(grp == 0)
    def _():
        def init(pi, carry):
            r0 = pl.multiple_of(pi * ATT_PIECE, ATT_PIECE)
            rows = pl.ds(r0, ATT_PIECE)
            for j in range(Q_PER_GROUP):
                acc_ref[j, rows, :] = jnp.zeros((ATT_PIECE, hd), F32)
                l_ref[j, rows, :] = jnp.zeros((ATT_PIECE, hd), F32)
                m_ref[j, rows, :] = jnp.full((ATT_PIECE, hd), ATT_M_INIT, F32)
            return carry

        lax.fori_loop(0, seq // ATT_PIECE, init, 0)

    def rows_of(start, size, dil):
        return pl.ds(start, size) if dil == 1 else pl.ds(start, size, stride=dil)

    def attend(g, dil, q_start, k_start, nk):
        qrows = rows_of(q_start, blk, dil)
        krows = rows_of(k_start, nk, dil)
        qi = lax.broadcasted_iota(jnp.int32, (blk, nk), 0)
        kidx = lax.broadcasted_iota(jnp.int32, (blk, nk), 1)
        dist = (nk - blk) + qi - kidx
        valid = (dist >= 0) & (dist <= blk)
        distf = dist.astype(F32)
        kv_heads = range(KV_PER_GROUP)
        heads = [(kvh, rep) for kvh in kv_heads for rep in range(Q_REP)]
        vb = [vf_ref[kvh, krows, :].astype(BF16) for kvh in kv_heads]
        sc2 = [_dot_nt(jnp.concatenate([qf_ref[kvh * Q_REP + rep, qrows, :] for rep in range(Q_REP)], axis=0),
                       kf_ref[kvh, krows, :]) for kvh in kv_heads]
        sc = [jnp.where(valid, sc2[kvh][rep * blk:(rep + 1) * blk, :] - float(slopes[g, kvh, rep] * dil) * distf,
                        -jnp.inf) for kvh, rep in heads]
        m_old = [m_ref[kvh * Q_REP + rep, qrows, :] for kvh, rep in heads]
        m_new = [jnp.maximum(mo, jnp.max(s, axis=-1, keepdims=True)) for mo, s in zip(m_old, sc)]
        alpha = [jnp.exp(mo - mn) for mo, mn in zip(m_old, m_new)]
        p = [jnp.exp(s - mn[:, 0:1]) for s, mn in zip(sc, m_new)]
        pv2 = [jnp.dot(jnp.concatenate([p[kvh * Q_REP + rep].astype(BF16) for rep in range(Q_REP)], axis=0),
                       vb[kvh], preferred_element_type=F32) for kvh in kv_heads]
        for i, (kvh, rep) in enumerate(heads):
            j = kvh * Q_REP + rep
            l_ref[j, qrows, :] = alpha[i] * l_ref[j, qrows, :] + jnp.sum(p[i], axis=-1, keepdims=True)
            acc_ref[j, qrows, :] = alpha[i] * acc_ref[j, qrows, :] + pv2[kvh][rep * blk:(rep + 1) * blk, :]
            m_ref[j, qrows, :] = m_new[i]

    for g, (window, dil) in enumerate(DIL_CONFIGS):
        sub_len = seq // dil
        nblk = sub_len // blk

        @pl.when(grp == g)
        def _(g=g, dil=dil, nblk=nblk):
            def residue(res, carry):
                attend(g, dil, res, res, blk)
                if nblk > 1:
                    def later(n, c2):
                        attend(g, dil, res + n * blk * dil, res + (n - 1) * blk * dil, 2 * blk)
                        return c2
                    lax.fori_loop(1, nblk, later, 0)
                return carry

            lax.fori_loop(0, dil, residue, 0)

    @pl.when(grp == N_ATT_GROUPS - 1)
    def _():
        def finish(pi, carry):
            r0 = pl.multiple_of(pi * ATT_PIECE, ATT_PIECE)
            rows = pl.ds(r0, ATT_PIECE)
            for j in range(Q_PER_GROUP):
                cols = slice(j * hd, (j + 1) * hd)
                o_ref[0, rows, cols] = (acc_ref[j, rows, :] / l_ref[j, rows, :]).astype(o_ref.dtype)
            return carry

        lax.fori_loop(0, seq // ATT_PIECE, finish, 0)


def dilated_attention(q, kv, q_norm, k_norm):
    b, s, _ = q.shape
    gw = GROUP_WIDTH
    return pl.pallas_call(
        _attention_kernel,
        grid=(b, N_ATT_GROUPS),
        in_specs=[pl.BlockSpec((1, s, gw), lambda bi, gi: (bi, 0, gi)),
                  pl.BlockSpec((1, s, gw), lambda bi, gi: (bi, 0, gi)),
                  pl.BlockSpec((1, 1, HEAD_DIM), lambda bi, gi: (gi, 0, 0)),
                  pl.BlockSpec((1, 1, HEAD_DIM), lambda bi, gi: (gi, 0, 0))],
        out_specs=pl.BlockSpec((1, s, gw), lambda bi, gi: (bi, 0, 0)),
        out_shape=jax.ShapeDtypeStruct((b, s, gw), BF16),
        scratch_shapes=[pltpu.VMEM((Q_PER_GROUP, s, HEAD_DIM), F32),
                        pltpu.VMEM((KV_PER_GROUP, s, HEAD_DIM), F32),
                        pltpu.VMEM((KV_PER_GROUP, s, HEAD_DIM), F32),
                        pltpu.VMEM((Q_PER_GROUP, s, HEAD_DIM), F32),
                        pltpu.VMEM((Q_PER_GROUP, s, HEAD_DIM), F32),
                        pltpu.VMEM((Q_PER_GROUP, s, HEAD_DIM), F32)],
        compiler_params=pltpu.CompilerParams(dimension_semantics=("parallel", "arbitrary"),
                                             vmem_limit_bytes=VMEM_LIMIT),
        name="dilated_attention",
    )(q, kv, q_norm.reshape(N_ATT_GROUPS, 1, HEAD_DIM), k_norm.reshape(N_ATT_GROUPS, 1, HEAD_DIM))


ROUTER_TM = 512
SEG_ALIGN = 8
SORT_ROWS = 2 * ROUTER_TM + 256
assert SORT_ROWS >= 2 * ROUTER_TM + MOE_EXPERTS * (SEG_ALIGN - 1) and SORT_ROWS % LANES == 0
META_W0, META_W1, META_P0, META_P1 = 0, 1, 2, 3
TAB_CNT, TAB_OFF, TAB_SEG = 0, 1, 2


def _router_kernel(a_ref, res_ref, wo_ref, g_ref, w_ref, b_ref, h_ref, xn_ref, meta_ref, post_ref, tab_ref, cnt_ref,
                   carry_ref):
    tm = res_ref.shape[0]

    @pl.when(pl.program_id(0) == 0)
    def _():
        carry_ref[...] = jnp.zeros_like(carry_ref)

    h = res_ref[...] + jnp.dot(a_ref[...], wo_ref[...], preferred_element_type=F32)
    h_ref[...] = h
    xn = _rms(h, g_ref[...])
    xn_ref[...] = xn.astype(BF16)
    xh, xl = _split2(xn)
    wh, wl = _split2(w_ref[...])
    d = lambda a, bb: jnp.dot(a, bb, preferred_element_type=F32)
    logits = d(xh, wh) + d(xh, wl) + d(xl, wh) + b_ref[...]

    lane = lax.broadcasted_iota(jnp.int32, (tm, ROUTER_LANES), 1)
    big = jnp.int32(ROUTER_LANES)
    first_where = lambda cond: jnp.min(jnp.where(cond, lane, big), axis=-1, keepdims=True)

    gl = jnp.where(lane < MOE_GROUPS, logits, -jnp.inf)
    ge = jnp.exp(gl - jnp.max(gl, axis=-1, keepdims=True))
    gp = ge / jnp.sum(ge, axis=-1, keepdims=True)
    g_w = jnp.max(gp, axis=-1, keepdims=True)
    g_idx = first_where(gp == g_w)

    lo = EXP_LANE0 + g_idx * MOE_EPG
    in_group = (lane >= lo) & (lane < lo + MOE_EPG)
    el = jnp.where(in_group, logits, -jnp.inf)
    ee = jnp.exp(el - jnp.max(el, axis=-1, keepdims=True))
    ep = ee / jnp.sum(ee, axis=-1, keepdims=True)
    p0 = jnp.max(jnp.where(in_group, ep, -1.0), axis=-1, keepdims=True)
    i0 = first_where(in_group & (ep == p0))
    rest = in_group & (lane != i0)
    p1 = jnp.max(jnp.where(rest, ep, -1.0), axis=-1, keepdims=True)
    i1 = first_where(rest & (ep == p1))
    w0 = g_w * p0 / (p0 + p1)
    w1 = g_w * p1 / (p0 + p1)

    oh0 = jnp.where(lane == i0, 1.0, 0.0)
    oh1 = jnp.where(lane == i1, 1.0, 0.0)
    both = oh0 + oh1
    ti = lax.broadcasted_iota(jnp.int32, (tm, tm), 0)
    tj = lax.broadcasted_iota(jnp.int32, (tm, tm), 1)
    before = jnp.where(tj < ti, 1.0, 0.0).astype(BF16)
    within = jnp.dot(before, both.astype(BF16), preferred_element_type=F32)
    cnt = jnp.sum(both, axis=0, keepdims=True)
    cnt_pad = jnp.floor((cnt + (SEG_ALIGN - 1)) * (1.0 / SEG_ALIGN)) * SEG_ALIGN
    li = lax.broadcasted_iota(jnp.int32, (ROUTER_LANES, ROUTER_LANES), 0)
    lj = lax.broadcasted_iota(jnp.int32, (ROUTER_LANES, ROUTER_LANES), 1)
    earlier = jnp.where(li < lj, 1.0, 0.0).astype(BF16)
    tile_off = _dot_exact01(jnp.broadcast_to(cnt_pad, (8, ROUTER_LANES)), earlier)[0:1, :]
    row = tile_off + within
    pos0 = jnp.sum(row * oh0, axis=-1, keepdims=True)
    pos1 = jnp.sum(row * oh1, axis=-1, keepdims=True)
    seg_off = carry_ref[...]
    total = seg_off + cnt_pad
    carry_ref[...] = total
    cnt_ref[...] = jnp.broadcast_to(total, cnt_ref.shape)

    sub = lax.broadcasted_iota(jnp.int32, (8, ROUTER_LANES), 0)
    tab_ref[...] = jnp.where(sub == TAB_CNT, cnt_pad, jnp.where(sub == TAB_OFF, tile_off,
                                                                jnp.where(sub == TAB_SEG, seg_off, 0.0)))
    meta = jnp.zeros((tm, ROUTER_LANES), F32)
    for idx, val in ((META_W0, w0), (META_W1, w1), (META_P0, pos0), (META_P1, pos1)):
        meta = jnp.where(lane == idx, val, meta)
    meta_ref[...] = meta
    post_ref[0] = meta.T[0:8, :]


def outproj_router(a, w_out, res, g, w_router, b_router):
    n, k = res.shape
    ka = a.shape[1]
    tm = ROUTER_TM
    nt = n // tm
    return pl.pallas_call(
        _router_kernel,
        grid=(nt,),
        in_specs=[pl.BlockSpec((tm, ka), lambda i: (i, 0)),
                  pl.BlockSpec((tm, k), lambda i: (i, 0)),
                  pl.BlockSpec((ka, k), lambda i: (0, 0)),
                  pl.BlockSpec((1, k), lambda i: (0, 0)),
                  pl.BlockSpec((k, ROUTER_LANES), lambda i: (0, 0)),
                  pl.BlockSpec((1, ROUTER_LANES), lambda i: (0, 0))],
        out_specs=[pl.BlockSpec((tm, k), lambda i: (i, 0)),
                   pl.BlockSpec((tm, k), lambda i: (i, 0)),
                   pl.BlockSpec((tm, ROUTER_LANES), lambda i: (i, 0)),
                   pl.BlockSpec((1, 8, tm), lambda i: (i, 0, 0)),
                   pl.BlockSpec((8, ROUTER_LANES), lambda i: (i, 0)),
                   pl.BlockSpec((8, ROUTER_LANES), lambda i: (0, 0))],
        out_shape=[jax.ShapeDtypeStruct((n, k), F32),
                   jax.ShapeDtypeStruct((n, k), BF16),
                   jax.ShapeDtypeStruct((n, ROUTER_LANES), F32),
                   jax.ShapeDtypeStruct((nt, 8, tm), F32),
                   jax.ShapeDtypeStruct((nt * 8, ROUTER_LANES), F32),
                   jax.ShapeDtypeStruct((8, ROUTER_LANES), F32)],
        scratch_shapes=[pltpu.VMEM((1, ROUTER_LANES), F32)],
        compiler_params=pltpu.CompilerParams(dimension_semantics=("arbitrary",),
                                             vmem_limit_bytes=VMEM_LIMIT),
        name="outproj_router",
    )(a, res, w_out, g.reshape(1, k), w_router, b_router)


def _pack_halves(x):
    k = x.shape[1] // 2
    lo = pltpu.bitcast(x[:, :k].astype(BF16).astype(F32), jnp.uint32)
    hi = pltpu.bitcast(x[:, k:].astype(BF16).astype(F32), jnp.uint32)
    return (hi & jnp.uint32(0xFFFF0000)) | (lo >> 16)


def _unpack_halves(w):
    lo = pltpu.bitcast(w << 16, F32)
    hi = pltpu.bitcast(w & jnp.uint32(0xFFFF0000), F32)
    return lo.astype(BF16), hi.astype(BF16)


SEG_PIECE = 2 * SEG_ALIGN


def _segment_copies(rows_of, src_ref, dst_ref, sem, wait):
    def piece(s, d, rows):
        cp = pltpu.make_async_copy(src_ref.at[pl.ds(pl.multiple_of(s, SEG_ALIGN), rows), :],
                                   dst_ref.at[pl.ds(pl.multiple_of(d, SEG_ALIGN), rows), :], sem)
        cp.wait() if wait else cp.start()

    def per_expert(e, carry):
        cnt, s0, d0 = rows_of(e)
        n_full = cnt // SEG_PIECE

        def full_piece(j, c2):
            piece(s0 + j * SEG_PIECE, d0 + j * SEG_PIECE, SEG_PIECE)
            return c2

        lax.fori_loop(0, n_full, full_piece, 0)

        @pl.when(cnt % SEG_PIECE != 0)
        def _():
            piece(s0 + n_full * SEG_PIECE, d0 + n_full * SEG_PIECE, SEG_ALIGN)

        return carry

    lax.fori_loop(0, MOE_EXPERTS, per_expert, 0)


def _tile_segments(cnt_ref, tile, src_off_ref, dst_off_ref):
    def rows_of(e):
        k = tile * MOE_EXPERTS + e
        return cnt_ref[k], src_off_ref[k], dst_off_ref[k]
    return rows_of


def _dispatch_kernel(cnt_ref, off_ref, seg_ref, tail_cnt_ref, tail_dst_ref, nvalid_ref, post_ref, x_ref, out_hbm,
                     xs_ref, zero_ref, sems):
    tile = pl.program_id(0)
    tm = x_ref.shape[0]
    bm = zero_ref.shape[0]
    sem = sems.at[2]
    n_blocks = out_hbm.shape[0] // bm

    @pl.when(tile == 0)
    def _():
        zero_ref[...] = jnp.zeros_like(zero_ref)
        tails = lambda e: (tail_cnt_ref[e], 0, tail_dst_ref[e])

        def unused_block(wait):
            def body(blk, carry):
                cp = pltpu.make_async_copy(zero_ref, out_hbm.at[pl.ds(pl.multiple_of(blk * bm, bm), bm), :], sem)
                cp.wait() if wait else cp.start()
                return carry
            return body

        _segment_copies(tails, zero_ref, out_hbm, sem, wait=False)
        lax.fori_loop(nvalid_ref[0], n_blocks, unused_block(False), 0)
        _segment_copies(tails, zero_ref, out_hbm, sem, wait=True)
        lax.fori_loop(nvalid_ref[0], n_blocks, unused_block(True), 0)

    n_tiles = pl.num_programs(0)
    slot = tile % 2

    def copies(t, s, wait):
        _segment_copies(_tile_segments(cnt_ref, t, off_ref, seg_ref), xs_ref.at[s], out_hbm, sems.at[s], wait)

    @pl.when(tile >= 2)
    def _():
        copies(tile - 2, slot, True)

    post = post_ref[0]
    p0 = post[META_P0:META_P0 + 1, :].astype(jnp.int32)
    p1 = post[META_P1:META_P1 + 1, :].astype(jnp.int32)
    r = lax.broadcasted_iota(jnp.int32, (SORT_ROWS, tm), 0)
    sel = jnp.where(r == p0, 1.0, jnp.where(r == p1, 1.0, 0.0)).astype(BF16)
    xs_ref[slot] = _pack_halves(jnp.dot(sel, x_ref[...], preferred_element_type=F32))
    copies(tile, slot, False)

    @pl.when(tile == n_tiles - 1)
    def _():
        @pl.when(tile >= 1)
        def _():
            copies(tile - 1, 1 - slot, True)

        copies(tile, slot, True)


def moe_dispatch(tab_cnt, tab_off, tab_seg, tail_cnt, tail_dst, n_valid, post, xn, rows):
    n, d = xn.shape
    tm = ROUTER_TM
    grid_spec = pltpu.PrefetchScalarGridSpec(
        num_scalar_prefetch=6,
        grid=(n // tm,),
        in_specs=[pl.BlockSpec((1, 8, tm), lambda i, *_: (i, 0, 0)),
                  pl.BlockSpec((tm, d), lambda i, *_: (i, 0))],
        out_specs=pl.BlockSpec(memory_space=pl.ANY),
        scratch_shapes=[pltpu.VMEM((2, SORT_ROWS, d // 2), jnp.uint32), pltpu.VMEM((MOE_BM, d // 2), jnp.uint32),
                        pltpu.SemaphoreType.DMA((3,))],
    )
    return pl.pallas_call(
        _dispatch_kernel,
        grid_spec=grid_spec,
        out_shape=jax.ShapeDtypeStruct((rows, d // 2), jnp.uint32),
        compiler_params=pltpu.CompilerParams(dimension_semantics=("arbitrary",),
                                             vmem_limit_bytes=VMEM_LIMIT),
        name="moe_dispatch",
    )(tab_cnt, tab_off, tab_seg, tail_cnt, tail_dst, n_valid, post, xn)


def _expert_kernel(be_ref, nvalid_ref, x_ref, w1_ref, w3_ref, w2_ref, y_ref, w1b_ref, w3b_ref, w2b_ref):
    i = pl.program_id(0)

    @pl.when(i < nvalid_ref[0])
    def _():
        @pl.when(jnp.logical_or(i == 0, be_ref[i] != be_ref[jnp.maximum(i - 1, 0)]))
        def _():
            w1b_ref[...] = w1_ref[0, 0].astype(BF16)
            w3b_ref[...] = w3_ref[0, 0].astype(BF16)
            w2b_ref[...] = w2_ref[0, 0].astype(BF16)

        xlo, xhi = _unpack_halves(x_ref[...])
        half = xlo.shape[1]
        up = lambda w_ref: (jnp.dot(xlo, w_ref[:half, :], preferred_element_type=F32)
                            + jnp.dot(xhi, w_ref[half:, :], preferred_element_type=F32))
        hdn = (_silu(up(w1b_ref)) * up(w3b_ref)).astype(BF16)
        y_ref[...] = _pack_halves(jnp.dot(hdn, w2b_ref[...], preferred_element_type=F32))

    @pl.when(i >= nvalid_ref[0])
    def _():
        y_ref[...] = jnp.zeros_like(y_ref)


def moe_experts(x_sorted, block_e, n_valid, w1, w3, w2, layer):
    rows, half = x_sorted.shape
    n_blocks = block_e.shape[0]
    bm = MOE_BM
    d, hid = w1.shape[2], w1.shape[3]
    grid_spec = pltpu.PrefetchScalarGridSpec(
        num_scalar_prefetch=2,
        grid=(n_blocks,),
        in_specs=[pl.BlockSpec((bm, half), lambda i, be, nv: (jnp.minimum(i, nv[0] - 1), 0)),
                  pl.BlockSpec((1, 1, d, hid), lambda i, be, nv: (layer, be[i], 0, 0)),
                  pl.BlockSpec((1, 1, d, hid), lambda i, be, nv: (layer, be[i], 0, 0)),
                  pl.BlockSpec((1, 1, hid, d), lambda i, be, nv: (layer, be[i], 0, 0))],
        out_specs=pl.BlockSpec((bm, half), lambda i, be, nv: (i, 0)),
        scratch_shapes=[pltpu.VMEM((d, hid), BF16), pltpu.VMEM((d, hid), BF16), pltpu.VMEM((hid, d), BF16)],
    )
    return pl.pallas_call(
        _expert_kernel,
        grid_spec=grid_spec,
        out_shape=jax.ShapeDtypeStruct((rows, half), jnp.uint32),
        compiler_params=pltpu.CompilerParams(dimension_semantics=("arbitrary",),
                                             vmem_limit_bytes=VMEM_LIMIT),
        name="moe_experts",
    )(block_e, n_valid, x_sorted, w1, w3, w2)


def _combine_ple_kernel(cnt_ref, off_ref, seg_ref, y_hbm, h_ref, meta_ref, p_ref, g_ref, wg_ref, wp_ref, o_ref,
                        ys_ref, sems):
    tile = pl.program_id(0)
    n_tiles = pl.num_programs(0)
    tm, d = h_ref.shape
    slot = tile % 2

    def copies(t, s, wait):
        _segment_copies(_tile_segments(cnt_ref, t, seg_ref, off_ref), y_hbm, ys_ref.at[s], sems.at[s], wait)

    @pl.when(tile == 0)
    def _():
        ys_ref[...] = jnp.zeros_like(ys_ref)
        copies(0, 0, False)

    @pl.when(tile + 1 < n_tiles)
    def _():
        copies(tile + 1, 1 - slot, False)

    copies(tile, slot, True)
    meta = meta_ref[...]
    w0 = meta[:, META_W0:META_W0 + 1]
    w1 = meta[:, META_W1:META_W1 + 1]
    p0 = meta[:, META_P0:META_P0 + 1].astype(jnp.int32)
    p1 = meta[:, META_P1:META_P1 + 1].astype(jnp.int32)
    r = lax.broadcasted_iota(jnp.int32, (tm, SORT_ROWS), 1)
    wmat = jnp.where(r == p0, w0, jnp.where(r == p1, w1, 0.0)).astype(BF16)
    ylo, yhi = _unpack_halves(ys_ref[slot])
    mix = lambda y: jnp.dot(wmat, y, preferred_element_type=F32)
    half = d // 2
    h = jnp.concatenate([h_ref[:, :half] + mix(ylo), h_ref[:, half:] + mix(yhi)], axis=1)
    gate = _sigmoid(jnp.dot(_rms(h, g_ref[...]).astype(BF16), wg_ref[...], preferred_element_type=F32))
    proj = jnp.dot(p_ref[...].astype(BF16), wp_ref[...], preferred_element_type=F32)
    o_ref[...] = h + gate * proj


def moe_combine_ple(tab_cnt, tab_off, tab_seg, y, h, meta, p, layer, g, w_gate, w_proj):
    n, d = h.shape
    pd = p.shape[2]
    tm = ROUTER_TM
    grid_spec = pltpu.PrefetchScalarGridSpec(
        num_scalar_prefetch=3,
        grid=(n // tm,),
        in_specs=[pl.BlockSpec(memory_space=pl.ANY),
                  pl.BlockSpec((tm, d), lambda i, *_: (i, 0)),
                  pl.BlockSpec((tm, ROUTER_LANES), lambda i, *_: (i, 0)),
                  pl.BlockSpec((None, tm, pd), lambda i, *_: (layer, i, 0)),
                  pl.BlockSpec((1, d), lambda i, *_: (0, 0)),
                  pl.BlockSpec((d, d), lambda i, *_: (0, 0)),
                  pl.BlockSpec((pd, d), lambda i, *_: (0, 0))],
        out_specs=pl.BlockSpec((tm, d), lambda i, *_: (i, 0)),
        scratch_shapes=[pltpu.VMEM((2, SORT_ROWS, d // 2), jnp.uint32), pltpu.SemaphoreType.DMA((2,))],
    )
    return pl.pallas_call(
        _combine_ple_kernel,
        grid_spec=grid_spec,
        out_shape=jax.ShapeDtypeStruct((n, d), F32),
        compiler_params=pltpu.CompilerParams(dimension_semantics=("arbitrary",),
                                             vmem_limit_bytes=VMEM_LIMIT),
        name="moe_combine_ple",
    )(tab_cnt, tab_off, tab_seg, y, h, meta, p, g.reshape(1, d), w_gate, w_proj)


def outproj_moe_embedding(a, w_out, res, ffn_norm, w_rg, b_rg, w_re, b_re, w1, w3, w2, p, ple_norm, w_ple_gate,
                          w_ple_proj, layer):
    n, d = res.shape
    pad = ROUTER_LANES - MOE_GROUPS - MOE_EXPERTS
    w_router = jnp.concatenate([w_rg, w_re, jnp.zeros((d, pad), F32)], axis=1)
    b_router = jnp.concatenate([b_rg, b_re, jnp.zeros((pad,), F32)]).reshape(1, ROUTER_LANES)
    h, xn, meta, post, tabs, cnt = outproj_router(a, w_out, res, ffn_norm, w_router, b_router)

    bm = MOE_BM
    nt = n // ROUTER_TM
    lanes = slice(EXP_LANE0, EXP_LANE0 + MOE_EXPERTS)
    totals = cnt[0, lanes].astype(jnp.int32)
    region = (totals + bm - 1) // bm * bm
    region_end = jnp.cumsum(region)
    region_start = region_end - region
    n_blocks = -(-(2 * n + nt * MOE_EXPERTS * (SEG_ALIGN - 1)) // bm) + MOE_EXPERTS
    block_row0 = jnp.arange(n_blocks, dtype=jnp.int32) * bm
    block_e = jnp.minimum(jnp.sum((block_row0[:, None] >= region_end[None, :]).astype(jnp.int32), axis=1),
                          MOE_EXPERTS - 1).astype(jnp.int32)
    n_valid = (region_end[-1:] // bm).astype(jnp.int32)
    tabs = tabs.reshape(nt, 8, ROUTER_LANES)[:, :, lanes].astype(jnp.int32)
    tab_cnt = tabs[:, TAB_CNT].reshape(-1)
    tab_off = tabs[:, TAB_OFF].reshape(-1)
    tab_seg = (tabs[:, TAB_SEG] + region_start[None, :]).reshape(-1)

    x_sorted = moe_dispatch(tab_cnt, tab_off, tab_seg, region - totals, region_start + totals, n_valid, post, xn,
                            n_blocks * bm)
    y = moe_experts(x_sorted, block_e, n_valid, w1, w3, w2, layer)
    return moe_combine_ple(tab_cnt, tab_off, tab_seg, y, h, meta, p, layer, ple_norm, w_ple_gate, w_ple_proj)


def kernel(x, p, a_norm, a_w_in, a_conv, a_A_log, a_dt_bias, a_o_norm, a_w_out, kv_norm, w_kv, k_norm, b_norm, b_w_q, b_q_norm, b_w_out, ffn_norm, w_router_group, b_router_group, w_router_expert, b_router_expert, w1, w3, w2, ple_norm, w_ple_gate, w_ple_proj):
    b, s, d = x.shape
    n = b * s
    depth = p.shape[0]
    n_a = a_norm.shape[0]
    h = x.reshape(n, d)
    p_rows = p.reshape(depth, n, -1)
    kv = None
    for i in range(depth):
        if i < n_a:
            w_in = a_w_in[i]
            proj, gates_t = dn_inproj(h, a_norm[i], w_in[:, :DN_MAIN].astype(BF16), w_in[:, DN_MAIN:].T)
            o = deltanet(proj.reshape(b, s, DN_MAIN), gates_t.reshape(2 * DN_HEADS, b, s // DN_CHUNK, DN_CHUNK),
                         a_conv[i], a_A_log[i], a_dt_bias[i], a_o_norm[i])
            mixed, w_out = o.reshape(n, DN_V), a_w_out[i]
        else:
            bl = i - n_a
            q = norm_matmul(h, b_norm[bl], b_w_q[bl].astype(BF16), BF16)
            o = dilated_attention(q.reshape(b, s, -1), kv.reshape(b, s, -1), b_q_norm[bl], k_norm)
            mixed, w_out = o.reshape(n, GROUP_WIDTH), b_w_out[bl]
        h = outproj_moe_embedding(mixed, w_out.astype(BF16), h, ffn_norm[i], w_router_group[i], b_router_group[i],
                                  w_router_expert[i], b_router_expert[i], w1, w3, w2, p_rows, ple_norm[i],
                                  w_ple_gate[i].astype(BF16), w_ple_proj[i].astype(BF16), i)
        if i == n_a - 1:
            kv = norm_matmul(h, kv_norm, w_kv.astype(BF16), BF16)
    return h.reshape(b, s, d)
```

```python
import numpy as np
import jax
import jax.numpy as jnp
from jax import lax
from jax.experimental import pallas as pl
from jax.experimental.pallas import tpu as pltpu

F32 = jnp.float32
BF16 = jnp.bfloat16

NORM_EPS = 1e-6

DN_HEADS = 8
DN_DK = 128
DN_DV = 128
DN_CONV = 4
DN_CHUNK = 128
DN_SQUARINGS = DN_CHUNK.bit_length() - 2
DN_GROUP = 16
assert DN_CHUNK == DN_DK == DN_DV
DN_QK = DN_HEADS * DN_DK
DN_V = DN_HEADS * DN_DV
DN_MAIN = 2 * DN_QK + 2 * DN_V

DIL_CONFIGS = ((128, 1), (512, 4), (2048, 16))
N_ATT_GROUPS = len(DIL_CONFIGS)
HEAD_DIM = 128
Q_PER_GROUP = 4
KV_PER_GROUP = 2
Q_REP = Q_PER_GROUP // KV_PER_GROUP
ATT_BLOCK = 128
ALIBI_MAX = 8.0
GROUP_WIDTH = Q_PER_GROUP * HEAD_DIM

MOE_GROUPS = 4
MOE_EPG = 8
MOE_EXPERTS = MOE_GROUPS * MOE_EPG
MOE_HIDDEN = 512
MOE_BM = 512
ROUTER_LANES = 128
EXP_LANE0 = MOE_GROUPS

LANES = 128
VMEM_LIMIT = 48 * 1024 * 1024
VMEM_LIMIT_DELTANET = 56 * 1024 * 1024


def _alibi_slopes():
    n = N_ATT_GROUPS * Q_PER_GROUP
    s = 2.0 ** (-ALIBI_MAX * np.arange(1, n + 1) / n)
    return s.reshape(N_ATT_GROUPS, KV_PER_GROUP, Q_REP)


def _rms(x, g):
    ms = jnp.mean(x * x, axis=-1, keepdims=True)
    return x * lax.rsqrt(ms + NORM_EPS) * g


def _dot(a, b):
    return jnp.dot(a.astype(BF16), b.astype(BF16), preferred_element_type=F32)


def _dot_nt(a, b):
    return lax.dot_general(a.astype(BF16), b.astype(BF16), (((1,), (1,)), ((), ())),
                           preferred_element_type=F32)


def _dot_tn(a, b):
    return lax.dot_general(a.astype(BF16), b.astype(BF16), (((0,), (0,)), ((), ())),
                           preferred_element_type=F32)


def _split2(x):
    hi = x.astype(BF16)
    lo = (x - hi.astype(F32)).astype(BF16)
    return hi, lo


def _split3(x):
    hi = x.astype(BF16)
    r = x - hi.astype(F32)
    mid = r.astype(BF16)
    lo = (r - mid.astype(F32)).astype(BF16)
    return hi, mid, lo


def _dot_exact01(x, sel):
    hi, mid, lo = _split3(x)
    d = lambda p: jnp.dot(p, sel, preferred_element_type=F32)
    return d(hi) + d(mid) + d(lo)


def _aligned(i, m):
    return i if isinstance(i, int) else pl.multiple_of(i, m)


def _sigmoid(x):
    return 1.0 / (1.0 + jnp.exp(-x))


def _silu(x):
    return x * _sigmoid(x)


ROW_TILE = 512


def _row_tiled_call(body, name, n, row_inputs, resident_inputs, out_widths, out_dtypes, extra_out_specs=(),
                    extra_out_shapes=()):
    tm = ROW_TILE
    row_spec = lambda width: pl.BlockSpec((tm, width), lambda i: (i, 0))
    whole = lambda a: pl.BlockSpec(a.shape, lambda i: (0,) * a.ndim)

    def in_row_spec(a):
        if isinstance(a, tuple):
            arr, layer = a
            return pl.BlockSpec((None, tm, arr.shape[2]), lambda i: (layer, i, 0))
        return row_spec(a.shape[1])

    row_specs = [in_row_spec(a) for a in row_inputs]
    row_inputs = [a[0] if isinstance(a, tuple) else a for a in row_inputs]
    return pl.pallas_call(
        body,
        grid=(n // tm,),
        in_specs=row_specs + [whole(a) for a in resident_inputs],
        out_specs=[row_spec(w) for w in out_widths] + list(extra_out_specs),
        out_shape=[jax.ShapeDtypeStruct((n, w), dt) for w, dt in zip(out_widths, out_dtypes)]
        + list(extra_out_shapes),
        compiler_params=pltpu.CompilerParams(dimension_semantics=("parallel",), vmem_limit_bytes=VMEM_LIMIT),
        name=name,
    )(*row_inputs, *resident_inputs)


def _nm_kernel(x_ref, g_ref, w_ref, o_ref):
    xn = _rms(x_ref[...], g_ref[...]).astype(BF16)
    o_ref[...] = jnp.dot(xn, w_ref[...], preferred_element_type=F32).astype(o_ref.dtype)


def norm_matmul(x, g, w, out_dtype):
    n, k = x.shape
    return _row_tiled_call(_nm_kernel, "norm_matmul", n, [x], [g.reshape(1, k), w], [w.shape[1]], [out_dtype])[0]


def _dn_inproj_kernel(x_ref, g_ref, w_ref, wgt_ref, o_ref, gt_ref):
    xn = _rms(x_ref[...], g_ref[...])
    xh, xl = _split2(xn)
    wh, wl = _split2(wgt_ref[...])
    gt_ref[...] = _dot_nt(wh, xh) + _dot_nt(wh, xl) + _dot_nt(wl, xh)
    o_ref[...] = jnp.dot(xh, w_ref[...], preferred_element_type=F32).astype(o_ref.dtype)


def dn_inproj(x, g, w_main, w_gates_t):
    n, k = x.shape
    ng = w_gates_t.shape[0]
    return _row_tiled_call(_dn_inproj_kernel, "dn_inproj", n, [x], [g.reshape(1, k), w_main, w_gates_t],
                           [w_main.shape[1]], [BF16],
                           extra_out_specs=[pl.BlockSpec((ng, ROW_TILE), lambda i: (0, i))],
                           extra_out_shapes=[jax.ShapeDtypeStruct((ng, n), F32)])


DN_PIECE = 256
DN_HALO = 8
DN_HB = 4
assert DN_HEADS % DN_HB == 0


def _deltanet_kernel(alog_ref, dtb_ref, q_ref, k_ref, v_ref, z_ref, cq_ref, ck_ref, cv_ref,
                     bpre_ref, apre_ref, onorm_ref, o_ref,
                     xf_ref, qs_ref, ks_ref, vs_ref, gcum_ref, betac_ref, gc_ref,
                     pm_ref, rq_ref, qq_ref, o0_ref, elast_ref):
    seq = q_ref.shape[1]
    c = DN_CHUNK
    n_chunks = seq // c
    assert 2 * n_chunks <= c
    head0 = pl.program_id(1) * DN_HB
    ki = lax.broadcasted_iota(jnp.int32, (c, c), 0)
    ji = lax.broadcasted_iota(jnp.int32, (c, c), 1)
    upper = jnp.where(ki <= ji, 1.0, 0.0).astype(BF16)
    causal = ki >= ji
    strict = ki > ji
    onorm = onorm_ref[...]

    xf_ref[0:DN_HALO, :] = jnp.zeros((DN_HALO, DN_DK), F32)

    def conv_silu(x_ref, w_ref, hb, finish, out_ref):
        cols = slice(hb * DN_DK, (hb + 1) * DN_DK)
        w = w_ref[:, cols]
        pieces = [slice(p * DN_PIECE, (p + 1) * DN_PIECE) for p in range(seq // DN_PIECE)]
        for rows in pieces:
            xf_ref[DN_HALO + rows.start:DN_HALO + rows.stop, :] = x_ref[0, rows, cols].astype(F32)
        for rows in pieces:
            acc = xf_ref[DN_HALO + rows.start:DN_HALO + rows.stop, :] * w[DN_CONV - 1:DN_CONV, :]
            for j in range(1, DN_CONV):
                acc = acc + xf_ref[DN_HALO + rows.start - j:DN_HALO + rows.stop - j, :] * w[DN_CONV - 1 - j:DN_CONV - j, :]
            out_ref[rows, :] = finish(_silu(acc))

    def l2n(scale):
        return lambda x: x * (lax.rsqrt(jnp.sum(x * x, axis=-1, keepdims=True) + NORM_EPS) * scale)

    def prologue(hb):
        conv_silu(q_ref, cq_ref, hb, l2n(DN_DK ** -0.5), qs_ref)
        conv_silu(k_ref, ck_ref, hb, l2n(1.0), ks_ref)
        conv_silu(v_ref, cv_ref, hb, lambda x: x, vs_ref)
        beta = _sigmoid(bpre_ref[hb, 0])
        a = apre_ref[hb, 0] + dtb_ref[head0 + hb]
        softplus = jnp.maximum(a, 0.0) + jnp.log(1.0 + jnp.exp(-jnp.abs(a)))
        g_log = -jnp.exp(jnp.full(a.shape, alog_ref[head0 + hb], F32)) * softplus
        gcum = _dot_exact01(g_log, upper)
        gcum_ref[...] = gcum
        t = jnp.concatenate([beta, gcum, jnp.zeros((c - 2 * n_chunks, c), F32)], axis=0).T
        for ci in range(n_chunks):
            betac_ref[ci * c:(ci + 1) * c, :] = jnp.broadcast_to(t[:, ci:ci + 1], (c, DN_DV))
            gc_ref[ci * c:(ci + 1) * c, :] = jnp.broadcast_to(t[:, n_chunks + ci:n_chunks + ci + 1], (c, DN_DV))

    def prepare(hb, cis):
        each = lambda f, *ls: [f(*xs) for xs in zip(*ls)]
        rows = [pl.ds(_aligned(ci * c, c), c) for ci in cis]
        qc = [qs_ref[r, :] for r in rows]
        kc = [ks_ref[r, :] for r in rows]
        vc = [vs_ref[r, :] for r in rows]
        beta_c = [betac_ref[r, :] for r in rows]
        g_c = [gc_ref[r, :] for r in rows]
        g_j = [jnp.broadcast_to(gcum_ref[pl.ds(ci, 1), :], (c, c)) for ci in cis]
        decay = each(lambda gi, gj: jnp.exp(jnp.where(causal, gi - gj, -jnp.inf)), g_c, g_j)
        kq = each(lambda k, q: _dot_nt(jnp.concatenate([k, q], axis=0), k), kc, qc)
        m = each(lambda b, x, d: jnp.where(strict, -(b * x[:c, :] * d), 0.0), beta_c, kq, decay)
        pw = each(lambda x: _dot(x, x), m)
        r = m
        for _ in range(DN_SQUARINGS - 1):
            xs = each(lambda p_, r_: _dot(p_, jnp.concatenate([p_, r_], axis=1)), pw, r)
            r = each(lambda r_, p_, x: r_ + p_ + x[:, c:], r, pw, xs)
            pw = [x[:, :c] for x in xs]
        xs = each(_dot, pw, r)
        r = each(lambda r_, p_, x: r_ + p_ + x, r, pw, xs)
        e_g = [jnp.exp(g) for g in g_c]
        rhs = each(lambda b, v, e, k: jnp.concatenate([b * v, b * e * k], axis=1), beta_c, vc, e_g, kc)
        sol = each(lambda rh, r_: rh + _dot(r_, rh), rhs, r)
        attn = each(lambda x, d: jnp.where(causal, x[c:, :] * d, 0.0), kq, decay)
        k_d = each(lambda k, g: k * jnp.exp(jnp.broadcast_to(g[c - 1:c, :], (c, DN_DV)) - g), kc, g_c)
        kt = each(_dot_tn, k_d, sol)
        at = each(_dot, attn, sol)
        for i, (ci, r_) in enumerate(zip(cis, rows)):
            qq_ref[hb, r_, :] = kt[i][:, :DN_DV]
            pm_ref[hb, r_, :] = kt[i][:, DN_DV:].astype(BF16)
            o0_ref[hb, r_, :] = at[i][:, :DN_DV]
            rq_ref[hb, r_, :] = (qc[i] * e_g[i] - at[i][:, DN_DV:]).astype(BF16)
            elast_ref[hb, pl.ds(_aligned(ci * 8, 8), 8), :] = jnp.exp(
                jnp.broadcast_to(g_c[i][c - 1:c, :], (8, DN_DV)))

    group = min(DN_GROUP, n_chunks)
    assert n_chunks % group == 0
    for hb in range(DN_HB):
        prologue(hb)
        if group == n_chunks:
            prepare(hb, list(range(n_chunks)))
        else:
            def prepare_group(gi, carry, hb=hb):
                prepare(hb, [gi * group + k for k in range(group)])
                return carry

            lax.fori_loop(0, n_chunks // group, prepare_group, 0)

    def chunk_step(ci, states):
        rows = pl.ds(pl.multiple_of(ci * c, c), c)
        xs = [_dot(jnp.concatenate([pm_ref[hb, rows, :], rq_ref[hb, rows, :]], axis=0), states[hb])
              for hb in range(DN_HB)]
        new_states = []
        for hb in range(DN_HB):
            cols = slice(hb * DN_DV, (hb + 1) * DN_DV)
            e_last = jnp.broadcast_to(elast_ref[hb, pl.ds(pl.multiple_of(ci * 8, 8), 1), :], (DN_DK, DN_DV))
            new_states.append(e_last * states[hb] - xs[hb][:c, :] + qq_ref[hb, rows, :])
            o = xs[hb][c:, :] + o0_ref[hb, rows, :]
            zc = z_ref[0, rows, cols].astype(F32)
            o_ref[0, rows, cols] = (_rms(o, onorm) * _silu(zc)).astype(o_ref.dtype)
        return tuple(new_states)

    lax.fori_loop(0, n_chunks, chunk_step, tuple(jnp.zeros((DN_DK, DN_DV), F32) for _ in range(DN_HB)))


def deltanet(proj, gates_t, conv_w, a_log, dt_bias, o_norm):
    b, s, _ = proj.shape
    ng = DN_HEADS // DN_HB
    nc = s // DN_CHUNK
    wide = DN_HB * DN_DK
    col = lambda off: pl.BlockSpec((1, s, wide), lambda bi, hi: (bi, 0, off + hi))
    cw = lambda off: pl.BlockSpec((DN_CONV, wide), lambda bi, hi: (0, off + hi))
    gate = lambda off: pl.BlockSpec((DN_HB, 1, nc, DN_CHUNK), lambda bi, hi: (off + hi, bi, 0, 0))
    smem = pl.BlockSpec(memory_space=pltpu.SMEM)
    per_head = lambda dt: pltpu.VMEM((DN_HB, s, DN_DV), dt)
    return pl.pallas_call(
        _deltanet_kernel,
        grid=(b, ng),
        in_specs=[smem, smem, col(0), col(ng), col(2 * ng), col(3 * ng), cw(0), cw(ng), cw(2 * ng),
                  gate(0), gate(ng), pl.BlockSpec((1, DN_DV), lambda bi, hi: (0, 0))],
        out_specs=pl.BlockSpec((1, s, wide), lambda bi, hi: (bi, 0, hi)),
        out_shape=jax.ShapeDtypeStruct((b, s, DN_V), BF16),
        scratch_shapes=[pltpu.VMEM((DN_HALO + s, DN_DK), F32),
                        pltpu.VMEM((s, DN_DK), F32), pltpu.VMEM((s, DN_DK), F32), pltpu.VMEM((s, DN_DV), F32),
                        pltpu.VMEM((nc, DN_CHUNK), F32), pltpu.VMEM((s, DN_DV), F32), pltpu.VMEM((s, DN_DV), F32),
                        per_head(BF16), per_head(BF16), per_head(F32), per_head(F32),
                        pltpu.VMEM((DN_HB, nc * 8, DN_DV), F32)],
        compiler_params=pltpu.CompilerParams(dimension_semantics=("parallel", "parallel"),
                                             vmem_limit_bytes=VMEM_LIMIT_DELTANET),
        name="deltanet",
    )(a_log, dt_bias, proj, proj, proj, proj, conv_w, conv_w, conv_w, gates_t, gates_t, o_norm.reshape(1, DN_DV))


ATT_PIECE = 256
ATT_M_INIT = -1e30


def _attention_kernel(q_ref, kv_ref, qn_ref, kn_ref, o_ref, qf_ref, kf_ref, vf_ref, acc_ref, m_ref, l_ref):
    seq = q_ref.shape[1]
    grp = pl.program_id(1)
    hd = HEAD_DIM
    blk = ATT_BLOCK
    slopes = _alibi_slopes()

    qg = qn_ref[0] * (hd ** -0.5)
    kg = kn_ref[0]

    def prep(pi, carry):
        r0 = pl.multiple_of(pi * ATT_PIECE, ATT_PIECE)
        rows = pl.ds(r0, ATT_PIECE)
        for j in range(Q_PER_GROUP):
            cols = slice(j * hd, (j + 1) * hd)
            qf_ref[j, rows, :] = _rms(q_ref[0, rows, cols].astype(F32), 1.0) * qg
        for j in range(KV_PER_GROUP):
            cols = slice(j * hd, (j + 1) * hd)
            kf_ref[j, rows, :] = _rms(kv_ref[0, rows, cols].astype(F32), 1.0) * kg
            vcols = slice((KV_PER_GROUP + j) * hd, (KV_PER_GROUP + j + 1) * hd)
            vf_ref[j, rows, :] = kv_ref[0, rows, vcols].astype(F32)
        return carry

    lax.fori_loop(0, seq // ATT_PIECE, prep, 0)

    @pl.when(grp == 0)
    def _():
        def init(pi, carry):
            r0 = pl.multiple_of(pi * ATT_PIECE, ATT_PIECE)
            rows = pl.ds(r0, ATT_PIECE)
            for j in range(Q_PER_GROUP):
                acc_ref[j, rows, :] = jnp.zeros((ATT_PIECE, hd), F32)
                l_ref[j, rows, :] = jnp.zeros((ATT_PIECE, hd), F32)
                m_ref[j, rows, :] = jnp.full((ATT_PIECE, hd), ATT_M_INIT, F32)
            return carry

        lax.fori_loop(0, seq // ATT_PIECE, init, 0)

    def rows_of(start, size, dil):
        return pl.ds(start, size) if dil == 1 else pl.ds(start, size, stride=dil)

    def attend(g, dil, q_start, k_start, nk):
        qrows = rows_of(q_start, blk, dil)
        krows = rows_of(k_start, nk, dil)
        qi = lax.broadcasted_iota(jnp.int32, (blk, nk), 0)
        kidx = lax.broadcasted_iota(jnp.int32, (blk, nk), 1)
        dist = (nk - blk) + qi - kidx
        valid = (dist >= 0) & (dist <= blk)
        distf = dist.astype(F32)
        kv_heads = range(KV_PER_GROUP)
        heads = [(kvh, rep) for kvh in kv_heads for rep in range(Q_REP)]
        vb = [vf_ref[kvh, krows, :].astype(BF16) for kvh in kv_heads]
        sc2 = [_dot_nt(jnp.concatenate([qf_ref[kvh * Q_REP + rep, qrows, :] for rep in range(Q_REP)], axis=0),
                       kf_ref[kvh, krows, :]) for kvh in kv_heads]
        sc = [jnp.where(valid, sc2[kvh][rep * blk:(rep + 1) * blk, :] - float(slopes[g, kvh, rep] * dil) * distf,
                        -jnp.inf) for kvh, rep in heads]
        m_old = [m_ref[kvh * Q_REP + rep, qrows, :] for kvh, rep in heads]
        m_new = [jnp.maximum(mo, jnp.max(s, axis=-1, keepdims=True)) for mo, s in zip(m_old, sc)]
        alpha = [jnp.exp(mo - mn) for mo, mn in zip(m_old, m_new)]
        p = [jnp.exp(s - mn[:, 0:1]) for s, mn in zip(sc, m_new)]
        pv2 = [jnp.dot(jnp.concatenate([p[kvh * Q_REP + rep].astype(BF16) for rep in range(Q_REP)], axis=0),
                       vb[kvh], preferred_element_type=F32) for kvh in kv_heads]
        for i, (kvh, rep) in enumerate(heads):
            j = kvh * Q_REP + rep
            l_ref[j, qrows, :] = alpha[i] * l_ref[j, qrows, :] + jnp.sum(p[i], axis=-1, keepdims=True)
            acc_ref[j, qrows, :] = alpha[i] * acc_ref[j, qrows, :] + pv2[kvh][rep * blk:(rep + 1) * blk, :]
            m_ref[j, qrows, :] = m_new[i]

    for g, (window, dil) in enumerate(DIL_CONFIGS):
        sub_len = seq // dil
        nblk = sub_len // blk

        @pl.when(grp == g)
        def _(g=g, dil=dil, nblk=nblk):
            def residue(res, carry):
                attend(g, dil, res, res, blk)
                if nblk > 1:
                    def later(n, c2):
                        attend(g, dil, res + n * blk * dil, res + (n - 1) * blk * dil, 2 * blk)
                        return c2
                    lax.fori_loop(1, nblk, later, 0)
                return carry

            lax.fori_loop(0, dil, residue, 0)

    @pl.when(grp == N_ATT_GROUPS - 1)
    def _():
        def finish(pi, carry):
            r0 = pl.multiple_of(pi * ATT_PIECE, ATT_PIECE)
            rows = pl.ds(r0, ATT_PIECE)
            for j in range(Q_PER_GROUP):
                cols = slice(j * hd, (j + 1) * hd)
                o_ref[0, rows, cols] = (acc_ref[j, rows, :] / l_ref[j, rows, :]).astype(o_ref.dtype)
            return carry

        lax.fori_loop(0, seq // ATT_PIECE, finish, 0)


def dilated_attention(q, kv, q_norm, k_norm):
    b, s, _ = q.shape
    gw = GROUP_WIDTH
    return pl.pallas_call(
        _attention_kernel,
        grid=(b, N_ATT_GROUPS),
        in_specs=[pl.BlockSpec((1, s, gw), lambda bi, gi: (bi, 0, gi)),
                  pl.BlockSpec((1, s, gw), lambda bi, gi: (bi, 0, gi)),
                  pl.BlockSpec((1, 1, HEAD_DIM), lambda bi, gi: (gi, 0, 0)),
                  pl.BlockSpec((1, 1, HEAD_DIM), lambda bi, gi: (gi, 0, 0))],
        out_specs=pl.BlockSpec((1, s, gw), lambda bi, gi: (bi, 0, 0)),
        out_shape=jax.ShapeDtypeStruct((b, s, gw), BF16),
        scratch_shapes=[pltpu.VMEM((Q_PER_GROUP, s, HEAD_DIM), F32),
                        pltpu.VMEM((KV_PER_GROUP, s, HEAD_DIM), F32),
                        pltpu.VMEM((KV_PER_GROUP, s, HEAD_DIM), F32),
                        pltpu.VMEM((Q_PER_GROUP, s, HEAD_DIM), F32),
                        pltpu.VMEM((Q_PER_GROUP, s, HEAD_DIM), F32),
                        pltpu.VMEM((Q_PER_GROUP, s, HEAD_DIM), F32)],
        compiler_params=pltpu.CompilerParams(dimension_semantics=("parallel", "arbitrary"),
                                             vmem_limit_bytes=VMEM_LIMIT),
        name="dilated_attention",
    )(q, kv, q_norm.reshape(N_ATT_GROUPS, 1, HEAD_DIM), k_norm.reshape(N_ATT_GROUPS, 1, HEAD_DIM))


ROUTER_TM = 512
SEG_ALIGN = 16
SORT_ROWS = 2 * ROUTER_TM + 512
assert SORT_ROWS >= 2 * ROUTER_TM + MOE_EXPERTS * (SEG_ALIGN - 1) and SORT_ROWS % LANES == 0
META_W0, META_W1, META_P0, META_P1 = 0, 1, 2, 3
TAB_CNT, TAB_OFF, TAB_SEG = 0, 1, 2


def _router_kernel(a_ref, res_ref, wo_ref, g_ref, w_ref, b_ref, h_ref, xn_ref, meta_ref, post_ref, tab_ref, cnt_ref,
                   carry_ref):
    tm = res_ref.shape[0]

    @pl.when(pl.program_id(0) == 0)
    def _():
        carry_ref[...] = jnp.zeros_like(carry_ref)

    h = res_ref[...] + jnp.dot(a_ref[...], wo_ref[...], preferred_element_type=F32)
    h_ref[...] = h
    xn = _rms(h, g_ref[...])
    xn_ref[...] = xn.astype(BF16)
    xh, xl = _split2(xn)
    wh, wl = _split2(w_ref[...])
    d = lambda a, bb: jnp.dot(a, bb, preferred_element_type=F32)
    logits = d(xh, wh) + d(xh, wl) + d(xl, wh) + b_ref[...]

    lane = lax.broadcasted_iota(jnp.int32, (tm, ROUTER_LANES), 1)
    big = jnp.int32(ROUTER_LANES)
    first_where = lambda cond: jnp.min(jnp.where(cond, lane, big), axis=-1, keepdims=True)

    gl = jnp.where(lane < MOE_GROUPS, logits, -jnp.inf)
    ge = jnp.exp(gl - jnp.max(gl, axis=-1, keepdims=True))
    gp = ge / jnp.sum(ge, axis=-1, keepdims=True)
    g_w = jnp.max(gp, axis=-1, keepdims=True)
    g_idx = first_where(gp == g_w)

    lo = EXP_LANE0 + g_idx * MOE_EPG
    in_group = (lane >= lo) & (lane < lo + MOE_EPG)
    el = jnp.where(in_group, logits, -jnp.inf)
    ee = jnp.exp(el - jnp.max(el, axis=-1, keepdims=True))
    ep = ee / jnp.sum(ee, axis=-1, keepdims=True)
    p0 = jnp.max(jnp.where(in_group, ep, -1.0), axis=-1, keepdims=True)
    i0 = first_where(in_group & (ep == p0))
    rest = in_group & (lane != i0)
    p1 = jnp.max(jnp.where(rest, ep, -1.0), axis=-1, keepdims=True)
    i1 = first_where(rest & (ep == p1))
    w0 = g_w * p0 / (p0 + p1)
    w1 = g_w * p1 / (p0 + p1)

    oh0 = jnp.where(lane == i0, 1.0, 0.0)
    oh1 = jnp.where(lane == i1, 1.0, 0.0)
    both = oh0 + oh1
    ti = lax.broadcasted_iota(jnp.int32, (tm, tm), 0)
    tj = lax.broadcasted_iota(jnp.int32, (tm, tm), 1)
    before = jnp.where(tj < ti, 1.0, 0.0).astype(BF16)
    within = jnp.dot(before, both.astype(BF16), preferred_element_type=F32)
    cnt = jnp.sum(both, axis=0, keepdims=True)
    cnt_pad = jnp.floor((cnt + (SEG_ALIGN - 1)) * (1.0 / SEG_ALIGN)) * SEG_ALIGN
    li = lax.broadcasted_iota(jnp.int32, (ROUTER_LANES, ROUTER_LANES), 0)
    lj = lax.broadcasted_iota(jnp.int32, (ROUTER_LANES, ROUTER_LANES), 1)
    earlier = jnp.where(li < lj, 1.0, 0.0).astype(BF16)
    tile_off = _dot_exact01(jnp.broadcast_to(cnt_pad, (8, ROUTER_LANES)), earlier)[0:1, :]
    row = tile_off + within
    pos0 = jnp.sum(row * oh0, axis=-1, keepdims=True)
    pos1 = jnp.sum(row * oh1, axis=-1, keepdims=True)
    seg_off = carry_ref[...]
    total = seg_off + cnt_pad
    carry_ref[...] = total
    cnt_ref[...] = jnp.broadcast_to(total, cnt_ref.shape)

    sub = lax.broadcasted_iota(jnp.int32, (8, ROUTER_LANES), 0)
    tab_ref[...] = jnp.where(sub == TAB_CNT, cnt_pad, jnp.where(sub == TAB_OFF, tile_off,
                                                                jnp.where(sub == TAB_SEG, seg_off, 0.0)))
    meta = jnp.zeros((tm, ROUTER_LANES), F32)
    for idx, val in ((META_W0, w0), (META_W1, w1), (META_P0, pos0), (META_P1, pos1)):
        meta = jnp.where(lane == idx, val, meta)
    meta_ref[...] = meta
    post_ref[0] = meta.T[0:8, :]


def outproj_router(a, w_out, res, g, w_router, b_router):
    n, k = res.shape
    ka = a.shape[1]
    tm = ROUTER_TM
    nt = n // tm
    return pl.pallas_call(
        _router_kernel,
        grid=(nt,),
        in_specs=[pl.BlockSpec((tm, ka), lambda i: (i, 0)),
                  pl.BlockSpec((tm, k), lambda i: (i, 0)),
                  pl.BlockSpec((ka, k), lambda i: (0, 0)),
                  pl.BlockSpec((1, k), lambda i: (0, 0)),
                  pl.BlockSpec((k, ROUTER_LANES), lambda i: (0, 0)),
                  pl.BlockSpec((1, ROUTER_LANES), lambda i: (0, 0))],
        out_specs=[pl.BlockSpec((tm, k), lambda i: (i, 0)),
                   pl.BlockSpec((tm, k), lambda i: (i, 0)),
                   pl.BlockSpec((tm, ROUTER_LANES), lambda i: (i, 0)),
                   pl.BlockSpec((1, 8, tm), lambda i: (i, 0, 0)),
                   pl.BlockSpec((8, ROUTER_LANES), lambda i: (i, 0)),
                   pl.BlockSpec((8, ROUTER_LANES), lambda i: (0, 0))],
        out_shape=[jax.ShapeDtypeStruct((n, k), F32),
                   jax.ShapeDtypeStruct((n, k), BF16),
                   jax.ShapeDtypeStruct((n, ROUTER_LANES), F32),
                   jax.ShapeDtypeStruct((nt, 8, tm), F32),
                   jax.ShapeDtypeStruct((nt * 8, ROUTER_LANES), F32),
                   jax.ShapeDtypeStruct((8, ROUTER_LANES), F32)],
        scratch_shapes=[pltpu.VMEM((1, ROUTER_LANES), F32)],
        compiler_params=pltpu.CompilerParams(dimension_semantics=("arbitrary",),
                                             vmem_limit_bytes=VMEM_LIMIT),
        name="outproj_router",
    )(a, res, w_out, g.reshape(1, k), w_router, b_router)


def _pack_halves(x):
    k = x.shape[1] // 2
    lo = pltpu.bitcast(x[:, :k].astype(BF16).astype(F32), jnp.uint32)
    hi = pltpu.bitcast(x[:, k:].astype(BF16).astype(F32), jnp.uint32)
    return (hi & jnp.uint32(0xFFFF0000)) | (lo >> 16)


def _unpack_halves(w):
    lo = pltpu.bitcast(w << 16, F32)
    hi = pltpu.bitcast(w & jnp.uint32(0xFFFF0000), F32)
    return lo.astype(BF16), hi.astype(BF16)


SEG_PIECE = SEG_ALIGN


def _segment_copies(rows_of, src_ref, dst_ref, sem, wait):
    def piece(s, d, rows):
        cp = pltpu.make_async_copy(src_ref.at[pl.ds(pl.multiple_of(s, SEG_ALIGN), rows), :],
                                   dst_ref.at[pl.ds(pl.multiple_of(d, SEG_ALIGN), rows), :], sem)
        cp.wait() if wait else cp.start()

    def per_expert(e, carry):
        cnt, s0, d0 = rows_of(e)
        n_full = cnt // SEG_PIECE

        def full_piece(j, c2):
            piece(s0 + j * SEG_PIECE, d0 + j * SEG_PIECE, SEG_PIECE)
            return c2

        lax.fori_loop(0, n_full, full_piece, 0)

        @pl.when(cnt % SEG_PIECE != 0)
        def _():
            piece(s0 + n_full * SEG_PIECE, d0 + n_full * SEG_PIECE, SEG_ALIGN)

        return carry

    lax.fori_loop(0, MOE_EXPERTS, per_expert, 0)


MAX_PIECES = SORT_ROWS // SEG_ALIGN


def _piece_copies(npieces_ref, rows_ref, tile, vmem_ref, hbm_ref, sem, to_hbm, wait):
    def body(j, carry):
        v = vmem_ref.at[pl.ds(pl.multiple_of(j * SEG_ALIGN, SEG_ALIGN), SEG_ALIGN), :]
        hb = hbm_ref.at[pl.ds(pl.multiple_of(rows_ref[tile * MAX_PIECES + j], SEG_ALIGN), SEG_ALIGN), :]
        cp = pltpu.make_async_copy(v, hb, sem) if to_hbm else pltpu.make_async_copy(hb, v, sem)
        cp.wait() if wait else cp.start()
        return carry

    lax.fori_loop(0, npieces_ref[tile], body, 0)


def _dispatch_kernel(npieces_ref, rows_ref, tail_cnt_ref, tail_dst_ref, nvalid_ref, post_ref, x_ref, out_hbm,
                     xs_ref, zero_ref, sems):
    tile = pl.program_id(0)
    tm = x_ref.shape[0]
    bm = zero_ref.shape[0]
    sem = sems.at[2]
    n_blocks = out_hbm.shape[0] // bm

    @pl.when(tile == 0)
    def _():
        zero_ref[...] = jnp.zeros_like(zero_ref)
        tails = lambda e: (tail_cnt_ref[e], 0, tail_dst_ref[e])

        def unused_block(wait):
            def body(blk, carry):
                cp = pltpu.make_async_copy(zero_ref, out_hbm.at[pl.ds(pl.multiple_of(blk * bm, bm), bm), :], sem)
                cp.wait() if wait else cp.start()
                return carry
            return body

        _segment_copies(tails, zero_ref, out_hbm, sem, wait=False)
        lax.fori_loop(nvalid_ref[0], n_blocks, unused_block(False), 0)
        _segment_copies(tails, zero_ref, out_hbm, sem, wait=True)
        lax.fori_loop(nvalid_ref[0], n_blocks, unused_block(True), 0)

    n_tiles = pl.num_programs(0)
    slot = tile % 2

    def copies(t, s, wait):
        _piece_copies(npieces_ref, rows_ref, t, xs_ref.at[s], out_hbm, sems.at[s], True, wait)

    @pl.when(tile >= 2)
    def _():
        copies(tile - 2, slot, True)

    post = post_ref[0]
    p0 = post[META_P0:META_P0 + 1, :].astype(jnp.int32)
    p1 = post[META_P1:META_P1 + 1, :].astype(jnp.int32)
    r = lax.broadcasted_iota(jnp.int32, (SORT_ROWS, tm), 0)
    sel = jnp.where(r == p0, 1.0, jnp.where(r == p1, 1.0, 0.0)).astype(BF16)
    xs_ref[slot] = _pack_halves(jnp.dot(sel, x_ref[...], preferred_element_type=F32))
    copies(tile, slot, False)

    @pl.when(tile == n_tiles - 1)
    def _():
        @pl.when(tile >= 1)
        def _():
            copies(tile - 1, 1 - slot, True)

        copies(tile, slot, True)


def moe_dispatch(n_pieces, piece_rows, tail_cnt, tail_dst, n_valid, post, xn, rows):
    n, d = xn.shape
    tm = ROUTER_TM
    grid_spec = pltpu.PrefetchScalarGridSpec(
        num_scalar_prefetch=5,
        grid=(n // tm,),
        in_specs=[pl.BlockSpec((1, 8, tm), lambda i, *_: (i, 0, 0)),
                  pl.BlockSpec((tm, d), lambda i, *_: (i, 0))],
        out_specs=pl.BlockSpec(memory_space=pl.ANY),
        scratch_shapes=[pltpu.VMEM((2, SORT_ROWS, d // 2), jnp.uint32), pltpu.VMEM((MOE_BM, d // 2), jnp.uint32),
                        pltpu.SemaphoreType.DMA((3,))],
    )
    return pl.pallas_call(
        _dispatch_kernel,
        grid_spec=grid_spec,
        out_shape=jax.ShapeDtypeStruct((rows, d // 2), jnp.uint32),
        compiler_params=pltpu.CompilerParams(dimension_semantics=("arbitrary",),
                                             vmem_limit_bytes=VMEM_LIMIT),
        name="moe_dispatch",
    )(n_pieces, piece_rows, tail_cnt, tail_dst, n_valid, post, xn)


def _expert_kernel(be_ref, nvalid_ref, x_ref, w1_ref, w3_ref, w2_ref, y_ref, w1b_ref, w3b_ref, w2b_ref):
    i = pl.program_id(0)

    @pl.when(i < nvalid_ref[0])
    def _():
        @pl.when(jnp.logical_or(i == 0, be_ref[i] != be_ref[jnp.maximum(i - 1, 0)]))
        def _():
            w1b_ref[...] = w1_ref[0, 0].astype(BF16)
            w3b_ref[...] = w3_ref[0, 0].astype(BF16)
            w2b_ref[...] = w2_ref[0, 0].astype(BF16)

        xlo, xhi = _unpack_halves(x_ref[...])
        half = xlo.shape[1]
        up = lambda w_ref: (jnp.dot(xlo, w_ref[:half, :], preferred_element_type=F32)
                            + jnp.dot(xhi, w_ref[half:, :], preferred_element_type=F32))
        hdn = (_silu(up(w1b_ref)) * up(w3b_ref)).astype(BF16)
        y_ref[...] = _pack_halves(jnp.dot(hdn, w2b_ref[...], preferred_element_type=F32))

    @pl.when(i >= nvalid_ref[0])
    def _():
        y_ref[...] = jnp.zeros_like(y_ref)


def moe_experts(x_sorted, block_e, n_valid, w1, w3, w2, layer):
    rows, half = x_sorted.shape
    n_blocks = block_e.shape[0]
    bm = MOE_BM
    d, hid = w1.shape[2], w1.shape[3]
    grid_spec = pltpu.PrefetchScalarGridSpec(
        num_scalar_prefetch=2,
        grid=(n_blocks,),
        in_specs=[pl.BlockSpec((bm, half), lambda i, be, nv: (jnp.minimum(i, nv[0] - 1), 0)),
                  pl.BlockSpec((1, 1, d, hid), lambda i, be, nv: (layer, be[i], 0, 0)),
                  pl.BlockSpec((1, 1, d, hid), lambda i, be, nv: (layer, be[i], 0, 0)),
                  pl.BlockSpec((1, 1, hid, d), lambda i, be, nv: (layer, be[i], 0, 0))],
        out_specs=pl.BlockSpec((bm, half), lambda i, be, nv: (i, 0)),
        scratch_shapes=[pltpu.VMEM((d, hid), BF16), pltpu.VMEM((d, hid), BF16), pltpu.VMEM((hid, d), BF16)],
    )
    return pl.pallas_call(
        _expert_kernel,
        grid_spec=grid_spec,
        out_shape=jax.ShapeDtypeStruct((rows, half), jnp.uint32),
        compiler_params=pltpu.CompilerParams(dimension_semantics=("arbitrary",),
                                             vmem_limit_bytes=VMEM_LIMIT),
        name="moe_experts",
    )(block_e, n_valid, x_sorted, w1, w3, w2)


def _combine_ple_kernel(npieces_ref, rows_ref, y_hbm, h_ref, meta_ref, p_ref, g_ref, wg_ref, wp_ref, o_ref,
                        ys_ref, sems):
    tile = pl.program_id(0)
    n_tiles = pl.num_programs(0)
    tm, d = h_ref.shape
    slot = tile % 2

    def copies(t, s, wait):
        _piece_copies(npieces_ref, rows_ref, t, ys_ref.at[s], y_hbm, sems.at[s], False, wait)

    @pl.when(tile == 0)
    def _():
        ys_ref[...] = jnp.zeros_like(ys_ref)
        copies(0, 0, False)

    @pl.when(tile + 1 < n_tiles)
    def _():
        copies(tile + 1, 1 - slot, False)

    copies(tile, slot, True)
    meta = meta_ref[...]
    w0 = meta[:, META_W0:META_W0 + 1]
    w1 = meta[:, META_W1:META_W1 + 1]
    p0 = meta[:, META_P0:META_P0 + 1].astype(jnp.int32)
    p1 = meta[:, META_P1:META_P1 + 1].astype(jnp.int32)
    r = lax.broadcasted_iota(jnp.int32, (tm, SORT_ROWS), 1)
    wmat = jnp.where(r == p0, w0, jnp.where(r == p1, w1, 0.0)).astype(BF16)
    ylo, yhi = _unpack_halves(ys_ref[slot])
    mix = lambda y: jnp.dot(wmat, y, preferred_element_type=F32)
    half = d // 2
    h = jnp.concatenate([h_ref[:, :half] + mix(ylo), h_ref[:, half:] + mix(yhi)], axis=1)
    gate = _sigmoid(jnp.dot(_rms(h, g_ref[...]).astype(BF16), wg_ref[...], preferred_element_type=F32))
    proj = jnp.dot(p_ref[...].astype(BF16), wp_ref[...], preferred_element_type=F32)
    o_ref[...] = h + gate * proj


def moe_combine_ple(n_pieces, piece_rows, y, h, meta, p, layer, g, w_gate, w_proj):
    n, d = h.shape
    pd = p.shape[2]
    tm = ROUTER_TM
    grid_spec = pltpu.PrefetchScalarGridSpec(
        num_scalar_prefetch=2,
        grid=(n // tm,),
        in_specs=[pl.BlockSpec(memory_space=pl.ANY),
                  pl.BlockSpec((tm, d), lambda i, *_: (i, 0)),
                  pl.BlockSpec((tm, ROUTER_LANES), lambda i, *_: (i, 0)),
                  pl.BlockSpec((None, tm, pd), lambda i, *_: (layer, i, 0)),
                  pl.BlockSpec((1, d), lambda i, *_: (0, 0)),
                  pl.BlockSpec((d, d), lambda i, *_: (0, 0)),
                  pl.BlockSpec((pd, d), lambda i, *_: (0, 0))],
        out_specs=pl.BlockSpec((tm, d), lambda i, *_: (i, 0)),
        scratch_shapes=[pltpu.VMEM((2, SORT_ROWS, d // 2), jnp.uint32), pltpu.SemaphoreType.DMA((2,))],
    )
    return pl.pallas_call(
        _combine_ple_kernel,
        grid_spec=grid_spec,
        out_shape=jax.ShapeDtypeStruct((n, d), F32),
        compiler_params=pltpu.CompilerParams(dimension_semantics=("arbitrary",),
                                             vmem_limit_bytes=VMEM_LIMIT),
        name="moe_combine_ple",
    )(n_pieces, piece_rows, y, h, meta, p, g.reshape(1, d), w_gate, w_proj)


def outproj_moe_embedding(a, w_out, res, ffn_norm, w_rg, b_rg, w_re, b_re, w1, w3, w2, p, ple_norm, w_ple_gate,
                          w_ple_proj, layer):
    n, d = res.shape
    pad = ROUTER_LANES - MOE_GROUPS - MOE_EXPERTS
    w_router = jnp.concatenate([w_rg, w_re, jnp.zeros((d, pad), F32)], axis=1)
    b_router = jnp.concatenate([b_rg, b_re, jnp.zeros((pad,), F32)]).reshape(1, ROUTER_LANES)
    h, xn, meta, post, tabs, cnt = outproj_router(a, w_out, res, ffn_norm, w_router, b_router)

    bm = MOE_BM
    nt = n // ROUTER_TM
    lanes = slice(EXP_LANE0, EXP_LANE0 + MOE_EXPERTS)
    totals = cnt[0, lanes].astype(jnp.int32)
    region = (totals + bm - 1) // bm * bm
    region_end = jnp.cumsum(region)
    region_start = region_end - region
    n_blocks = -(-(2 * n + nt * MOE_EXPERTS * (SEG_ALIGN - 1)) // bm) + MOE_EXPERTS
    block_row0 = jnp.arange(n_blocks, dtype=jnp.int32) * bm
    block_e = jnp.minimum(jnp.sum((block_row0[:, None] >= region_end[None, :]).astype(jnp.int32), axis=1),
                          MOE_EXPERTS - 1).astype(jnp.int32)
    n_valid = (region_end[-1:] // bm).astype(jnp.int32)
    tabs = tabs.reshape(nt, 8, ROUTER_LANES)[:, :, lanes].astype(jnp.int32)
    tab_cnt, tab_off = tabs[:, TAB_CNT], tabs[:, TAB_OFF]
    tab_seg = tabs[:, TAB_SEG] + region_start[None, :]
    piece_row0 = jnp.arange(MAX_PIECES, dtype=jnp.int32) * SEG_ALIGN
    piece_e = jnp.minimum(jnp.sum(((tab_off + tab_cnt)[:, None, :] <= piece_row0[None, :, None]).astype(jnp.int32),
                                  axis=2), MOE_EXPERTS - 1)
    take = lambda t: jnp.take_along_axis(t, piece_e, axis=1)
    piece_rows = (take(tab_seg) + piece_row0[None, :] - take(tab_off)).reshape(-1)
    n_pieces = jnp.sum(tab_cnt, axis=1) // SEG_ALIGN

    x_sorted = moe_dispatch(n_pieces, piece_rows, region - totals, region_start + totals, n_valid, post, xn,
                            n_blocks * bm)
    y = moe_experts(x_sorted, block_e, n_valid, w1, w3, w2, layer)
    return moe_combine_ple(n_pieces, piece_rows, y, h, meta, p, layer, ple_norm, w_ple_gate, w_ple_proj)


def kernel(x, p, a_norm, a_w_in, a_conv, a_A_log, a_dt_bias, a_o_norm, a_w_out, kv_norm, w_kv, k_norm, b_norm, b_w_q, b_q_norm, b_w_out, ffn_norm, w_router_group, b_router_group, w_router_expert, b_router_expert, w1, w3, w2, ple_norm, w_ple_gate, w_ple_proj):
    b, s, d = x.shape
    n = b * s
    depth = p.shape[0]
    n_a = a_norm.shape[0]
    h = x.reshape(n, d)
    p_rows = p.reshape(depth, n, -1)
    kv = None
    for i in range(depth):
        if i < n_a:
            w_in = a_w_in[i]
            proj, gates_t = dn_inproj(h, a_norm[i], w_in[:, :DN_MAIN].astype(BF16), w_in[:, DN_MAIN:].T)
            o = deltanet(proj.reshape(b, s, DN_MAIN), gates_t.reshape(2 * DN_HEADS, b, s // DN_CHUNK, DN_CHUNK),
                         a_conv[i], a_A_log[i], a_dt_bias[i], a_o_norm[i])
            mixed, w_out = o.reshape(n, DN_V), a_w_out[i]
        else:
            bl = i - n_a
            q = norm_matmul(h, b_norm[bl], b_w_q[bl].astype(BF16), BF16)
            o = dilated_attention(q.reshape(b, s, -1), kv.reshape(b, s, -1), b_q_norm[bl], k_norm)
            mixed, w_out = o.reshape(n, GROUP_WIDTH), b_w_out[bl]
        h = outproj_moe_embedding(mixed, w_out.astype(BF16), h, ffn_norm[i], w_router_group[i], b_router_group[i],
                                  w_router_expert[i], b_router_expert[i], w1, w3, w2, p_rows, ple_norm[i],
                                  w_ple_gate[i].astype(BF16), w_ple_proj[i].astype(BF16), i)
        if i == n_a - 1:
            kv = norm_matmul(h, kv_norm, w_kv.astype(BF16), BF16)
    return h.reshape(b, s, d)
```

```python
import numpy as np
import jax
import jax.numpy as jnp
from jax import lax
from jax.experimental import pallas as pl
from jax.experimental.pallas import tpu as pltpu

F32 = jnp.float32
BF16 = jnp.bfloat16

NORM_EPS = 1e-6

DN_HEADS = 8
DN_DK = 128
DN_DV = 128
DN_CONV = 4
DN_CHUNK = 128
DN_SQUARINGS = DN_CHUNK.bit_length() - 2
DN_GROUP = 16
assert DN_CHUNK == DN_DK == DN_DV
DN_QK = DN_HEADS * DN_DK
DN_V = DN_HEADS * DN_DV
DN_MAIN = 2 * DN_QK + 2 * DN_V

DIL_CONFIGS = ((128, 1), (512, 4), (2048, 16))
N_ATT_GROUPS = len(DIL_CONFIGS)
HEAD_DIM = 128
Q_PER_GROUP = 4
KV_PER_GROUP = 2
Q_REP = Q_PER_GROUP // KV_PER_GROUP
ATT_BLOCK = 128
ALIBI_MAX = 8.0
GROUP_WIDTH = Q_PER_GROUP * HEAD_DIM

MOE_GROUPS = 4
MOE_EPG = 8
MOE_EXPERTS = MOE_GROUPS * MOE_EPG
MOE_HIDDEN = 512
MOE_BM = 512
ROUTER_LANES = 128
EXP_LANE0 = MOE_GROUPS

LANES = 128
VMEM_LIMIT = 48 * 1024 * 1024
VMEM_LIMIT_DELTANET = 56 * 1024 * 1024


def _alibi_slopes():
    n = N_ATT_GROUPS * Q_PER_GROUP
    s = 2.0 ** (-ALIBI_MAX * np.arange(1, n + 1) / n)
    return s.reshape(N_ATT_GROUPS, KV_PER_GROUP, Q_REP)


def _rms(x, g):
    ms = jnp.mean(x * x, axis=-1, keepdims=True)
    return x * lax.rsqrt(ms + NORM_EPS) * g


def _dot(a, b):
    return jnp.dot(a.astype(BF16), b.astype(BF16), preferred_element_type=F32)


def _dot_nt(a, b):
    return lax.dot_general(a.astype(BF16), b.astype(BF16), (((1,), (1,)), ((), ())),
                           preferred_element_type=F32)


def _dot_tn(a, b):
    return lax.dot_general(a.astype(BF16), b.astype(BF16), (((0,), (0,)), ((), ())),
                           preferred_element_type=F32)


def _split2(x):
    hi = x.astype(BF16)
    lo = (x - hi.astype(F32)).astype(BF16)
    return hi, lo


def _split3(x):
    hi = x.astype(BF16)
    r = x - hi.astype(F32)
    mid = r.astype(BF16)
    lo = (r - mid.astype(F32)).astype(BF16)
    return hi, mid, lo


def _dot_exact01(x, sel):
    hi, mid, lo = _split3(x)
    d = lambda p: jnp.dot(p, sel, preferred_element_type=F32)
    return d(hi) + d(mid) + d(lo)


def _aligned(i, m):
    return i if isinstance(i, int) else pl.multiple_of(i, m)


def _sigmoid(x):
    return 1.0 / (1.0 + jnp.exp(-x))


def _silu(x):
    return x * _sigmoid(x)


ROW_TILE = 512


def _row_tiled_call(body, name, n, row_inputs, resident_inputs, out_widths, out_dtypes, extra_out_specs=(),
                    extra_out_shapes=()):
    tm = ROW_TILE
    row_spec = lambda width: pl.BlockSpec((tm, width), lambda i: (i, 0))
    whole = lambda a: pl.BlockSpec(a.shape, lambda i: (0,) * a.ndim)

    def in_row_spec(a):
        if isinstance(a, tuple):
            arr, layer = a
            return pl.BlockSpec((None, tm, arr.shape[2]), lambda i: (layer, i, 0))
        return row_spec(a.shape[1])

    row_specs = [in_row_spec(a) for a in row_inputs]
    row_inputs = [a[0] if isinstance(a, tuple) else a for a in row_inputs]
    return pl.pallas_call(
        body,
        grid=(n // tm,),
        in_specs=row_specs + [whole(a) for a in resident_inputs],
        out_specs=[row_spec(w) for w in out_widths] + list(extra_out_specs),
        out_shape=[jax.ShapeDtypeStruct((n, w), dt) for w, dt in zip(out_widths, out_dtypes)]
        + list(extra_out_shapes),
        compiler_params=pltpu.CompilerParams(dimension_semantics=("parallel",), vmem_limit_bytes=VMEM_LIMIT),
        name=name,
    )(*row_inputs, *resident_inputs)


def _nm_kernel(x_ref, g_ref, w_ref, o_ref):
    xn = _rms(x_ref[...], g_ref[...]).astype(BF16)
    o_ref[...] = jnp.dot(xn, w_ref[...], preferred_element_type=F32).astype(o_ref.dtype)


def norm_matmul(x, g, w, out_dtype):
    n, k = x.shape
    return _row_tiled_call(_nm_kernel, "norm_matmul", n, [x], [g.reshape(1, k), w], [w.shape[1]], [out_dtype])[0]


def _dn_inproj_kernel(x_ref, g_ref, w_ref, wgt_ref, o_ref, gt_ref):
    xn = _rms(x_ref[...], g_ref[...])
    xh, xl = _split2(xn)
    wh, wl = _split2(wgt_ref[...])
    gt_ref[...] = _dot_nt(wh, xh) + _dot_nt(wh, xl) + _dot_nt(wl, xh)
    o_ref[...] = jnp.dot(xh, w_ref[...], preferred_element_type=F32).astype(o_ref.dtype)


def dn_inproj(x, g, w_main, w_gates_t):
    n, k = x.shape
    ng = w_gates_t.shape[0]
    return _row_tiled_call(_dn_inproj_kernel, "dn_inproj", n, [x], [g.reshape(1, k), w_main, w_gates_t],
                           [w_main.shape[1]], [BF16],
                           extra_out_specs=[pl.BlockSpec((ng, ROW_TILE), lambda i: (0, i))],
                           extra_out_shapes=[jax.ShapeDtypeStruct((ng, n), F32)])


DN_PIECE = 256
DN_HALO = 8
DN_HB = 4
assert DN_HEADS % DN_HB == 0


def _deltanet_kernel(alog_ref, dtb_ref, q_ref, k_ref, v_ref, z_ref, cq_ref, ck_ref, cv_ref,
                     bpre_ref, apre_ref, onorm_ref, o_ref,
                     xf_ref, qs_ref, ks_ref, vs_ref, gcum_ref, betac_ref, gc_ref,
                     pm_ref, rq_ref, qq_ref, o0_ref, elast_ref):
    seq = q_ref.shape[1]
    c = DN_CHUNK
    n_chunks = seq // c
    assert 2 * n_chunks <= c
    head0 = pl.program_id(1) * DN_HB
    ki = lax.broadcasted_iota(jnp.int32, (c, c), 0)
    ji = lax.broadcasted_iota(jnp.int32, (c, c), 1)
    upper = jnp.where(ki <= ji, 1.0, 0.0).astype(BF16)
    causal = ki >= ji
    strict = ki > ji
    onorm = onorm_ref[...]

    xf_ref[0:DN_HALO, :] = jnp.zeros((DN_HALO, DN_DK), F32)

    def conv_silu(x_ref, w_ref, hb, finish, out_ref):
        cols = slice(hb * DN_DK, (hb + 1) * DN_DK)
        w = w_ref[:, cols]
        pieces = [slice(p * DN_PIECE, (p + 1) * DN_PIECE) for p in range(seq // DN_PIECE)]
        for rows in pieces:
            xf_ref[DN_HALO + rows.start:DN_HALO + rows.stop, :] = x_ref[0, rows, cols].astype(F32)
        for rows in pieces:
            acc = xf_ref[DN_HALO + rows.start:DN_HALO + rows.stop, :] * w[DN_CONV - 1:DN_CONV, :]
            for j in range(1, DN_CONV):
                acc = acc + xf_ref[DN_HALO + rows.start - j:DN_HALO + rows.stop - j, :] * w[DN_CONV - 1 - j:DN_CONV - j, :]
            out_ref[rows, :] = finish(_silu(acc))

    def l2n(scale):
        return lambda x: x * (lax.rsqrt(jnp.sum(x * x, axis=-1, keepdims=True) + NORM_EPS) * scale)

    def prologue(hb):
        conv_silu(q_ref, cq_ref, hb, l2n(DN_DK ** -0.5), qs_ref)
        conv_silu(k_ref, ck_ref, hb, l2n(1.0), ks_ref)
        conv_silu(v_ref, cv_ref, hb, lambda x: x, vs_ref)
        beta = _sigmoid(bpre_ref[hb, 0])
        a = apre_ref[hb, 0] + dtb_ref[head0 + hb]
        softplus = jnp.maximum(a, 0.0) + jnp.log(1.0 + jnp.exp(-jnp.abs(a)))
        g_log = -jnp.exp(jnp.full(a.shape, alog_ref[head0 + hb], F32)) * softplus
        gcum = _dot_exact01(g_log, upper)
        gcum_ref[...] = gcum
        t = jnp.concatenate([beta, gcum, jnp.zeros((c - 2 * n_chunks, c), F32)], axis=0).T
        for ci in range(n_chunks):
            betac_ref[ci * c:(ci + 1) * c, :] = jnp.broadcast_to(t[:, ci:ci + 1], (c, DN_DV))
            gc_ref[ci * c:(ci + 1) * c, :] = jnp.broadcast_to(t[:, n_chunks + ci:n_chunks + ci + 1], (c, DN_DV))

    def prepare(hb, cis):
        each = lambda f, *ls: [f(*xs) for xs in zip(*ls)]
        rows = [pl.ds(_aligned(ci * c, c), c) for ci in cis]
        qc = [qs_ref[r, :] for r in rows]
        kc = [ks_ref[r, :] for r in rows]
        vc = [vs_ref[r, :] for r in rows]
        beta_c = [betac_ref[r, :] for r in rows]
        g_c = [gc_ref[r, :] for r in rows]
        g_j = [jnp.broadcast_to(gcum_ref[pl.ds(ci, 1), :], (c, c)) for ci in cis]
        decay = each(lambda gi, gj: jnp.exp(jnp.where(causal, gi - gj, -jnp.inf)), g_c, g_j)
        kq = each(lambda k, q: _dot_nt(jnp.concatenate([k, q], axis=0), k), kc, qc)
        m = each(lambda b, x, d: jnp.where(strict, -(b * x[:c, :] * d), 0.0), beta_c, kq, decay)
        pw = each(lambda x: _dot(x, x), m)
        r = m
        for _ in range(DN_SQUARINGS - 1):
            xs = each(lambda p_, r_: _dot(p_, jnp.concatenate([p_, r_], axis=1)), pw, r)
            r = each(lambda r_, p_, x: r_ + p_ + x[:, c:], r, pw, xs)
            pw = [x[:, :c] for x in xs]
        xs = each(_dot, pw, r)
        r = each(lambda r_, p_, x: r_ + p_ + x, r, pw, xs)
        e_g = [jnp.exp(g) for g in g_c]
        rhs = each(lambda b, v, e, k: jnp.concatenate([b * v, b * e * k], axis=1), beta_c, vc, e_g, kc)
        sol = each(lambda rh, r_: rh + _dot(r_, rh), rhs, r)
        attn = each(lambda x, d: jnp.where(causal, x[c:, :] * d, 0.0), kq, decay)
        k_d = each(lambda k, g: k * jnp.exp(jnp.broadcast_to(g[c - 1:c, :], (c, DN_DV)) - g), kc, g_c)
        kt = each(_dot_tn, k_d, sol)
        at = each(_dot, attn, sol)
        for i, (ci, r_) in enumerate(zip(cis, rows)):
            qq_ref[hb, r_, :] = kt[i][:, :DN_DV]
            pm_ref[hb, r_, :] = kt[i][:, DN_DV:].astype(BF16)
            o0_ref[hb, r_, :] = at[i][:, :DN_DV]
            rq_ref[hb, r_, :] = (qc[i] * e_g[i] - at[i][:, DN_DV:]).astype(BF16)
            elast_ref[hb, pl.ds(_aligned(ci * 8, 8), 8), :] = jnp.exp(
                jnp.broadcast_to(g_c[i][c - 1:c, :], (8, DN_DV)))

    group = min(DN_GROUP, n_chunks)
    assert n_chunks % group == 0
    for hb in range(DN_HB):
        prologue(hb)
        if group == n_chunks:
            prepare(hb, list(range(n_chunks)))
        else:
            def prepare_group(gi, carry, hb=hb):
                prepare(hb, [gi * group + k for k in range(group)])
                return carry

            lax.fori_loop(0, n_chunks // group, prepare_group, 0)

    def chunk_step(ci, states):
        rows = pl.ds(pl.multiple_of(ci * c, c), c)
        xs = [_dot(jnp.concatenate([pm_ref[hb, rows, :], rq_ref[hb, rows, :]], axis=0), states[hb])
              for hb in range(DN_HB)]
        new_states = []
        for hb in range(DN_HB):
            cols = slice(hb * DN_DV, (hb + 1) * DN_DV)
            e_last = jnp.broadcast_to(elast_ref[hb, pl.ds(pl.multiple_of(ci * 8, 8), 1), :], (DN_DK, DN_DV))
            new_states.append(e_last * states[hb] - xs[hb][:c, :] + qq_ref[hb, rows, :])
            o = xs[hb][c:, :] + o0_ref[hb, rows, :]
            zc = z_ref[0, rows, cols].astype(F32)
            o_ref[0, rows, cols] = (_rms(o, onorm) * _silu(zc)).astype(o_ref.dtype)
        return tuple(new_states)

    lax.fori_loop(0, n_chunks, chunk_step, tuple(jnp.zeros((DN_DK, DN_DV), F32) for _ in range(DN_HB)))


def deltanet(proj, gates_t, conv_w, a_log, dt_bias, o_norm):
    b, s, _ = proj.shape
    ng = DN_HEADS // DN_HB
    nc = s // DN_CHUNK
    wide = DN_HB * DN_DK
    col = lambda off: pl.BlockSpec((1, s, wide), lambda bi, hi: (bi, 0, off + hi))
    cw = lambda off: pl.BlockSpec((DN_CONV, wide), lambda bi, hi: (0, off + hi))
    gate = lambda off: pl.BlockSpec((DN_HB, 1, nc, DN_CHUNK), lambda bi, hi: (off + hi, bi, 0, 0))
    smem = pl.BlockSpec(memory_space=pltpu.SMEM)
    per_head = lambda dt: pltpu.VMEM((DN_HB, s, DN_DV), dt)
    return pl.pallas_call(
        _deltanet_kernel,
        grid=(b, ng),
        in_specs=[smem, smem, col(0), col(ng), col(2 * ng), col(3 * ng), cw(0), cw(ng), cw(2 * ng),
                  gate(0), gate(ng), pl.BlockSpec((1, DN_DV), lambda bi, hi: (0, 0))],
        out_specs=pl.BlockSpec((1, s, wide), lambda bi, hi: (bi, 0, hi)),
        out_shape=jax.ShapeDtypeStruct((b, s, DN_V), BF16),
        scratch_shapes=[pltpu.VMEM((DN_HALO + s, DN_DK), F32),
                        pltpu.VMEM((s, DN_DK), F32), pltpu.VMEM((s, DN_DK), F32), pltpu.VMEM((s, DN_DV), F32),
                        pltpu.VMEM((nc, DN_CHUNK), F32), pltpu.VMEM((s, DN_DV), F32), pltpu.VMEM((s, DN_DV), F32),
                        per_head(BF16), per_head(BF16), per_head(F32), per_head(F32),
                        pltpu.VMEM((DN_HB, nc * 8, DN_DV), F32)],
        compiler_params=pltpu.CompilerParams(dimension_semantics=("parallel", "parallel"),
                                             vmem_limit_bytes=VMEM_LIMIT_DELTANET),
        name="deltanet",
    )(a_log, dt_bias, proj, proj, proj, proj, conv_w, conv_w, conv_w, gates_t, gates_t, o_norm.reshape(1, DN_DV))


ATT_PIECE = 256
ATT_M_INIT = -1e30


def _attention_kernel(q_ref, kv_ref, qn_ref, kn_ref, o_ref, qf_ref, kf_ref, vf_ref, acc_ref, m_ref, l_ref):
    seq = q_ref.shape[1]
    grp = pl.program_id(1)
    hd = HEAD_DIM
    blk = ATT_BLOCK
    slopes = _alibi_slopes()

    qg = qn_ref[0] * (hd ** -0.5)
    kg = kn_ref[0]

    def prep(pi, carry):
        r0 = pl.multiple_of(pi * ATT_PIECE, ATT_PIECE)
        rows = pl.ds(r0, ATT_PIECE)
        for j in range(Q_PER_GROUP):
            cols = slice(j * hd, (j + 1) * hd)
            qf_ref[j, rows, :] = _rms(q_ref[0, rows, cols].astype(F32), 1.0) * qg
        for j in range(KV_PER_GROUP):
            cols = slice(j * hd, (j + 1) * hd)
            kf_ref[j, rows, :] = _rms(kv_ref[0, rows, cols].astype(F32), 1.0) * kg
            vcols = slice((KV_PER_GROUP + j) * hd, (KV_PER_GROUP + j + 1) * hd)
            vf_ref[j, rows, :] = kv_ref[0, rows, vcols].astype(F32)
        return carry

    lax.fori_loop(0, seq // ATT_PIECE, prep, 0)

    @pl.when(grp == 0)
    def _():
        def init(pi, carry):
            r0 = pl.multiple_of(pi * ATT_PIECE, ATT_PIECE)
            rows = pl.ds(r0, ATT_PIECE)
            for j in range(Q_PER_GROUP):
                acc_ref[j, rows, :] = jnp.zeros((ATT_PIECE, hd), F32)
                l_ref[j, rows, :] = jnp.zeros((ATT_PIECE, hd), F32)
                m_ref[j, rows, :] = jnp.full((ATT_PIECE, hd), ATT_M_INIT, F32)
            return carry

        lax.fori_loop(0, seq // ATT_PIECE, init, 0)

    def rows_of(start, size, dil):
        return pl.ds(start, size) if dil == 1 else pl.ds(start, size, stride=dil)

    def attend(g, dil, q_start, k_start, nk):
        qrows = rows_of(q_start, blk, dil)
        krows = rows_of(k_start, nk, dil)
        qi = lax.broadcasted_iota(jnp.int32, (blk, nk), 0)
        kidx = lax.broadcasted_iota(jnp.int32, (blk, nk), 1)
        dist = (nk - blk) + qi - kidx
        valid = (dist >= 0) & (dist <= blk)
        distf = dist.astype(F32)
        kv_heads = range(KV_PER_GROUP)
        heads = [(kvh, rep) for kvh in kv_heads for rep in range(Q_REP)]
        vb = [vf_ref[kvh, krows, :].astype(BF16) for kvh in kv_heads]
        sc2 = [_dot_nt(jnp.concatenate([qf_ref[kvh * Q_REP + rep, qrows, :] for rep in range(Q_REP)], axis=0),
                       kf_ref[kvh, krows, :]) for kvh in kv_heads]
        sc = [jnp.where(valid, sc2[kvh][rep * blk:(rep + 1) * blk, :] - float(slopes[g, kvh, rep] * dil) * distf,
                        -jnp.inf) for kvh, rep in heads]
        m_old = [m_ref[kvh * Q_REP + rep, qrows, :] for kvh, rep in heads]
        m_new = [jnp.maximum(mo, jnp.max(s, axis=-1, keepdims=True)) for mo, s in zip(m_old, sc)]
        alpha = [jnp.exp(mo - mn) for mo, mn in zip(m_old, m_new)]
        p = [jnp.exp(s - mn[:, 0:1]) for s, mn in zip(sc, m_new)]
        pv2 = [jnp.dot(jnp.concatenate([p[kvh * Q_REP + rep].astype(BF16) for rep in range(Q_REP)], axis=0),
                       vb[kvh], preferred_element_type=F32) for kvh in kv_heads]
        for i, (kvh, rep) in enumerate(heads):
            j = kvh * Q_REP + rep
            l_ref[j, qrows, :] = alpha[i] * l_ref[j, qrows, :] + jnp.sum(p[i], axis=-1, keepdims=True)
            acc_ref[j, qrows, :] = alpha[i] * acc_ref[j, qrows, :] + pv2[kvh][rep * blk:(rep + 1) * blk, :]
            m_ref[j, qrows, :] = m_new[i]

    for g, (window, dil) in enumerate(DIL_CONFIGS):
        sub_len = seq // dil
        nblk = sub_len // blk

        @pl.when(grp == g)
        def _(g=g, dil=dil, nblk=nblk):
            def residue(res, carry):
                attend(g, dil, res, res, blk)
                if nblk > 1:
                    def later(n, c2):
                        attend(g, dil, res + n * blk * dil, res + (n - 1) * blk * dil, 2 * blk)
                        return c2
                    lax.fori_loop(1, nblk, later, 0)
                return carry

            lax.fori_loop(0, dil, residue, 0)

    @pl.when(grp == N_ATT_GROUPS - 1)
    def _():
        def finish(pi, carry):
            r0 = pl.multiple_of(pi * ATT_PIECE, ATT_PIECE)
            rows = pl.ds(r0, ATT_PIECE)
            for j in range(Q_PER_GROUP):
                cols = slice(j * hd, (j + 1) * hd)
                o_ref[0, rows, cols] = (acc_ref[j, rows, :] / l_ref[j, rows, :]).astype(o_ref.dtype)
            return carry

        lax.fori_loop(0, seq // ATT_PIECE, finish, 0)


def dilated_attention(q, kv, q_norm, k_norm):
    b, s, _ = q.shape
    gw = GROUP_WIDTH
    return pl.pallas_call(
        _attention_kernel,
        grid=(b, N_ATT_GROUPS),
        in_specs=[pl.BlockSpec((1, s, gw), lambda bi, gi: (bi, 0, gi)),
                  pl.BlockSpec((1, s, gw), lambda bi, gi: (bi, 0, gi)),
                  pl.BlockSpec((1, 1, HEAD_DIM), lambda bi, gi: (gi, 0, 0)),
                  pl.BlockSpec((1, 1, HEAD_DIM), lambda bi, gi: (gi, 0, 0))],
        out_specs=pl.BlockSpec((1, s, gw), lambda bi, gi: (bi, 0, 0)),
        out_shape=jax.ShapeDtypeStruct((b, s, gw), BF16),
        scratch_shapes=[pltpu.VMEM((Q_PER_GROUP, s, HEAD_DIM), F32),
                        pltpu.VMEM((KV_PER_GROUP, s, HEAD_DIM), F32),
                        pltpu.VMEM((KV_PER_GROUP, s, HEAD_DIM), F32),
                        pltpu.VMEM((Q_PER_GROUP, s, HEAD_DIM), F32),
                        pltpu.VMEM((Q_PER_GROUP, s, HEAD_DIM), F32),
                        pltpu.VMEM((Q_PER_GROUP, s, HEAD_DIM), F32)],
        compiler_params=pltpu.CompilerParams(dimension_semantics=("parallel", "arbitrary"),
                                             vmem_limit_bytes=VMEM_LIMIT),
        name="dilated_attention",
    )(q, kv, q_norm.reshape(N_ATT_GROUPS, 1, HEAD_DIM), k_norm.reshape(N_ATT_GROUPS, 1, HEAD_DIM))


ROUTER_TM = 512
SEG_ALIGN = 16
SORT_ROWS = 2 * ROUTER_TM + 512
assert SORT_ROWS >= 2 * ROUTER_TM + MOE_EXPERTS * (SEG_ALIGN - 1) and SORT_ROWS % LANES == 0
META_W0, META_W1, META_P0, META_P1 = 0, 1, 2, 3
TAB_CNT, TAB_OFF, TAB_SEG = 0, 1, 2


def _router_kernel(a_ref, res_ref, wo_ref, g_ref, w_ref, b_ref, h_ref, xn_ref, meta_ref, post_ref, tab_ref, cnt_ref,
                   carry_ref):
    tm = res_ref.shape[0]

    @pl.when(pl.program_id(0) == 0)
    def _():
        carry_ref[...] = jnp.zeros_like(carry_ref)

    h = res_ref[...] + jnp.dot(a_ref[...], wo_ref[...], preferred_element_type=F32)
    h_ref[...] = h
    xn = _rms(h, g_ref[...])
    xn_ref[...] = xn.astype(BF16)
    xh, xl = _split2(xn)
    wh, wl = _split2(w_ref[...])
    d = lambda a, bb: jnp.dot(a, bb, preferred_element_type=F32)
    logits = d(xh, wh) + d(xh, wl) + d(xl, wh) + b_ref[...]

    lane = lax.broadcasted_iota(jnp.int32, (tm, ROUTER_LANES), 1)
    big = jnp.int32(ROUTER_LANES)
    first_where = lambda cond: jnp.min(jnp.where(cond, lane, big), axis=-1, keepdims=True)

    gl = jnp.where(lane < MOE_GROUPS, logits, -jnp.inf)
    ge = jnp.exp(gl - jnp.max(gl, axis=-1, keepdims=True))
    gp = ge / jnp.sum(ge, axis=-1, keepdims=True)
    g_w = jnp.max(gp, axis=-1, keepdims=True)
    g_idx = first_where(gp == g_w)

    lo = EXP_LANE0 + g_idx * MOE_EPG
    in_group = (lane >= lo) & (lane < lo + MOE_EPG)
    el = jnp.where(in_group, logits, -jnp.inf)
    ee = jnp.exp(el - jnp.max(el, axis=-1, keepdims=True))
    ep = ee / jnp.sum(ee, axis=-1, keepdims=True)
    p0 = jnp.max(jnp.where(in_group, ep, -1.0), axis=-1, keepdims=True)
    i0 = first_where(in_group & (ep == p0))
    rest = in_group & (lane != i0)
    p1 = jnp.max(jnp.where(rest, ep, -1.0), axis=-1, keepdims=True)
    i1 = first_where(rest & (ep == p1))
    w0 = g_w * p0 / (p0 + p1)
    w1 = g_w * p1 / (p0 + p1)

    oh0 = jnp.where(lane == i0, 1.0, 0.0)
    oh1 = jnp.where(lane == i1, 1.0, 0.0)
    both = oh0 + oh1
    ti = lax.broadcasted_iota(jnp.int32, (tm, tm), 0)
    tj = lax.broadcasted_iota(jnp.int32, (tm, tm), 1)
    before = jnp.where(tj < ti, 1.0, 0.0).astype(BF16)
    within = jnp.dot(before, both.astype(BF16), preferred_element_type=F32)
    cnt = jnp.sum(both, axis=0, keepdims=True)
    cnt_pad = jnp.floor((cnt + (SEG_ALIGN - 1)) * (1.0 / SEG_ALIGN)) * SEG_ALIGN
    li = lax.broadcasted_iota(jnp.int32, (ROUTER_LANES, ROUTER_LANES), 0)
    lj = lax.broadcasted_iota(jnp.int32, (ROUTER_LANES, ROUTER_LANES), 1)
    earlier = jnp.where(li < lj, 1.0, 0.0).astype(BF16)
    tile_off = _dot_exact01(jnp.broadcast_to(cnt_pad, (8, ROUTER_LANES)), earlier)[0:1, :]
    row = tile_off + within
    pos0 = jnp.sum(row * oh0, axis=-1, keepdims=True)
    pos1 = jnp.sum(row * oh1, axis=-1, keepdims=True)
    seg_off = carry_ref[...]
    total = seg_off + cnt_pad
    carry_ref[...] = total
    cnt_ref[...] = jnp.broadcast_to(total, cnt_ref.shape)

    sub = lax.broadcasted_iota(jnp.int32, (8, ROUTER_LANES), 0)
    tab_ref[...] = jnp.where(sub == TAB_CNT, cnt_pad, jnp.where(sub == TAB_OFF, tile_off,
                                                                jnp.where(sub == TAB_SEG, seg_off, 0.0)))
    meta = jnp.zeros((tm, ROUTER_LANES), F32)
    for idx, val in ((META_W0, w0), (META_W1, w1), (META_P0, pos0), (META_P1, pos1)):
        meta = jnp.where(lane == idx, val, meta)
    meta_ref[...] = meta
    post_ref[0] = meta.T[0:8, :]


def outproj_router(a, w_out, res, g, w_router, b_router):
    n, k = res.shape
    ka = a.shape[1]
    tm = ROUTER_TM
    nt = n // tm
    return pl.pallas_call(
        _router_kernel,
        grid=(nt,),
        in_specs=[pl.BlockSpec((tm, ka), lambda i: (i, 0)),
                  pl.BlockSpec((tm, k), lambda i: (i, 0)),
                  pl.BlockSpec((ka, k), lambda i: (0, 0)),
                  pl.BlockSpec((1, k), lambda i: (0, 0)),
                  pl.BlockSpec((k, ROUTER_LANES), lambda i: (0, 0)),
                  pl.BlockSpec((1, ROUTER_LANES), lambda i: (0, 0))],
        out_specs=[pl.BlockSpec((tm, k), lambda i: (i, 0)),
                   pl.BlockSpec((tm, k), lambda i: (i, 0)),
                   pl.BlockSpec((tm, ROUTER_LANES), lambda i: (i, 0)),
                   pl.BlockSpec((1, 8, tm), lambda i: (i, 0, 0)),
                   pl.BlockSpec((8, ROUTER_LANES), lambda i: (i, 0)),
                   pl.BlockSpec((8, ROUTER_LANES), lambda i: (0, 0))],
        out_shape=[jax.ShapeDtypeStruct((n, k), F32),
                   jax.ShapeDtypeStruct((n, k), BF16),
                   jax.ShapeDtypeStruct((n, ROUTER_LANES), F32),
                   jax.ShapeDtypeStruct((nt, 8, tm), F32),
                   jax.ShapeDtypeStruct((nt * 8, ROUTER_LANES), F32),
                   jax.ShapeDtypeStruct((8, ROUTER_LANES), F32)],
        scratch_shapes=[pltpu.VMEM((1, ROUTER_LANES), F32)],
        compiler_params=pltpu.CompilerParams(dimension_semantics=("arbitrary",),
                                             vmem_limit_bytes=VMEM_LIMIT),
        name="outproj_router",
    )(a, res, w_out, g.reshape(1, k), w_router, b_router)


def _pack_halves(x):
    k = x.shape[1] // 2
    lo = pltpu.bitcast(x[:, :k].astype(BF16).astype(F32), jnp.uint32)
    hi = pltpu.bitcast(x[:, k:].astype(BF16).astype(F32), jnp.uint32)
    return (hi & jnp.uint32(0xFFFF0000)) | (lo >> 16)


def _unpack_halves(w):
    lo = pltpu.bitcast(w << 16, F32)
    hi = pltpu.bitcast(w & jnp.uint32(0xFFFF0000), F32)
    return lo.astype(BF16), hi.astype(BF16)


SEG_PIECE = SEG_ALIGN


def _segment_copies(rows_of, src_ref, dst_ref, sem, wait):
    def piece(s, d, rows):
        cp = pltpu.make_async_copy(src_ref.at[pl.ds(pl.multiple_of(s, SEG_ALIGN), rows), :],
                                   dst_ref.at[pl.ds(pl.multiple_of(d, SEG_ALIGN), rows), :], sem)
        cp.wait() if wait else cp.start()

    def per_expert(e, carry):
        cnt, s0, d0 = rows_of(e)
        n_full = cnt // SEG_PIECE

        def full_piece(j, c2):
            piece(s0 + j * SEG_PIECE, d0 + j * SEG_PIECE, SEG_PIECE)
            return c2

        lax.fori_loop(0, n_full, full_piece, 0)

        @pl.when(cnt % SEG_PIECE != 0)
        def _():
            piece(s0 + n_full * SEG_PIECE, d0 + n_full * SEG_PIECE, SEG_ALIGN)

        return carry

    lax.fori_loop(0, MOE_EXPERTS, per_expert, 0)


MAX_PIECES = SORT_ROWS // SEG_ALIGN


def _piece_copies(npieces_ref, rows_ref, tile, vmem_ref, hbm_ref, sem, to_hbm, wait):
    def body(j, carry):
        v = vmem_ref.at[pl.ds(pl.multiple_of(j * SEG_ALIGN, SEG_ALIGN), SEG_ALIGN), :]
        hb = hbm_ref.at[pl.ds(pl.multiple_of(rows_ref[tile * MAX_PIECES + j], SEG_ALIGN), SEG_ALIGN), :]
        cp = pltpu.make_async_copy(v, hb, sem) if to_hbm else pltpu.make_async_copy(hb, v, sem)
        cp.wait() if wait else cp.start()
        return carry

    lax.fori_loop(0, npieces_ref[tile], body, 0)


def _dispatch_kernel(npieces_ref, rows_ref, tail_cnt_ref, tail_dst_ref, nvalid_ref, post_ref, x_ref, out_hbm,
                     xs_ref, zero_ref, sems):
    tile = pl.program_id(0)
    tm = x_ref.shape[0]
    bm = zero_ref.shape[0]
    sem = sems.at[2]
    n_blocks = out_hbm.shape[0] // bm

    @pl.when(tile == 0)
    def _():
        zero_ref[...] = jnp.zeros_like(zero_ref)
        tails = lambda e: (tail_cnt_ref[e], 0, tail_dst_ref[e])

        def unused_block(wait):
            def body(blk, carry):
                cp = pltpu.make_async_copy(zero_ref, out_hbm.at[pl.ds(pl.multiple_of(blk * bm, bm), bm), :], sem)
                cp.wait() if wait else cp.start()
                return carry
            return body

        _segment_copies(tails, zero_ref, out_hbm, sem, wait=False)
        lax.fori_loop(nvalid_ref[0], n_blocks, unused_block(False), 0)
        _segment_copies(tails, zero_ref, out_hbm, sem, wait=True)
        lax.fori_loop(nvalid_ref[0], n_blocks, unused_block(True), 0)

    n_tiles = pl.num_programs(0)
    slot = tile % 2

    def copies(t, s, wait):
        _piece_copies(npieces_ref, rows_ref, t, xs_ref.at[s], out_hbm, sems.at[s], True, wait)

    @pl.when(tile >= 2)
    def _():
        copies(tile - 2, slot, True)

    post = post_ref[0]
    p0 = post[META_P0:META_P0 + 1, :].astype(jnp.int32)
    p1 = post[META_P1:META_P1 + 1, :].astype(jnp.int32)
    r = lax.broadcasted_iota(jnp.int32, (SORT_ROWS, tm), 0)
    sel = jnp.where(r == p0, 1.0, jnp.where(r == p1, 1.0, 0.0)).astype(BF16)
    xs_ref[slot] = _pack_halves(jnp.dot(sel, x_ref[...], preferred_element_type=F32))
    copies(tile, slot, False)

    @pl.when(tile == n_tiles - 1)
    def _():
        @pl.when(tile >= 1)
        def _():
            copies(tile - 1, 1 - slot, True)

        copies(tile, slot, True)


def moe_dispatch(n_pieces, piece_rows, tail_cnt, tail_dst, n_valid, post, xn, rows):
    n, d = xn.shape
    tm = ROUTER_TM
    grid_spec = pltpu.PrefetchScalarGridSpec(
        num_scalar_prefetch=5,
        grid=(n // tm,),
        in_specs=[pl.BlockSpec((1, 8, tm), lambda i, *_: (i, 0, 0)),
                  pl.BlockSpec((tm, d), lambda i, *_: (i, 0))],
        out_specs=pl.BlockSpec(memory_space=pl.ANY),
        scratch_shapes=[pltpu.VMEM((2, SORT_ROWS, d // 2), jnp.uint32), pltpu.VMEM((MOE_BM, d // 2), jnp.uint32),
                        pltpu.SemaphoreType.DMA((3,))],
    )
    return pl.pallas_call(
        _dispatch_kernel,
        grid_spec=grid_spec,
        out_shape=jax.ShapeDtypeStruct((rows, d // 2), jnp.uint32),
        compiler_params=pltpu.CompilerParams(dimension_semantics=("arbitrary",),
                                             vmem_limit_bytes=VMEM_LIMIT),
        name="moe_dispatch",
    )(n_pieces, piece_rows, tail_cnt, tail_dst, n_valid, post, xn)


def _expert_kernel(be_ref, nvalid_ref, x_ref, w1_ref, w3_ref, w2_ref, y_ref, w1b_ref, w3b_ref, w2b_ref):
    i = pl.program_id(0)

    @pl.when(i < nvalid_ref[0])
    def _():
        @pl.when(jnp.logical_or(i == 0, be_ref[i] != be_ref[jnp.maximum(i - 1, 0)]))
        def _():
            w1b_ref[...] = w1_ref[0, 0].astype(BF16)
            w3b_ref[...] = w3_ref[0, 0].astype(BF16)
            w2b_ref[...] = w2_ref[0, 0].astype(BF16)

        xlo, xhi = _unpack_halves(x_ref[...])
        half = xlo.shape[1]
        up = lambda w_ref: (jnp.dot(xlo, w_ref[:half, :], preferred_element_type=F32)
                            + jnp.dot(xhi, w_ref[half:, :], preferred_element_type=F32))
        hdn = (_silu(up(w1b_ref)) * up(w3b_ref)).astype(BF16)
        y_ref[...] = _pack_halves(jnp.dot(hdn, w2b_ref[...], preferred_element_type=F32))

    @pl.when(i >= nvalid_ref[0])
    def _():
        y_ref[...] = jnp.zeros_like(y_ref)


def moe_experts(x_sorted, block_e, n_valid, w1, w3, w2, layer):
    rows, half = x_sorted.shape
    n_blocks = block_e.shape[0]
    bm = MOE_BM
    d, hid = w1.shape[2], w1.shape[3]
    grid_spec = pltpu.PrefetchScalarGridSpec(
        num_scalar_prefetch=2,
        grid=(n_blocks,),
        in_specs=[pl.BlockSpec((bm, half), lambda i, be, nv: (jnp.minimum(i, nv[0] - 1), 0)),
                  pl.BlockSpec((1, 1, d, hid), lambda i, be, nv: (layer, be[i], 0, 0)),
                  pl.BlockSpec((1, 1, d, hid), lambda i, be, nv: (layer, be[i], 0, 0)),
                  pl.BlockSpec((1, 1, hid, d), lambda i, be, nv: (layer, be[i], 0, 0))],
        out_specs=pl.BlockSpec((bm, half), lambda i, be, nv: (i, 0)),
        scratch_shapes=[pltpu.VMEM((d, hid), BF16), pltpu.VMEM((d, hid), BF16), pltpu.VMEM((hid, d), BF16)],
    )
    return pl.pallas_call(
        _expert_kernel,
        grid_spec=grid_spec,
        out_shape=jax.ShapeDtypeStruct((rows, half), jnp.uint32),
        compiler_params=pltpu.CompilerParams(dimension_semantics=("arbitrary",),
                                             vmem_limit_bytes=VMEM_LIMIT),
        name="moe_experts",
    )(block_e, n_valid, x_sorted, w1, w3, w2)


def _combine_ple_kernel(npieces_ref, rows_ref, y_hbm, h_ref, meta_ref, p_ref, g_ref, wg_ref, wp_ref, o_ref,
                        ys_ref, sems):
    tile = pl.program_id(0)
    n_tiles = pl.num_programs(0)
    tm, d = h_ref.shape
    slot = tile % 2

    def copies(t, s, wait):
        _piece_copies(npieces_ref, rows_ref, t, ys_ref.at[s], y_hbm, sems.at[s], False, wait)

    @pl.when(tile == 0)
    def _():
        ys_ref[...] = jnp.zeros_like(ys_ref)
        copies(0, 0, False)

    @pl.when(tile + 1 < n_tiles)
    def _():
        copies(tile + 1, 1 - slot, False)

    copies(tile, slot, True)
    meta = meta_ref[...]
    w0 = meta[:, META_W0:META_W0 + 1]
    w1 = meta[:, META_W1:META_W1 + 1]
    p0 = meta[:, META_P0:META_P0 + 1].astype(jnp.int32)
    p1 = meta[:, META_P1:META_P1 + 1].astype(jnp.int32)
    r = lax.broadcasted_iota(jnp.int32, (tm, SORT_ROWS), 1)
    wmat = jnp.where(r == p0, w0, jnp.where(r == p1, w1, 0.0)).astype(BF16)
    ylo, yhi = _unpack_halves(ys_ref[slot])
    mix = lambda y: jnp.dot(wmat, y, preferred_element_type=F32)
    half = d // 2
    h = jnp.concatenate([h_ref[:, :half] + mix(ylo), h_ref[:, half:] + mix(yhi)], axis=1)
    gate = _sigmoid(jnp.dot(_rms(h, g_ref[...]).astype(BF16), wg_ref[...], preferred_element_type=F32))
    proj = jnp.dot(p_ref[...].astype(BF16), wp_ref[...], preferred_element_type=F32)
    o_ref[...] = h + gate * proj


def moe_combine_ple(n_pieces, piece_rows, y, h, meta, p, layer, g, w_gate, w_proj):
    n, d = h.shape
    pd = p.shape[2]
    tm = ROUTER_TM
    grid_spec = pltpu.PrefetchScalarGridSpec(
        num_scalar_prefetch=2,
        grid=(n // tm,),
        in_specs=[pl.BlockSpec(memory_space=pl.ANY),
                  pl.BlockSpec((tm, d), lambda i, *_: (i, 0)),
                  pl.BlockSpec((tm, ROUTER_LANES), lambda i, *_: (i, 0)),
                  pl.BlockSpec((None, tm, pd), lambda i, *_: (layer, i, 0)),
                  pl.BlockSpec((1, d), lambda i, *_: (0, 0)),
                  pl.BlockSpec((d, d), lambda i, *_: (0, 0)),
                  pl.BlockSpec((pd, d), lambda i, *_: (0, 0))],
        out_specs=pl.BlockSpec((tm, d), lambda i, *_: (i, 0)),
        scratch_shapes=[pltpu.VMEM((2, SORT_ROWS, d // 2), jnp.uint32), pltpu.SemaphoreType.DMA((2,))],
    )
    return pl.pallas_call(
        _combine_ple_kernel,
        grid_spec=grid_spec,
        out_shape=jax.ShapeDtypeStruct((n, d), F32),
        compiler_params=pltpu.CompilerParams(dimension_semantics=("arbitrary",),
                                             vmem_limit_bytes=VMEM_LIMIT),
        name="moe_combine_ple",
    )(n_pieces, piece_rows, y, h, meta, p, g.reshape(1, d), w_gate, w_proj)


def outproj_moe_embedding(a, w_out, res, ffn_norm, w_rg, b_rg, w_re, b_re, w1, w3, w2, p, ple_norm, w_ple_gate,
                          w_ple_proj, layer):
    n, d = res.shape
    pad = ROUTER_LANES - MOE_GROUPS - MOE_EXPERTS
    w_router = jnp.concatenate([w_rg, w_re, jnp.zeros((d, pad), F32)], axis=1)
    b_router = jnp.concatenate([b_rg, b_re, jnp.zeros((pad,), F32)]).reshape(1, ROUTER_LANES)
    h, xn, meta, post, tabs, cnt = outproj_router(a, w_out, res, ffn_norm, w_router, b_router)

    bm = MOE_BM
    nt = n // ROUTER_TM
    lanes = slice(EXP_LANE0, EXP_LANE0 + MOE_EXPERTS)
    totals = cnt[0, lanes].astype(jnp.int32)
    region = (totals + bm - 1) // bm * bm
    region_end = jnp.cumsum(region)
    region_start = region_end - region
    n_blocks = -(-(2 * n + nt * MOE_EXPERTS * (SEG_ALIGN - 1)) // bm) + MOE_EXPERTS
    block_row0 = jnp.arange(n_blocks, dtype=jnp.int32) * bm
    block_e = jnp.minimum(jnp.sum((block_row0[:, None] >= region_end[None, :]).astype(jnp.int32), axis=1),
                          MOE_EXPERTS - 1).astype(jnp.int32)
    n_valid = (region_end[-1:] // bm).astype(jnp.int32)
    tabs = tabs.reshape(nt, 8, ROUTER_LANES)[:, :, lanes].astype(jnp.int32)
    tab_cnt, tab_off = tabs[:, TAB_CNT], tabs[:, TAB_OFF]
    tab_seg = tabs[:, TAB_SEG] + region_start[None, :]
    piece_row0 = jnp.arange(MAX_PIECES, dtype=jnp.int32) * SEG_ALIGN
    row0 = piece_row0[None, :, None]
    owner = ((tab_off[:, None, :] <= row0) & (row0 < (tab_off + tab_cnt)[:, None, :])).astype(jnp.int32)
    of_owner = lambda t: jnp.sum(owner * t[:, None, :], axis=2)
    piece_rows = (of_owner(tab_seg) + piece_row0[None, :] - of_owner(tab_off)).reshape(-1)
    n_pieces = jnp.sum(tab_cnt, axis=1) // SEG_ALIGN

    x_sorted = moe_dispatch(n_pieces, piece_rows, region - totals, region_start + totals, n_valid, post, xn,
                            n_blocks * bm)
    y = moe_experts(x_sorted, block_e, n_valid, w1, w3, w2, layer)
    return moe_combine_ple(n_pieces, piece_rows, y, h, meta, p, layer, ple_norm, w_ple_gate, w_ple_proj)


def kernel(x, p, a_norm, a_w_in, a_conv, a_A_log, a_dt_bias, a_o_norm, a_w_out, kv_norm, w_kv, k_norm, b_norm, b_w_q, b_q_norm, b_w_out, ffn_norm, w_router_group, b_router_group, w_router_expert, b_router_expert, w1, w3, w2, ple_norm, w_ple_gate, w_ple_proj):
    b, s, d = x.shape
    n = b * s
    depth = p.shape[0]
    n_a = a_norm.shape[0]
    h = x.reshape(n, d)
    p_rows = p.reshape(depth, n, -1)
    kv = None
    for i in range(depth):
        if i < n_a:
            w_in = a_w_in[i]
            proj, gates_t = dn_inproj(h, a_norm[i], w_in[:, :DN_MAIN].astype(BF16), w_in[:, DN_MAIN:].T)
            o = deltanet(proj.reshape(b, s, DN_MAIN), gates_t.reshape(2 * DN_HEADS, b, s // DN_CHUNK, DN_CHUNK),
                         a_conv[i], a_A_log[i], a_dt_bias[i], a_o_norm[i])
            mixed, w_out = o.reshape(n, DN_V), a_w_out[i]
        else:
            bl = i - n_a
            q = norm_matmul(h, b_norm[bl], b_w_q[bl].astype(BF16), BF16)
            o = dilated_attention(q.reshape(b, s, -1), kv.reshape(b, s, -1), b_q_norm[bl], k_norm)
            mixed, w_out = o.reshape(n, GROUP_WIDTH), b_w_out[bl]
        h = outproj_moe_embedding(mixed, w_out.astype(BF16), h, ffn_norm[i], w_router_group[i], b_router_group[i],
                                  w_router_expert[i], b_router_expert[i], w1, w3, w2, p_rows, ple_norm[i],
                                  w_ple_gate[i].astype(BF16), w_ple_proj[i].astype(BF16), i)
        if i == n_a - 1:
            kv = norm_matmul(h, kv_norm, w_kv.astype(BF16), BF16)
    return h.reshape(b, s, d)
```

```python
import numpy as np
import jax
import jax.numpy as jnp
from jax import lax
from jax.experimental import pallas as pl
from jax.experimental.pallas import tpu as pltpu

F32 = jnp.float32
BF16 = jnp.bfloat16

NORM_EPS = 1e-6

DN_HEADS = 8
DN_DK = 128
DN_DV = 128
DN_CONV = 4
DN_CHUNK = 128
DN_SQUARINGS = DN_CHUNK.bit_length() - 2
DN_GROUP = 16
assert DN_CHUNK == DN_DK == DN_DV
DN_QK = DN_HEADS * DN_DK
DN_V = DN_HEADS * DN_DV
DN_MAIN = 2 * DN_QK + 2 * DN_V

DIL_CONFIGS = ((128, 1), (512, 4), (2048, 16))
N_ATT_GROUPS = len(DIL_CONFIGS)
HEAD_DIM = 128
Q_PER_GROUP = 4
KV_PER_GROUP = 2
Q_REP = Q_PER_GROUP // KV_PER_GROUP
ATT_BLOCK = 128
ALIBI_MAX = 8.0
GROUP_WIDTH = Q_PER_GROUP * HEAD_DIM

MOE_GROUPS = 4
MOE_EPG = 8
MOE_EXPERTS = MOE_GROUPS * MOE_EPG
MOE_HIDDEN = 512
MOE_BM = 512
ROUTER_LANES = 128
EXP_LANE0 = MOE_GROUPS

LANES = 128
VMEM_LIMIT = 48 * 1024 * 1024
VMEM_LIMIT_DELTANET = 56 * 1024 * 1024


def _alibi_slopes():
    n = N_ATT_GROUPS * Q_PER_GROUP
    s = 2.0 ** (-ALIBI_MAX * np.arange(1, n + 1) / n)
    return s.reshape(N_ATT_GROUPS, KV_PER_GROUP, Q_REP)


def _rms(x, g):
    ms = jnp.mean(x * x, axis=-1, keepdims=True)
    return x * lax.rsqrt(ms + NORM_EPS) * g


def _dot(a, b):
    return jnp.dot(a.astype(BF16), b.astype(BF16), preferred_element_type=F32)


def _dot_nt(a, b):
    return lax.dot_general(a.astype(BF16), b.astype(BF16), (((1,), (1,)), ((), ())),
                           preferred_element_type=F32)


def _dot_tn(a, b):
    return lax.dot_general(a.astype(BF16), b.astype(BF16), (((0,), (0,)), ((), ())),
                           preferred_element_type=F32)


def _split2(x):
    hi = x.astype(BF16)
    lo = (x - hi.astype(F32)).astype(BF16)
    return hi, lo


def _split3(x):
    hi = x.astype(BF16)
    r = x - hi.astype(F32)
    mid = r.astype(BF16)
    lo = (r - mid.astype(F32)).astype(BF16)
    return hi, mid, lo


def _dot_exact01(x, sel):
    hi, mid, lo = _split3(x)
    d = lambda p: jnp.dot(p, sel, preferred_element_type=F32)
    return d(hi) + d(mid) + d(lo)


def _aligned(i, m):
    return i if isinstance(i, int) else pl.multiple_of(i, m)


def _sigmoid(x):
    return 1.0 / (1.0 + jnp.exp(-x))


def _silu(x):
    return x * _sigmoid(x)


ROW_TILE = 512


def _row_tiled_call(body, name, n, row_inputs, resident_inputs, out_widths, out_dtypes, extra_out_specs=(),
                    extra_out_shapes=()):
    tm = ROW_TILE
    row_spec = lambda width: pl.BlockSpec((tm, width), lambda i: (i, 0))
    whole = lambda a: pl.BlockSpec(a.shape, lambda i: (0,) * a.ndim)

    def in_row_spec(a):
        if isinstance(a, tuple):
            arr, layer = a
            return pl.BlockSpec((None, tm, arr.shape[2]), lambda i: (layer, i, 0))
        return row_spec(a.shape[1])

    row_specs = [in_row_spec(a) for a in row_inputs]
    row_inputs = [a[0] if isinstance(a, tuple) else a for a in row_inputs]
    return pl.pallas_call(
        body,
        grid=(n // tm,),
        in_specs=row_specs + [whole(a) for a in resident_inputs],
        out_specs=[row_spec(w) for w in out_widths] + list(extra_out_specs),
        out_shape=[jax.ShapeDtypeStruct((n, w), dt) for w, dt in zip(out_widths, out_dtypes)]
        + list(extra_out_shapes),
        compiler_params=pltpu.CompilerParams(dimension_semantics=("parallel",), vmem_limit_bytes=VMEM_LIMIT),
        name=name,
    )(*row_inputs, *resident_inputs)


def _nm_kernel(x_ref, g_ref, w_ref, o_ref):
    xn = _rms(x_ref[...], g_ref[...]).astype(BF16)
    o_ref[...] = jnp.dot(xn, w_ref[...], preferred_element_type=F32).astype(o_ref.dtype)


def norm_matmul(x, g, w, out_dtype):
    n, k = x.shape
    return _row_tiled_call(_nm_kernel, "norm_matmul", n, [x], [g.reshape(1, k), w], [w.shape[1]], [out_dtype])[0]


def _dn_inproj_kernel(x_ref, g_ref, w_ref, wgt_ref, o_ref, gt_ref):
    xn = _rms(x_ref[...], g_ref[...])
    xh, xl = _split2(xn)
    wh, wl = _split2(wgt_ref[...])
    gt_ref[...] = _dot_nt(wh, xh) + _dot_nt(wh, xl) + _dot_nt(wl, xh)
    o_ref[...] = jnp.dot(xh, w_ref[...], preferred_element_type=F32).astype(o_ref.dtype)


def dn_inproj(x, g, w_main, w_gates_t):
    n, k = x.shape
    ng = w_gates_t.shape[0]
    return _row_tiled_call(_dn_inproj_kernel, "dn_inproj", n, [x], [g.reshape(1, k), w_main, w_gates_t],
                           [w_main.shape[1]], [BF16],
                           extra_out_specs=[pl.BlockSpec((ng, ROW_TILE), lambda i: (0, i))],
                           extra_out_shapes=[jax.ShapeDtypeStruct((ng, n), F32)])


DN_PIECE = 256
DN_HALO = 8
DN_HB = 4
assert DN_HEADS % DN_HB == 0


def _deltanet_kernel(alog_ref, dtb_ref, q_ref, k_ref, v_ref, z_ref, cq_ref, ck_ref, cv_ref,
                     bpre_ref, apre_ref, onorm_ref, o_ref,
                     xf_ref, qs_ref, ks_ref, vs_ref, gcum_ref, betac_ref, gc_ref,
                     pm_ref, rq_ref, qq_ref, o0_ref, elast_ref):
    seq = q_ref.shape[1]
    c = DN_CHUNK
    n_chunks = seq // c
    assert 2 * n_chunks <= c
    head0 = pl.program_id(1) * DN_HB
    ki = lax.broadcasted_iota(jnp.int32, (c, c), 0)
    ji = lax.broadcasted_iota(jnp.int32, (c, c), 1)
    upper = jnp.where(ki <= ji, 1.0, 0.0).astype(BF16)
    causal = ki >= ji
    strict = ki > ji
    onorm = onorm_ref[...]

    xf_ref[0:DN_HALO, :] = jnp.zeros((DN_HALO, DN_DK), F32)

    def conv_silu(x_ref, w_ref, hb, finish, out_ref):
        cols = slice(hb * DN_DK, (hb + 1) * DN_DK)
        w = w_ref[:, cols]
        pieces = [slice(p * DN_PIECE, (p + 1) * DN_PIECE) for p in range(seq // DN_PIECE)]
        for rows in pieces:
            xf_ref[DN_HALO + rows.start:DN_HALO + rows.stop, :] = x_ref[0, rows, cols].astype(F32)
        for rows in pieces:
            acc = xf_ref[DN_HALO + rows.start:DN_HALO + rows.stop, :] * w[DN_CONV - 1:DN_CONV, :]
            for j in range(1, DN_CONV):
                acc = acc + xf_ref[DN_HALO + rows.start - j:DN_HALO + rows.stop - j, :] * w[DN_CONV - 1 - j:DN_CONV - j, :]
            out_ref[rows, :] = finish(_silu(acc))

    def l2n(scale):
        return lambda x: x * (lax.rsqrt(jnp.sum(x * x, axis=-1, keepdims=True) + NORM_EPS) * scale)

    def prologue(hb):
        conv_silu(q_ref, cq_ref, hb, l2n(DN_DK ** -0.5), qs_ref)
        conv_silu(k_ref, ck_ref, hb, l2n(1.0), ks_ref)
        conv_silu(v_ref, cv_ref, hb, lambda x: x, vs_ref)
        beta = _sigmoid(bpre_ref[hb, 0])
        a = apre_ref[hb, 0] + dtb_ref[head0 + hb]
        softplus = jnp.maximum(a, 0.0) + jnp.log(1.0 + jnp.exp(-jnp.abs(a)))
        g_log = -jnp.exp(jnp.full(a.shape, alog_ref[head0 + hb], F32)) * softplus
        gcum = _dot_exact01(g_log, upper)
        gcum_ref[...] = gcum
        t = jnp.concatenate([beta, gcum, jnp.zeros((c - 2 * n_chunks, c), F32)], axis=0).T
        for ci in range(n_chunks):
            betac_ref[ci * c:(ci + 1) * c, :] = jnp.broadcast_to(t[:, ci:ci + 1], (c, DN_DV))
            gc_ref[ci * c:(ci + 1) * c, :] = jnp.broadcast_to(t[:, n_chunks + ci:n_chunks + ci + 1], (c, DN_DV))

    def prepare(hb, cis):
        each = lambda f, *ls: [f(*xs) for xs in zip(*ls)]
        rows = [pl.ds(_aligned(ci * c, c), c) for ci in cis]
        qc = [qs_ref[r, :] for r in rows]
        kc = [ks_ref[r, :] for r in rows]
        vc = [vs_ref[r, :] for r in rows]
        beta_c = [betac_ref[r, :] for r in rows]
        g_c = [gc_ref[r, :] for r in rows]
        g_j = [jnp.broadcast_to(gcum_ref[pl.ds(ci, 1), :], (c, c)) for ci in cis]
        decay = each(lambda gi, gj: jnp.exp(jnp.where(causal, gi - gj, -jnp.inf)), g_c, g_j)
        kq = each(lambda k, q: _dot_nt(jnp.concatenate([k, q], axis=0), k), kc, qc)
        m = each(lambda b, x, d: jnp.where(strict, -(b * x[:c, :] * d), 0.0), beta_c, kq, decay)
        pw = each(lambda x: _dot(x, x), m)
        r = m
        for _ in range(DN_SQUARINGS - 1):
            xs = each(lambda p_, r_: _dot(p_, jnp.concatenate([p_, r_], axis=1)), pw, r)
            r = each(lambda r_, p_, x: r_ + p_ + x[:, c:], r, pw, xs)
            pw = [x[:, :c] for x in xs]
        xs = each(_dot, pw, r)
        r = each(lambda r_, p_, x: r_ + p_ + x, r, pw, xs)
        e_g = [jnp.exp(g) for g in g_c]
        rhs = each(lambda b, v, e, k: jnp.concatenate([b * v, b * e * k], axis=1), beta_c, vc, e_g, kc)
        sol = each(lambda rh, r_: rh + _dot(r_, rh), rhs, r)
        attn = each(lambda x, d: jnp.where(causal, x[c:, :] * d, 0.0), kq, decay)
        k_d = each(lambda k, g: k * jnp.exp(jnp.broadcast_to(g[c - 1:c, :], (c, DN_DV)) - g), kc, g_c)
        kt = each(_dot_tn, k_d, sol)
        at = each(_dot, attn, sol)
        for i, (ci, r_) in enumerate(zip(cis, rows)):
            qq_ref[hb, r_, :] = kt[i][:, :DN_DV]
            pm_ref[hb, r_, :] = kt[i][:, DN_DV:].astype(BF16)
            o0_ref[hb, r_, :] = at[i][:, :DN_DV]
            rq_ref[hb, r_, :] = (qc[i] * e_g[i] - at[i][:, DN_DV:]).astype(BF16)
            elast_ref[hb, pl.ds(_aligned(ci * 8, 8), 8), :] = jnp.exp(
                jnp.broadcast_to(g_c[i][c - 1:c, :], (8, DN_DV)))

    group = min(DN_GROUP, n_chunks)
    assert n_chunks % group == 0
    for hb in range(DN_HB):
        prologue(hb)
        if group == n_chunks:
            prepare(hb, list(range(n_chunks)))
        else:
            def prepare_group(gi, carry, hb=hb):
                prepare(hb, [gi * group + k for k in range(group)])
                return carry

            lax.fori_loop(0, n_chunks // group, prepare_group, 0)

    def chunk_step(ci, states):
        rows = pl.ds(pl.multiple_of(ci * c, c), c)
        xs = [_dot(jnp.concatenate([pm_ref[hb, rows, :], rq_ref[hb, rows, :]], axis=0), states[hb])
              for hb in range(DN_HB)]
        new_states = []
        for hb in range(DN_HB):
            cols = slice(hb * DN_DV, (hb + 1) * DN_DV)
            e_last = jnp.broadcast_to(elast_ref[hb, pl.ds(pl.multiple_of(ci * 8, 8), 1), :], (DN_DK, DN_DV))
            new_states.append(e_last * states[hb] - xs[hb][:c, :] + qq_ref[hb, rows, :])
            o = xs[hb][c:, :] + o0_ref[hb, rows, :]
            zc = z_ref[0, rows, cols].astype(F32)
            o_ref[0, rows, cols] = (_rms(o, onorm) * _silu(zc)).astype(o_ref.dtype)
        return tuple(new_states)

    lax.fori_loop(0, n_chunks, chunk_step, tuple(jnp.zeros((DN_DK, DN_DV), F32) for _ in range(DN_HB)))


def deltanet(proj, gates_t, conv_w, a_log, dt_bias, o_norm):
    b, s, _ = proj.shape
    ng = DN_HEADS // DN_HB
    nc = s // DN_CHUNK
    wide = DN_HB * DN_DK
    col = lambda off: pl.BlockSpec((1, s, wide), lambda bi, hi: (bi, 0, off + hi))
    cw = lambda off: pl.BlockSpec((DN_CONV, wide), lambda bi, hi: (0, off + hi))
    gate = lambda off: pl.BlockSpec((DN_HB, 1, nc, DN_CHUNK), lambda bi, hi: (off + hi, bi, 0, 0))
    smem = pl.BlockSpec(memory_space=pltpu.SMEM)
    per_head = lambda dt: pltpu.VMEM((DN_HB, s, DN_DV), dt)
    return pl.pallas_call(
        _deltanet_kernel,
        grid=(b, ng),
        in_specs=[smem, smem, col(0), col(ng), col(2 * ng), col(3 * ng), cw(0), cw(ng), cw(2 * ng),
                  gate(0), gate(ng), pl.BlockSpec((1, DN_DV), lambda bi, hi: (0, 0))],
        out_specs=pl.BlockSpec((1, s, wide), lambda bi, hi: (bi, 0, hi)),
        out_shape=jax.ShapeDtypeStruct((b, s, DN_V), BF16),
        scratch_shapes=[pltpu.VMEM((DN_HALO + s, DN_DK), F32),
                        pltpu.VMEM((s, DN_DK), F32), pltpu.VMEM((s, DN_DK), F32), pltpu.VMEM((s, DN_DV), F32),
                        pltpu.VMEM((nc, DN_CHUNK), F32), pltpu.VMEM((s, DN_DV), F32), pltpu.VMEM((s, DN_DV), F32),
                        per_head(BF16), per_head(BF16), per_head(F32), per_head(F32),
                        pltpu.VMEM((DN_HB, nc * 8, DN_DV), F32)],
        compiler_params=pltpu.CompilerParams(dimension_semantics=("parallel", "parallel"),
                                             vmem_limit_bytes=VMEM_LIMIT_DELTANET),
        name="deltanet",
    )(a_log, dt_bias, proj, proj, proj, proj, conv_w, conv_w, conv_w, gates_t, gates_t, o_norm.reshape(1, DN_DV))


ATT_PIECE = 256
ATT_M_INIT = -1e30


def _attention_kernel(q_ref, kv_ref, qn_ref, kn_ref, o_ref, qf_ref, kf_ref, vf_ref, acc_ref, m_ref, l_ref):
    seq = q_ref.shape[1]
    grp = pl.program_id(1)
    hd = HEAD_DIM
    blk = ATT_BLOCK
    slopes = _alibi_slopes()

    qg = qn_ref[0] * (hd ** -0.5)
    kg = kn_ref[0]

    def prep(pi, carry):
        r0 = pl.multiple_of(pi * ATT_PIECE, ATT_PIECE)
        rows = pl.ds(r0, ATT_PIECE)
        for j in range(Q_PER_GROUP):
            cols = slice(j * hd, (j + 1) * hd)
            qf_ref[j, rows, :] = _rms(q_ref[0, rows, cols].astype(F32), 1.0) * qg
        for j in range(KV_PER_GROUP):
            cols = slice(j * hd, (j + 1) * hd)
            kf_ref[j, rows, :] = _rms(kv_ref[0, rows, cols].astype(F32), 1.0) * kg
            vcols = slice((KV_PER_GROUP + j) * hd, (KV_PER_GROUP + j + 1) * hd)
            vf_ref[j, rows, :] = kv_ref[0, rows, vcols].astype(F32)
        return carry

    lax.fori_loop(0, seq // ATT_PIECE, prep, 0)

    @pl.when(grp == 0)
    def _():
        def init(pi, carry):
            r0 = pl.multiple_of(pi * ATT_PIECE, ATT_PIECE)
            rows = pl.ds(r0, ATT_PIECE)
            for j in range(Q_PER_GROUP):
                acc_ref[j, rows, :] = jnp.zeros((ATT_PIECE, hd), F32)
                l_ref[j, rows, :] = jnp.zeros((ATT_PIECE, hd), F32)
                m_ref[j, rows, :] = jnp.full((ATT_PIECE, hd), ATT_M_INIT, F32)
            return carry

        lax.fori_loop(0, seq // ATT_PIECE, init, 0)

    def rows_of(start, size, dil):
        return pl.ds(start, size) if dil == 1 else pl.ds(start, size, stride=dil)

    def attend(g, dil, q_start, k_start, nk):
        qrows = rows_of(q_start, blk, dil)
        krows = rows_of(k_start, nk, dil)
        qi = lax.broadcasted_iota(jnp.int32, (blk, nk), 0)
        kidx = lax.broadcasted_iota(jnp.int32, (blk, nk), 1)
        dist = (nk - blk) + qi - kidx
        valid = (dist >= 0) & (dist <= blk)
        distf = dist.astype(F32)
        kv_heads = range(KV_PER_GROUP)
        heads = [(kvh, rep) for kvh in kv_heads for rep in range(Q_REP)]
        vb = [vf_ref[kvh, krows, :].astype(BF16) for kvh in kv_heads]
        sc2 = [_dot_nt(jnp.concatenate([qf_ref[kvh * Q_REP + rep, qrows, :] for rep in range(Q_REP)], axis=0),
                       kf_ref[kvh, krows, :]) for kvh in kv_heads]
        sc = [jnp.where(valid, sc2[kvh][rep * blk:(rep + 1) * blk, :] - float(slopes[g, kvh, rep] * dil) * distf,
                        -jnp.inf) for kvh, rep in heads]
        m_old = [m_ref[kvh * Q_REP + rep, qrows, :] for kvh, rep in heads]
        m_new = [jnp.maximum(mo, jnp.max(s, axis=-1, keepdims=True)) for mo, s in zip(m_old, sc)]
        alpha = [jnp.exp(mo - mn) for mo, mn in zip(m_old, m_new)]
        p = [jnp.exp(s - mn[:, 0:1]) for s, mn in zip(sc, m_new)]
        pv2 = [jnp.dot(jnp.concatenate([p[kvh * Q_REP + rep].astype(BF16) for rep in range(Q_REP)], axis=0),
                       vb[kvh], preferred_element_type=F32) for kvh in kv_heads]
        for i, (kvh, rep) in enumerate(heads):
            j = kvh * Q_REP + rep
            l_ref[j, qrows, :] = alpha[i] * l_ref[j, qrows, :] + jnp.sum(p[i], axis=-1, keepdims=True)
            acc_ref[j, qrows, :] = alpha[i] * acc_ref[j, qrows, :] + pv2[kvh][rep * blk:(rep + 1) * blk, :]
            m_ref[j, qrows, :] = m_new[i]

    for g, (window, dil) in enumerate(DIL_CONFIGS):
        sub_len = seq // dil
        nblk = sub_len // blk

        @pl.when(grp == g)
        def _(g=g, dil=dil, nblk=nblk):
            def residue(res, carry):
                attend(g, dil, res, res, blk)
                if nblk > 1:
                    def later(n, c2):
                        attend(g, dil, res + n * blk * dil, res + (n - 1) * blk * dil, 2 * blk)
                        return c2
                    lax.fori_loop(1, nblk, later, 0)
                return carry

            lax.fori_loop(0, dil, residue, 0)

    @pl.when(grp == N_ATT_GROUPS - 1)
    def _():
        def finish(pi, carry):
            r0 = pl.multiple_of(pi * ATT_PIECE, ATT_PIECE)
            rows = pl.ds(r0, ATT_PIECE)
            for j in range(Q_PER_GROUP):
                cols = slice(j * hd, (j + 1) * hd)
                o_ref[0, rows, cols] = (acc_ref[j, rows, :] / l_ref[j, rows, :]).astype(o_ref.dtype)
            return carry

        lax.fori_loop(0, seq // ATT_PIECE, finish, 0)


def dilated_attention(q, kv, q_norm, k_norm):
    b, s, _ = q.shape
    gw = GROUP_WIDTH
    return pl.pallas_call(
        _attention_kernel,
        grid=(b, N_ATT_GROUPS),
        in_specs=[pl.BlockSpec((1, s, gw), lambda bi, gi: (bi, 0, gi)),
                  pl.BlockSpec((1, s, gw), lambda bi, gi: (bi, 0, gi)),
                  pl.BlockSpec((1, 1, HEAD_DIM), lambda bi, gi: (gi, 0, 0)),
                  pl.BlockSpec((1, 1, HEAD_DIM), lambda bi, gi: (gi, 0, 0))],
        out_specs=pl.BlockSpec((1, s, gw), lambda bi, gi: (bi, 0, 0)),
        out_shape=jax.ShapeDtypeStruct((b, s, gw), BF16),
        scratch_shapes=[pltpu.VMEM((Q_PER_GROUP, s, HEAD_DIM), F32),
                        pltpu.VMEM((KV_PER_GROUP, s, HEAD_DIM), F32),
                        pltpu.VMEM((KV_PER_GROUP, s, HEAD_DIM), F32),
                        pltpu.VMEM((Q_PER_GROUP, s, HEAD_DIM), F32),
                        pltpu.VMEM((Q_PER_GROUP, s, HEAD_DIM), F32),
                        pltpu.VMEM((Q_PER_GROUP, s, HEAD_DIM), F32)],
        compiler_params=pltpu.CompilerParams(dimension_semantics=("parallel", "arbitrary"),
                                             vmem_limit_bytes=VMEM_LIMIT),
        name="dilated_attention",
    )(q, kv, q_norm.reshape(N_ATT_GROUPS, 1, HEAD_DIM), k_norm.reshape(N_ATT_GROUPS, 1, HEAD_DIM))


ROUTER_TM = 512
SEG_ALIGN = 16
SORT_ROWS = 2 * ROUTER_TM + 512
assert SORT_ROWS >= 2 * ROUTER_TM + MOE_EXPERTS * (SEG_ALIGN - 1) and SORT_ROWS % LANES == 0
META_W0, META_W1, META_P0, META_P1 = 0, 1, 2, 3
TAB_CNT, TAB_OFF, TAB_SEG = 0, 1, 2


def _router_kernel(a_ref, res_ref, wo_ref, g_ref, w_ref, b_ref, h_ref, xn_ref, meta_ref, post_ref, tab_ref, cnt_ref,
                   carry_ref):
    tm = res_ref.shape[0]

    @pl.when(pl.program_id(0) == 0)
    def _():
        carry_ref[...] = jnp.zeros_like(carry_ref)

    h = res_ref[...] + jnp.dot(a_ref[...], wo_ref[...], preferred_element_type=F32)
    h_ref[...] = h
    xn = _rms(h, g_ref[...])
    xn_ref[...] = xn.astype(BF16)
    xh, xl = _split2(xn)
    wh, wl = _split2(w_ref[...])
    d = lambda a, bb: jnp.dot(a, bb, preferred_element_type=F32)
    logits = d(xh, wh) + d(xh, wl) + d(xl, wh) + b_ref[...]

    lane = lax.broadcasted_iota(jnp.int32, (tm, ROUTER_LANES), 1)
    big = jnp.int32(ROUTER_LANES)
    first_where = lambda cond: jnp.min(jnp.where(cond, lane, big), axis=-1, keepdims=True)

    gl = jnp.where(lane < MOE_GROUPS, logits, -jnp.inf)
    ge = jnp.exp(gl - jnp.max(gl, axis=-1, keepdims=True))
    gp = ge / jnp.sum(ge, axis=-1, keepdims=True)
    g_w = jnp.max(gp, axis=-1, keepdims=True)
    g_idx = first_where(gp == g_w)

    lo = EXP_LANE0 + g_idx * MOE_EPG
    in_group = (lane >= lo) & (lane < lo + MOE_EPG)
    el = jnp.where(in_group, logits, -jnp.inf)
    ee = jnp.exp(el - jnp.max(el, axis=-1, keepdims=True))
    ep = ee / jnp.sum(ee, axis=-1, keepdims=True)
    p0 = jnp.max(jnp.where(in_group, ep, -1.0), axis=-1, keepdims=True)
    i0 = first_where(in_group & (ep == p0))
    rest = in_group & (lane != i0)
    p1 = jnp.max(jnp.where(rest, ep, -1.0), axis=-1, keepdims=True)
    i1 = first_where(rest & (ep == p1))
    w0 = g_w * p0 / (p0 + p1)
    w1 = g_w * p1 / (p0 + p1)

    oh0 = jnp.where(lane == i0, 1.0, 0.0)
    oh1 = jnp.where(lane == i1, 1.0, 0.0)
    both = oh0 + oh1
    ti = lax.broadcasted_iota(jnp.int32, (tm, tm), 0)
    tj = lax.broadcasted_iota(jnp.int32, (tm, tm), 1)
    before = jnp.where(tj < ti, 1.0, 0.0).astype(BF16)
    within = jnp.dot(before, both.astype(BF16), preferred_element_type=F32)
    cnt = jnp.sum(both, axis=0, keepdims=True)
    cnt_pad = jnp.floor((cnt + (SEG_ALIGN - 1)) * (1.0 / SEG_ALIGN)) * SEG_ALIGN
    li = lax.broadcasted_iota(jnp.int32, (ROUTER_LANES, ROUTER_LANES), 0)
    lj = lax.broadcasted_iota(jnp.int32, (ROUTER_LANES, ROUTER_LANES), 1)
    earlier = jnp.where(li < lj, 1.0, 0.0).astype(BF16)
    tile_off = _dot_exact01(jnp.broadcast_to(cnt_pad, (8, ROUTER_LANES)), earlier)[0:1, :]
    row = tile_off + within
    pos0 = jnp.sum(row * oh0, axis=-1, keepdims=True)
    pos1 = jnp.sum(row * oh1, axis=-1, keepdims=True)
    seg_off = carry_ref[...]
    total = seg_off + cnt_pad
    carry_ref[...] = total
    cnt_ref[...] = jnp.broadcast_to(total, cnt_ref.shape)

    sub = lax.broadcasted_iota(jnp.int32, (8, ROUTER_LANES), 0)
    tab_ref[...] = jnp.where(sub == TAB_CNT, cnt_pad, jnp.where(sub == TAB_OFF, tile_off,
                                                                jnp.where(sub == TAB_SEG, seg_off, 0.0)))
    meta = jnp.zeros((tm, ROUTER_LANES), F32)
    for idx, val in ((META_W0, w0), (META_W1, w1), (META_P0, pos0), (META_P1, pos1)):
        meta = jnp.where(lane == idx, val, meta)
    meta_ref[...] = meta
    post_ref[0] = meta.T[0:8, :]


def outproj_router(a, w_out, res, g, w_router, b_router):
    n, k = res.shape
    ka = a.shape[1]
    tm = ROUTER_TM
    nt = n // tm
    return pl.pallas_call(
        _router_kernel,
        grid=(nt,),
        in_specs=[pl.BlockSpec((tm, ka), lambda i: (i, 0)),
                  pl.BlockSpec((tm, k), lambda i: (i, 0)),
                  pl.BlockSpec((ka, k), lambda i: (0, 0)),
                  pl.BlockSpec((1, k), lambda i: (0, 0)),
                  pl.BlockSpec((k, ROUTER_LANES), lambda i: (0, 0)),
                  pl.BlockSpec((1, ROUTER_LANES), lambda i: (0, 0))],
        out_specs=[pl.BlockSpec((tm, k), lambda i: (i, 0)),
                   pl.BlockSpec((tm, k), lambda i: (i, 0)),
                   pl.BlockSpec((tm, ROUTER_LANES), lambda i: (i, 0)),
                   pl.BlockSpec((1, 8, tm), lambda i: (i, 0, 0)),
                   pl.BlockSpec((8, ROUTER_LANES), lambda i: (i, 0)),
                   pl.BlockSpec((8, ROUTER_LANES), lambda i: (0, 0))],
        out_shape=[jax.ShapeDtypeStruct((n, k), F32),
                   jax.ShapeDtypeStruct((n, k), BF16),
                   jax.ShapeDtypeStruct((n, ROUTER_LANES), F32),
                   jax.ShapeDtypeStruct((nt, 8, tm), F32),
                   jax.ShapeDtypeStruct((nt * 8, ROUTER_LANES), F32),
                   jax.ShapeDtypeStruct((8, ROUTER_LANES), F32)],
        scratch_shapes=[pltpu.VMEM((1, ROUTER_LANES), F32)],
        compiler_params=pltpu.CompilerParams(dimension_semantics=("arbitrary",),
                                             vmem_limit_bytes=VMEM_LIMIT),
        name="outproj_router",
    )(a, res, w_out, g.reshape(1, k), w_router, b_router)


def _pack_halves(x):
    k = x.shape[1] // 2
    lo = pltpu.bitcast(x[:, :k].astype(BF16).astype(F32), jnp.uint32)
    hi = pltpu.bitcast(x[:, k:].astype(BF16).astype(F32), jnp.uint32)
    return (hi & jnp.uint32(0xFFFF0000)) | (lo >> 16)


def _unpack_halves(w):
    lo = pltpu.bitcast(w << 16, F32)
    hi = pltpu.bitcast(w & jnp.uint32(0xFFFF0000), F32)
    return lo.astype(BF16), hi.astype(BF16)


SEG_PIECE = SEG_ALIGN


def _segment_copies(rows_of, src_ref, dst_ref, sem, wait):
    def piece(s, d, rows):
        cp = pltpu.make_async_copy(src_ref.at[pl.ds(pl.multiple_of(s, SEG_ALIGN), rows), :],
                                   dst_ref.at[pl.ds(pl.multiple_of(d, SEG_ALIGN), rows), :], sem)
        cp.wait() if wait else cp.start()

    def per_expert(e, carry):
        cnt, s0, d0 = rows_of(e)
        n_full = cnt // SEG_PIECE

        def full_piece(j, c2):
            piece(s0 + j * SEG_PIECE, d0 + j * SEG_PIECE, SEG_PIECE)
            return c2

        lax.fori_loop(0, n_full, full_piece, 0)

        @pl.when(cnt % SEG_PIECE != 0)
        def _():
            piece(s0 + n_full * SEG_PIECE, d0 + n_full * SEG_PIECE, SEG_ALIGN)

        return carry

    lax.fori_loop(0, MOE_EXPERTS, per_expert, 0)


MAX_PIECES = SORT_ROWS // SEG_ALIGN


def _piece_copies(npieces_ref, rows_ref, tile, vmem_ref, hbm_ref, sem, to_hbm, wait):
    def piece(j, priority):
        v = vmem_ref.at[pl.ds(pl.multiple_of(j * SEG_ALIGN, SEG_ALIGN), SEG_ALIGN), :]
        hb = hbm_ref.at[pl.ds(pl.multiple_of(rows_ref[tile * MAX_PIECES + j], SEG_ALIGN), SEG_ALIGN), :]
        cp = pltpu.make_async_copy(v, hb, sem) if to_hbm else pltpu.make_async_copy(hb, v, sem)
        cp.wait() if wait else cp.start(priority=priority)

    def pair(jj, carry):
        piece(2 * jj, 0)
        piece(2 * jj + 1, 1)
        return carry

    n = npieces_ref[tile]
    lax.fori_loop(0, n // 2, pair, 0)

    @pl.when(n % 2 == 1)
    def _():
        piece(n - 1, 0)


def _dispatch_kernel(npieces_ref, rows_ref, tail_cnt_ref, tail_dst_ref, nvalid_ref, post_ref, x_ref, out_hbm,
                     xs_ref, zero_ref, sems):
    tile = pl.program_id(0)
    tm = x_ref.shape[0]
    bm = zero_ref.shape[0]
    sem = sems.at[2]
    n_blocks = out_hbm.shape[0] // bm

    @pl.when(tile == 0)
    def _():
        zero_ref[...] = jnp.zeros_like(zero_ref)
        tails = lambda e: (tail_cnt_ref[e], 0, tail_dst_ref[e])

        def unused_block(wait):
            def body(blk, carry):
                cp = pltpu.make_async_copy(zero_ref, out_hbm.at[pl.ds(pl.multiple_of(blk * bm, bm), bm), :], sem)
                cp.wait() if wait else cp.start()
                return carry
            return body

        _segment_copies(tails, zero_ref, out_hbm, sem, wait=False)
        lax.fori_loop(nvalid_ref[0], n_blocks, unused_block(False), 0)
        _segment_copies(tails, zero_ref, out_hbm, sem, wait=True)
        lax.fori_loop(nvalid_ref[0], n_blocks, unused_block(True), 0)

    n_tiles = pl.num_programs(0)
    slot = tile % 2

    def copies(t, s, wait):
        _piece_copies(npieces_ref, rows_ref, t, xs_ref.at[s], out_hbm, sems.at[s], True, wait)

    @pl.when(tile >= 2)
    def _():
        copies(tile - 2, slot, True)

    post = post_ref[0]
    p0 = post[META_P0:META_P0 + 1, :].astype(jnp.int32)
    p1 = post[META_P1:META_P1 + 1, :].astype(jnp.int32)
    r = lax.broadcasted_iota(jnp.int32, (SORT_ROWS, tm), 0)
    sel = jnp.where(r == p0, 1.0, jnp.where(r == p1, 1.0, 0.0)).astype(BF16)
    xs_ref[slot] = _pack_halves(jnp.dot(sel, x_ref[...], preferred_element_type=F32))
    copies(tile, slot, False)

    @pl.when(tile == n_tiles - 1)
    def _():
        @pl.when(tile >= 1)
        def _():
            copies(tile - 1, 1 - slot, True)

        copies(tile, slot, True)


def moe_dispatch(n_pieces, piece_rows, tail_cnt, tail_dst, n_valid, post, xn, rows):
    n, d = xn.shape
    tm = ROUTER_TM
    grid_spec = pltpu.PrefetchScalarGridSpec(
        num_scalar_prefetch=5,
        grid=(n // tm,),
        in_specs=[pl.BlockSpec((1, 8, tm), lambda i, *_: (i, 0, 0)),
                  pl.BlockSpec((tm, d), lambda i, *_: (i, 0))],
        out_specs=pl.BlockSpec(memory_space=pl.ANY),
        scratch_shapes=[pltpu.VMEM((2, SORT_ROWS, d // 2), jnp.uint32), pltpu.VMEM((MOE_BM, d // 2), jnp.uint32),
                        pltpu.SemaphoreType.DMA((3,))],
    )
    return pl.pallas_call(
        _dispatch_kernel,
        grid_spec=grid_spec,
        out_shape=jax.ShapeDtypeStruct((rows, d // 2), jnp.uint32),
        compiler_params=pltpu.CompilerParams(dimension_semantics=("arbitrary",),
                                             vmem_limit_bytes=VMEM_LIMIT),
        name="moe_dispatch",
    )(n_pieces, piece_rows, tail_cnt, tail_dst, n_valid, post, xn)


def _expert_kernel(be_ref, nvalid_ref, x_ref, w1_ref, w3_ref, w2_ref, y_ref, w1b_ref, w3b_ref, w2b_ref):
    i = pl.program_id(0)

    @pl.when(i < nvalid_ref[0])
    def _():
        @pl.when(jnp.logical_or(i == 0, be_ref[i] != be_ref[jnp.maximum(i - 1, 0)]))
        def _():
            w1b_ref[...] = w1_ref[0, 0].astype(BF16)
            w3b_ref[...] = w3_ref[0, 0].astype(BF16)
            w2b_ref[...] = w2_ref[0, 0].astype(BF16)

        xlo, xhi = _unpack_halves(x_ref[...])
        half = xlo.shape[1]
        up = lambda w_ref: (jnp.dot(xlo, w_ref[:half, :], preferred_element_type=F32)
                            + jnp.dot(xhi, w_ref[half:, :], preferred_element_type=F32))
        hdn = (_silu(up(w1b_ref)) * up(w3b_ref)).astype(BF16)
        y_ref[...] = _pack_halves(jnp.dot(hdn, w2b_ref[...], preferred_element_type=F32))

    @pl.when(i >= nvalid_ref[0])
    def _():
        y_ref[...] = jnp.zeros_like(y_ref)


def moe_experts(x_sorted, block_e, n_valid, w1, w3, w2, layer):
    rows, half = x_sorted.shape
    n_blocks = block_e.shape[0]
    bm = MOE_BM
    d, hid = w1.shape[2], w1.shape[3]
    grid_spec = pltpu.PrefetchScalarGridSpec(
        num_scalar_prefetch=2,
        grid=(n_blocks,),
        in_specs=[pl.BlockSpec((bm, half), lambda i, be, nv: (jnp.minimum(i, nv[0] - 1), 0)),
                  pl.BlockSpec((1, 1, d, hid), lambda i, be, nv: (layer, be[i], 0, 0)),
                  pl.BlockSpec((1, 1, d, hid), lambda i, be, nv: (layer, be[i], 0, 0)),
                  pl.BlockSpec((1, 1, hid, d), lambda i, be, nv: (layer, be[i], 0, 0))],
        out_specs=pl.BlockSpec((bm, half), lambda i, be, nv: (i, 0)),
        scratch_shapes=[pltpu.VMEM((d, hid), BF16), pltpu.VMEM((d, hid), BF16), pltpu.VMEM((hid, d), BF16)],
    )
    return pl.pallas_call(
        _expert_kernel,
        grid_spec=grid_spec,
        out_shape=jax.ShapeDtypeStruct((rows, half), jnp.uint32),
        compiler_params=pltpu.CompilerParams(dimension_semantics=("arbitrary",),
                                             vmem_limit_bytes=VMEM_LIMIT),
        name="moe_experts",
    )(block_e, n_valid, x_sorted, w1, w3, w2)


def _combine_ple_kernel(npieces_ref, rows_ref, y_hbm, h_ref, meta_ref, p_ref, g_ref, wg_ref, wp_ref, o_ref,
                        ys_ref, sems):
    tile = pl.program_id(0)
    n_tiles = pl.num_programs(0)
    tm, d = h_ref.shape
    slot = tile % 2

    def copies(t, s, wait):
        _piece_copies(npieces_ref, rows_ref, t, ys_ref.at[s], y_hbm, sems.at[s], False, wait)

    @pl.when(tile == 0)
    def _():
        ys_ref[...] = jnp.zeros_like(ys_ref)
        copies(0, 0, False)

    @pl.when(tile + 1 < n_tiles)
    def _():
        copies(tile + 1, 1 - slot, False)

    copies(tile, slot, True)
    meta = meta_ref[...]
    w0 = meta[:, META_W0:META_W0 + 1]
    w1 = meta[:, META_W1:META_W1 + 1]
    p0 = meta[:, META_P0:META_P0 + 1].astype(jnp.int32)
    p1 = meta[:, META_P1:META_P1 + 1].astype(jnp.int32)
    r = lax.broadcasted_iota(jnp.int32, (tm, SORT_ROWS), 1)
    wmat = jnp.where(r == p0, w0, jnp.where(r == p1, w1, 0.0)).astype(BF16)
    ylo, yhi = _unpack_halves(ys_ref[slot])
    mix = lambda y: jnp.dot(wmat, y, preferred_element_type=F32)
    half = d // 2
    h = jnp.concatenate([h_ref[:, :half] + mix(ylo), h_ref[:, half:] + mix(yhi)], axis=1)
    gate = _sigmoid(jnp.dot(_rms(h, g_ref[...]).astype(BF16), wg_ref[...], preferred_element_type=F32))
    proj = jnp.dot(p_ref[...].astype(BF16), wp_ref[...], preferred_element_type=F32)
    o_ref[...] = h + gate * proj


def moe_combine_ple(n_pieces, piece_rows, y, h, meta, p, layer, g, w_gate, w_proj):
    n, d = h.shape
    pd = p.shape[2]
    tm = ROUTER_TM
    grid_spec = pltpu.PrefetchScalarGridSpec(
        num_scalar_prefetch=2,
        grid=(n // tm,),
        in_specs=[pl.BlockSpec(memory_space=pl.ANY),
                  pl.BlockSpec((tm, d), lambda i, *_: (i, 0)),
                  pl.BlockSpec((tm, ROUTER_LANES), lambda i, *_: (i, 0)),
                  pl.BlockSpec((None, tm, pd), lambda i, *_: (layer, i, 0)),
                  pl.BlockSpec((1, d), lambda i, *_: (0, 0)),
                  pl.BlockSpec((d, d), lambda i, *_: (0, 0)),
                  pl.BlockSpec((pd, d), lambda i, *_: (0, 0))],
        out_specs=pl.BlockSpec((tm, d), lambda i, *_: (i, 0)),
        scratch_shapes=[pltpu.VMEM((2, SORT_ROWS, d // 2), jnp.uint32), pltpu.SemaphoreType.DMA((2,))],
    )
    return pl.pallas_call(
        _combine_ple_kernel,
        grid_spec=grid_spec,
        out_shape=jax.ShapeDtypeStruct((n, d), F32),
        compiler_params=pltpu.CompilerParams(dimension_semantics=("arbitrary",),
                                             vmem_limit_bytes=VMEM_LIMIT),
        name="moe_combine_ple",
    )(n_pieces, piece_rows, y, h, meta, p, g.reshape(1, d), w_gate, w_proj)


def outproj_moe_embedding(a, w_out, res, ffn_norm, w_rg, b_rg, w_re, b_re, w1, w3, w2, p, ple_norm, w_ple_gate,
                          w_ple_proj, layer):
    n, d = res.shape
    pad = ROUTER_LANES - MOE_GROUPS - MOE_EXPERTS
    w_router = jnp.concatenate([w_rg, w_re, jnp.zeros((d, pad), F32)], axis=1)
    b_router = jnp.concatenate([b_rg, b_re, jnp.zeros((pad,), F32)]).reshape(1, ROUTER_LANES)
    h, xn, meta, post, tabs, cnt = outproj_router(a, w_out, res, ffn_norm, w_router, b_router)

    bm = MOE_BM
    nt = n // ROUTER_TM
    lanes = slice(EXP_LANE0, EXP_LANE0 + MOE_EXPERTS)
    totals = cnt[0, lanes].astype(jnp.int32)
    region = (totals + bm - 1) // bm * bm
    region_end = jnp.cumsum(region)
    region_start = region_end - region
    n_blocks = -(-(2 * n + nt * MOE_EXPERTS * (SEG_ALIGN - 1)) // bm) + MOE_EXPERTS
    block_row0 = jnp.arange(n_blocks, dtype=jnp.int32) * bm
    block_e = jnp.minimum(jnp.sum((block_row0[:, None] >= region_end[None, :]).astype(jnp.int32), axis=1),
                          MOE_EXPERTS - 1).astype(jnp.int32)
    n_valid = (region_end[-1:] // bm).astype(jnp.int32)
    tabs = tabs.reshape(nt, 8, ROUTER_LANES)[:, :, lanes].astype(jnp.int32)
    tab_cnt, tab_off = tabs[:, TAB_CNT], tabs[:, TAB_OFF]
    tab_seg = tabs[:, TAB_SEG] + region_start[None, :]
    piece_row0 = jnp.arange(MAX_PIECES, dtype=jnp.int32) * SEG_ALIGN
    row0 = piece_row0[None, :, None]
    owner = ((tab_off[:, None, :] <= row0) & (row0 < (tab_off + tab_cnt)[:, None, :])).astype(jnp.int32)
    of_owner = lambda t: jnp.sum(owner * t[:, None, :], axis=2)
    piece_rows = (of_owner(tab_seg) + piece_row0[None, :] - of_owner(tab_off)).reshape(-1)
    n_pieces = jnp.sum(tab_cnt, axis=1) // SEG_ALIGN

    x_sorted = moe_dispatch(n_pieces, piece_rows, region - totals, region_start + totals, n_valid, post, xn,
                            n_blocks * bm)
    y = moe_experts(x_sorted, block_e, n_valid, w1, w3, w2, layer)
    return moe_combine_ple(n_pieces, piece_rows, y, h, meta, p, layer, ple_norm, w_ple_gate, w_ple_proj)


def kernel(x, p, a_norm, a_w_in, a_conv, a_A_log, a_dt_bias, a_o_norm, a_w_out, kv_norm, w_kv, k_norm, b_norm, b_w_q, b_q_norm, b_w_out, ffn_norm, w_router_group, b_router_group, w_router_expert, b_router_expert, w1, w3, w2, ple_norm, w_ple_gate, w_ple_proj):
    b, s, d = x.shape
    n = b * s
    depth = p.shape[0]
    n_a = a_norm.shape[0]
    h = x.reshape(n, d)
    p_rows = p.reshape(depth, n, -1)
    kv = None
    for i in range(depth):
        if i < n_a:
            w_in = a_w_in[i]
            proj, gates_t = dn_inproj(h, a_norm[i], w_in[:, :DN_MAIN].astype(BF16), w_in[:, DN_MAIN:].T)
            o = deltanet(proj.reshape(b, s, DN_MAIN), gates_t.reshape(2 * DN_HEADS, b, s // DN_CHUNK, DN_CHUNK),
                         a_conv[i], a_A_log[i], a_dt_bias[i], a_o_norm[i])
            mixed, w_out = o.reshape(n, DN_V), a_w_out[i]
        else:
            bl = i - n_a
            q = norm_matmul(h, b_norm[bl], b_w_q[bl].astype(BF16), BF16)
            o = dilated_attention(q.reshape(b, s, -1), kv.reshape(b, s, -1), b_q_norm[bl], k_norm)
            mixed, w_out = o.reshape(n, GROUP_WIDTH), b_w_out[bl]
        h = outproj_moe_embedding(mixed, w_out.astype(BF16), h, ffn_norm[i], w_router_group[i], b_router_group[i],
                                  w_router_expert[i], b_router_expert[i], w1, w3, w2, p_rows, ple_norm[i],
                                  w_ple_gate[i].astype(BF16), w_ple_proj[i].astype(BF16), i)
        if i == n_a - 1:
            kv = norm_matmul(h, kv_norm, w_kv.astype(BF16), BF16)
    return h.reshape(b, s, d)
```
